```python
import jax, jax.numpy as jnp
from jax import lax
import numpy as np

D_MODEL = 1024
BATCH = 32
SEQ = 256
DEPTH = 1
DEC_BATCH = 2
DEC_SEQ = 2048
PAST_LEN = 512

GRID_W = 64
N_RET_HEADS = 4
RET_DK = 256
RET_DV = 256
D_RET = N_RET_HEADS * RET_DV
D_QK = N_RET_HEADS * RET_DK
D_CONV = 1024
CONV_WIDTH = 31
D_FF = 2816
FFN_CONV_WIDTH = 3
CHUNK = 128
ROPE_BASE = 10000.0
EPS = 1e-6
GN_EPS = 1e-5
IN_COLS = 2 * D_QK + 2 * D_RET + 2 * D_CONV + 2 * D_MODEL

kernel_name = 'retention_conformer_convffn_dit_step'


def rms_norm(x, w):
    xf = x.astype(jnp.float32)
    y = xf * lax.rsqrt(jnp.mean(xf * xf, axis=-1, keepdims=True) + EPS)
    return (y * w.astype(jnp.float32)).astype(x.dtype)


def layer_norm(x, w, b):
    xf = x.astype(jnp.float32)
    mu = jnp.mean(xf, axis=-1, keepdims=True)
    var = jnp.mean(jnp.square(xf - mu), axis=-1, keepdims=True)
    y = (xf - mu) * lax.rsqrt(var + GN_EPS)
    return (y * w.astype(jnp.float32) + b.astype(jnp.float32)).astype(x.dtype)


def head_group_norm(o, w):
    mu = jnp.mean(o, axis=-1, keepdims=True)
    var = jnp.mean(jnp.square(o - mu), axis=-1, keepdims=True)
    y = (o - mu) * lax.rsqrt(var + GN_EPS)
    b, t = o.shape[0], o.shape[1]
    return y.reshape(b, t, D_RET) * w.astype(jnp.float32)


def depthwise_conv(x, w, bias):
    width, ch = w.shape
    y = lax.conv_general_dilated(
        x, w[:, None, :].astype(x.dtype), window_strides=(1,),
        padding=[(width // 2, width // 2)],
        dimension_numbers=('NWC', 'WIO', 'NWC'), feature_group_count=ch)
    return y + bias.astype(x.dtype)


def adaln(cond, w, b):
    return jax.nn.silu(cond.astype(jnp.float32)) @ w.astype(jnp.float32) + b.astype(jnp.float32)


def axial_rope(x, rows):
    d = x.shape[-1]
    nf = d // 4
    r = jnp.repeat(jnp.arange(rows), GRID_W).astype(jnp.float32)
    col = jnp.tile(jnp.arange(GRID_W), rows).astype(jnp.float32)
    inv = ROPE_BASE ** (-jnp.arange(nf, dtype=jnp.float32) / nf)

    def rot(xh, pos):
        ang = pos[:, None] * inv[None, :]
        cos = jnp.cos(ang)[None, :, None, :]
        sin = jnp.sin(ang)[None, :, None, :]
        x1, x2 = xh[..., :nf], xh[..., nf:]
        return jnp.concatenate([x1 * cos - x2 * sin, x2 * cos + x1 * sin], axis=-1)

    xf = x.astype(jnp.float32)
    out = jnp.concatenate([rot(xf[..., :2 * nf], r), rot(xf[..., 2 * nf:], col)], axis=-1)
    return out.astype(x.dtype)


def retention_dir(q, k, v, log_gamma, s0):
    b, t, h, dk = q.shape
    dv = v.shape[-1]
    n = t // CHUNK
    qc = q.astype(jnp.float32).reshape(b, n, CHUNK, h, dk)
    kc = k.astype(jnp.float32).reshape(b, n, CHUNK, h, dk)
    vc = v.astype(jnp.float32).reshape(b, n, CHUNK, h, dv)
    idx = jnp.arange(CHUNK, dtype=jnp.float32)
    rel = idx[:, None] - idx[None, :]
    decay_in = jnp.where(rel[None] >= 0,
                         jnp.exp(jnp.maximum(rel, 0.0)[None] * log_gamma[:, None, None]), 0.0)
    scores = jnp.einsum('bnqhd,bnkhd->bnhqk', qc, kc) * decay_in[None, None]
    inner = jnp.einsum('bnhqk,bnkhe->bnqhe', scores, vc)
    q_dec = jnp.exp((idx + 1.0)[:, None] * log_gamma[None, :])
    k_dec = jnp.exp((CHUNK - 1.0 - idx)[:, None] * log_gamma[None, :])
    chunk_dec = jnp.exp(CHUNK * log_gamma)

    def step(s, xs):
        qi, ki, vi = xs
        cross = jnp.einsum('bqhd,bhde->bqhe', qi * q_dec[None, :, :, None], s)
        s = s * chunk_dec[None, :, None, None] + jnp.einsum(
            'bkhd,bkhe->bhde', ki * k_dec[None, :, :, None], vi)
        return s, cross

    s_final, cross = lax.scan(step, s0.astype(jnp.float32),
                              (jnp.moveaxis(qc, 1, 0), jnp.moveaxis(kc, 1, 0), jnp.moveaxis(vc, 1, 0)))
    out = inner + jnp.moveaxis(cross, 0, 1)
    return out.reshape(b, t, h, dv), s_final


def bidirectional_retention(q, k, v, lg_f, lg_b, s_f0, s_b0):
    o_f, s_f = retention_dir(q, k, v, lg_f, s_f0)
    o_b, s_b = retention_dir(q[:, ::-1], k[:, ::-1], v[:, ::-1], lg_b, s_b0)
    return o_f + o_b[:, ::-1], s_f, s_b


def trunk_layer(x, mod, rows, s_f0, s_b0, n1_pre, n1_post, n2_pre, n2_post, w_in, dec_f, dec_b,
                gn_w, ret_w_out, cdw_w, cdw_b, cln_w, cln_b, conv_w_out, w_mix_out,
                ffn_w_up, fdw_w, fdw_b, ffn_w_down):
    b, t, _ = x.shape
    mod = mod.astype(x.dtype)
    sh1, sc1, g1, sh2, sc2, g2 = jnp.split(mod, 6, axis=-1)
    h = rms_norm(x, n1_pre) * (1 + sc1) + sh1
    proj = h @ w_in
    cuts = [D_QK, 2 * D_QK, 2 * D_QK + D_RET, 2 * D_QK + 2 * D_RET,
            2 * D_QK + 2 * D_RET + D_CONV, 2 * D_QK + 2 * D_RET + 2 * D_CONV,
            2 * D_QK + 2 * D_RET + 2 * D_CONV + D_MODEL]
    q, k, v, g_ret, u_a, u_b, gate_ret, gate_conv = jnp.split(proj, cuts, axis=-1)
    q = q.reshape(b, t, N_RET_HEADS, RET_DK)
    k = k.reshape(b, t, N_RET_HEADS, RET_DK)
    v = v.reshape(b, t, N_RET_HEADS, RET_DV)
    if rows is not None:
        q = axial_rope(q, rows)
        k = axial_rope(k, rows)
    q = q * (RET_DK ** -0.5)
    lg_f = jax.nn.log_sigmoid(dec_f.astype(jnp.float32))
    lg_b = jax.nn.log_sigmoid(dec_b.astype(jnp.float32))
    o, s_f, s_b = bidirectional_retention(q, k, v, lg_f, lg_b, s_f0, s_b0)
    o = head_group_norm(o, gn_w).astype(x.dtype)
    ret_out = (jax.nn.silu(g_ret) * o) @ ret_w_out
    u = u_a * jax.nn.sigmoid(u_b)
    u = depthwise_conv(u, cdw_w, cdw_b)
    u = jax.nn.silu(layer_norm(u, cln_w, cln_b))
    conv_out = u @ conv_w_out
    mixed = jax.nn.sigmoid(gate_ret) * ret_out + jax.nn.sigmoid(gate_conv) * conv_out
    x = x + g1 * rms_norm(mixed @ w_mix_out, n1_post)
    h2 = rms_norm(x, n2_pre) * (1 + sc2) + sh2
    up = depthwise_conv(h2 @ ffn_w_up, fdw_w, fdw_b)
    a, gl = jnp.split(up, 2, axis=-1)
    x = x + g2 * rms_norm((jax.nn.silu(a) * gl) @ ffn_w_down, n2_post)
    return x, s_f, s_b


def setup_inputs(seed: int = 0) -> dict:
    key = jax.random.key(seed)
    ks = jax.random.split(key, 32)

    def nrm(k, shape, scale):
        return jax.random.normal(k, shape, jnp.float32) * scale

    gamma = 1.0 - 2.0 ** (-5.0 - np.arange(N_RET_HEADS))
    base_logit = jnp.asarray(np.log(gamma) - np.log1p(-gamma), dtype=jnp.float32)
    st_shape = (DEC_BATCH, DEPTH, N_RET_HEADS, RET_DK, RET_DV)
    return {
        'x_prompt': nrm(ks[0], (BATCH, SEQ, D_MODEL), 1.0),
        'x_sample': nrm(ks[1], (DEC_BATCH, DEC_SEQ, D_MODEL), 1.0),
        'state_ret_fwd': nrm(ks[2], st_shape, 0.5),
        'state_ret_bwd': nrm(ks[3], st_shape, 0.5),
        'c': nrm(ks[4], (DEC_BATCH, D_MODEL), 1.0),
        'c_ctx': nrm(ks[5], (D_MODEL,), 1.0),
        'norm1_pre': 1.0 + nrm(ks[6], (DEPTH, D_MODEL), 0.1),
        'norm1_post': 1.0 + nrm(ks[7], (DEPTH, D_MODEL), 0.1),
        'norm2_pre': 1.0 + nrm(ks[8], (DEPTH, D_MODEL), 0.1),
        'norm2_post': 1.0 + nrm(ks[9], (DEPTH, D_MODEL), 0.1),
        'ada_w': nrm(ks[10], (DEPTH, D_MODEL, 6 * D_MODEL), D_MODEL ** -0.5),
        'ada_b': nrm(ks[11], (DEPTH, 6 * D_MODEL), 0.02),
        'w_in': nrm(ks[12], (DEPTH, D_MODEL, IN_COLS), D_MODEL ** -0.5),
        'ret_decay_fwd': base_logit[None] + nrm(ks[13], (DEPTH, N_RET_HEADS), 0.1),
        'ret_decay_bwd': base_logit[None] + nrm(ks[14], (DEPTH, N_RET_HEADS), 0.1),
        'ret_gn_w': 1.0 + nrm(ks[15], (DEPTH, D_RET), 0.1),
        'ret_w_out': nrm(ks[16], (DEPTH, D_RET, D_MODEL), D_RET ** -0.5),
        'conv_dw_w': nrm(ks[17], (DEPTH, CONV_WIDTH, D_CONV), CONV_WIDTH ** -0.5),
        'conv_dw_b': nrm(ks[18], (DEPTH, D_CONV), 0.02),
        'conv_ln_w': 1.0 + nrm(ks[19], (DEPTH, D_CONV), 0.1),
        'conv_ln_b': nrm(ks[20], (DEPTH, D_CONV), 0.02),
        'conv_w_out': nrm(ks[21], (DEPTH, D_CONV, D_MODEL), D_CONV ** -0.5),
        'w_mix_out': nrm(ks[22], (DEPTH, D_MODEL, D_MODEL), D_MODEL ** -0.5),
        'ffn_w_up': nrm(ks[23], (DEPTH, D_MODEL, 2 * D_FF), D_MODEL ** -0.5),
        'ffn_dw_w': nrm(ks[24], (DEPTH, FFN_CONV_WIDTH, 2 * D_FF), FFN_CONV_WIDTH ** -0.5),
        'ffn_dw_b': nrm(ks[25], (DEPTH, 2 * D_FF), 0.02),
        'ffn_w_down': nrm(ks[26], (DEPTH, D_FF, D_MODEL), D_FF ** -0.5),
    }


def reference(x_prompt, x_sample, state_ret_fwd, state_ret_bwd, c, c_ctx,
              norm1_pre, norm1_post, norm2_pre, norm2_post, ada_w, ada_b, w_in,
              ret_decay_fwd, ret_decay_bwd, ret_gn_w, ret_w_out,
              conv_dw_w, conv_dw_b, conv_ln_w, conv_ln_b, conv_w_out, w_mix_out,
              ffn_w_up, ffn_dw_w, ffn_dw_b, ffn_w_down):
    rows = x_sample.shape[1] // GRID_W
    xp, xs = x_prompt, x_sample
    zero_state = jnp.zeros((x_prompt.shape[0], N_RET_HEADS, RET_DK, RET_DV), jnp.float32)
    new_f, new_b = [], []
    for l in range(DEPTH):
        lw = (norm1_pre[l], norm1_post[l], norm2_pre[l], norm2_post[l], w_in[l],
              ret_decay_fwd[l], ret_decay_bwd[l], ret_gn_w[l], ret_w_out[l],
              conv_dw_w[l], conv_dw_b[l], conv_ln_w[l], conv_ln_b[l], conv_w_out[l],
              w_mix_out[l], ffn_w_up[l], ffn_dw_w[l], ffn_dw_b[l], ffn_w_down[l])
        mod_ctx = adaln(c_ctx[None], ada_w[l], ada_b[l])[:, None, :]
        mod_lat = adaln(c, ada_w[l], ada_b[l])[:, None, :]
        xp, s_f, s_b = trunk_layer(xp, mod_ctx, None, zero_state, zero_state, *lw)
        new_f.append(s_f)
        new_b.append(s_b)
        xs, _, _ = trunk_layer(xs, mod_lat, rows, state_ret_fwd[:, l], state_ret_bwd[:, l], *lw)
    new_state_ret_fwd = jnp.stack(new_f, axis=1).astype(x_prompt.dtype)
    new_state_ret_bwd = jnp.stack(new_b, axis=1).astype(x_prompt.dtype)
    return (xp, xs, new_state_ret_fwd, new_state_ret_bwd)
```

```python
import functools

import numpy as np
import jax
import jax.numpy as jnp
from jax import lax
from jax.experimental import pallas as pl
from jax.experimental.pallas import tpu as pltpu

F32 = jnp.float32
BF16 = jnp.bfloat16

D_MODEL = 1024
N_HEADS = 4
D_HEAD = 256
D_FF = 2816
CONV_WIDTH = 31
FFN_CONV_WIDTH = 3
GRID_W = 64
ROPE_BASE = 10000.0
EPS = 1e-6
GN_EPS = 1e-5
RET_CHUNK = 256
HALO = 16
CONV_ROWS = 16
VMEM_LIMIT = 56 * 1024 * 1024


def _sigmoid(x):
    return 1.0 / (1.0 + jnp.exp(-x))


def _rms(x, w):
    ms = jnp.mean(x * x, axis=-1, keepdims=True)
    return (x * lax.rsqrt(ms + EPS)) * w


def _params(sem):
    return pltpu.CompilerParams(dimension_semantics=sem, vmem_limit_bytes=VMEM_LIMIT)


def _resident(shape):
    nd = len(shape)
    return pl.BlockSpec(shape, lambda *_: (0,) * nd, pipeline_mode=pl.Buffered(1))


def _adaln_kernel(cond_ref, w_ref, b_ref, o_ref):
    c = cond_ref[...]
    s = (c * _sigmoid(c)).astype(BF16)
    o_ref[0] = jnp.dot(s, w_ref[...].astype(BF16), preferred_element_type=F32) + b_ref[0]


def _adaln(cond8, ada_w, ada_b):
    return pl.pallas_call(
        _adaln_kernel,
        grid=(6,),
        in_specs=[pl.BlockSpec((8, D_MODEL), lambda j: (0, 0)),
                  pl.BlockSpec((D_MODEL, D_MODEL), lambda j: (0, j)),
                  pl.BlockSpec((1, 1, D_MODEL), lambda j: (j, 0, 0))],
        out_specs=pl.BlockSpec((1, 8, D_MODEL), lambda j: (j, 0, 0)),
        out_shape=jax.ShapeDtypeStruct((6, 8, D_MODEL), F32),
        compiler_params=_params(("arbitrary",)),
        name="adaln",
    )(cond8, ada_w, ada_b.reshape(6, 1, D_MODEL))


def _decay_kernel(df_ref, db_ref, d_ref, kf_ref, kb_ref, qf_ref, qb_ref, cf_ref, cb_ref):
    def log_sigmoid(x):
        return -(jnp.maximum(-x, 0.0) + jnp.log1p(jnp.exp(-jnp.abs(x))))

    lgf = log_sigmoid(df_ref[...])
    lgb = log_sigmoid(db_ref[...])
    c = RET_CHUNK
    row = lax.broadcasted_iota(jnp.int32, (c, 128), 0).astype(F32)
    col = lax.broadcasted_iota(jnp.int32, (c, 128), 1).astype(F32)
    for h in range(N_HEADS):
        f = lgf[h:h + 1, :]
        b = lgb[h:h + 1, :]
        for half in range(2):
            sl = slice(half * 128, (half + 1) * 128)
            rel = row - (col + 128.0 * half)
            fwd = jnp.exp(jnp.maximum(rel, 0.0) * f)
            bwd = jnp.exp(jnp.maximum(-rel, 0.0) * b)
            d_ref[h, :, sl] = jnp.where(rel > 0, fwd, jnp.where(rel < 0, bwd, 2.0))
            kf_ref[h, :, sl] = jnp.exp((c - 1.0 - row) * f)
            kb_ref[h, :, sl] = jnp.exp(row * b)
            qf_ref[h, :, sl] = jnp.exp((row + 1.0) * f)
            qb_ref[h, :, sl] = jnp.exp((c - row) * b)
            cf_ref[h, :, sl] = jnp.exp(jnp.broadcast_to(c * f, (8, 128)))
            cb_ref[h, :, sl] = jnp.exp(jnp.broadcast_to(c * b, (8, 128)))


def _decay_tables(dec_f, dec_b):
    def lanes(d):
        return jnp.zeros((8, 128), F32).at[:N_HEADS].set(jnp.broadcast_to(d[:, None], (N_HEADS, 128)))

    big = jax.ShapeDtypeStruct((N_HEADS, RET_CHUNK, D_HEAD), F32)
    small = jax.ShapeDtypeStruct((N_HEADS, 8, D_HEAD), F32)
    return pl.pallas_call(
        _decay_kernel,
        out_shape=(big, big, big, big, big, small, small),
        name="decay_tables",
    )(lanes(dec_f), lanes(dec_b))


def _rope_tables(seq):
    rows = seq // GRID_W
    nf = D_HEAD // 4
    r = np.repeat(np.arange(rows), GRID_W).astype(np.float64)
    c = np.tile(np.arange(GRID_W), rows).astype(np.float64)
    inv = ROPE_BASE ** (-np.arange(nf, dtype=np.float64) / nf)
    ar, ac = r[:, None] * inv[None, :], c[:, None] * inv[None, :]
    cos = np.concatenate([np.cos(ar), np.cos(ar), np.cos(ac), np.cos(ac)], axis=1)
    sin = np.concatenate([-np.sin(ar), np.sin(ar), -np.sin(ac), np.sin(ac)], axis=1)
    return jnp.asarray(cos, F32), jnp.asarray(sin, F32)


def _proj_kernel(*refs, rope, tiles_per_seq):
    if rope:
        x_ref, mod_ref, n1_ref, w_ref, cos_ref, sin_ref = refs[:6]
        outs = refs[6:]
    else:
        x_ref, mod_ref, n1_ref, w_ref = refs[:4]
        outs = refs[4:]
    q_ref, k_ref, v_ref, sg_ref, u_ref, gr_ref, gc_ref = outs
    r = (1 + pl.program_id(0) // tiles_per_seq) if rope else 0
    sh1 = mod_ref[0, pl.ds(r, 1), :]
    sc1 = mod_ref[1, pl.ds(r, 1), :]
    h = _rms(x_ref[...], n1_ref[...]) * (1.0 + sc1) + sh1
    hb = h.astype(BF16)

    def proj(g):
        return jnp.dot(hb, w_ref[:, g * D_MODEL:(g + 1) * D_MODEL], preferred_element_type=F32)

    def store_rot(dst, a, scale):
        for j in range(D_MODEL // 128):
            slab = a[:, j * 128:(j + 1) * 128]
            if rope:
                tsl = slice((j % 2) * 128, (j % 2 + 1) * 128)
                slab = slab * cos_ref[:, tsl] + pltpu.roll(slab, 64, axis=1) * sin_ref[:, tsl]
            if scale != 1.0:
                slab = slab * scale
            dst[:, j * 128:(j + 1) * 128] = slab.astype(BF16)

    store_rot(q_ref, proj(0), D_HEAD ** -0.5)
    store_rot(k_ref, proj(1), 1.0)
    v_ref[...] = proj(2).astype(BF16)
    g = proj(3)
    sg_ref[...] = (g * _sigmoid(g)).astype(BF16)
    ub = _sigmoid(proj(5))
    u_ref[...] = (proj(4) * ub).astype(BF16)
    gr_ref[...] = _sigmoid(proj(6)).astype(BF16)
    gc_ref[...] = _sigmoid(proj(7)).astype(BF16)


def _proj(x, mod, n1_pre, w_in, *, rope, seq, tm):
    t = x.shape[0]
    tiles_per_seq = seq // tm
    tile = pl.BlockSpec((tm, D_MODEL), lambda i: (i, 0))
    in_specs = [tile, _resident(mod.shape), _resident((1, D_MODEL)), _resident(w_in.shape)]
    args = [x, mod, n1_pre, w_in]
    if rope:
        cos, sin = _rope_tables(seq)
        tab = pl.BlockSpec((tm, D_HEAD), lambda i: (i % tiles_per_seq, 0))
        in_specs += [tab, tab]
        args += [cos, sin]
    out = jax.ShapeDtypeStruct((t, D_MODEL), BF16)
    return pl.pallas_call(
        functools.partial(_proj_kernel, rope=rope, tiles_per_seq=tiles_per_seq),
        grid=(t // tm,),
        in_specs=in_specs,
        out_specs=[tile] * 7,
        out_shape=[out] * 7,
        compiler_params=_params(("arbitrary",)),
        name="proj_lat" if rope else "proj_ctx",
    )(*args)


def _dot_nt(a, b):
    return lax.dot_general(a, b, (((1,), (1,)), ((), ())), preferred_element_type=F32)


def _dot_tn(a, b):
    return lax.dot_general(a, b, (((0,), (0,)), ((), ())), preferred_element_type=F32)


def _gn_gate(o, gnw, sg):
    mu = jnp.mean(o, axis=-1, keepdims=True)
    d = o - mu
    var = jnp.mean(d * d, axis=-1, keepdims=True)
    y = (d * lax.rsqrt(var + GN_EPS)) * gnw
    return (sg.astype(F32) * y).astype(BF16)


def _ret_ctx_kernel(q_ref, k_ref, v_ref, sg_ref, gnw_ref, d_ref, kf_ref, kb_ref,
                    ro_ref, sf_ref, sb_ref):
    for h in range(N_HEADS):
        sl = slice(h * D_HEAD, (h + 1) * D_HEAD)
        q, k, v = q_ref[:, sl], k_ref[:, sl], v_ref[:, sl]
        p = (_dot_nt(q, k) * d_ref[h]).astype(BF16)
        o = jnp.dot(p, v, preferred_element_type=F32)
        ro_ref[:, sl] = _gn_gate(o, gnw_ref[:, sl], sg_ref[:, sl])
        kf = k.astype(F32)
        sf_ref[0, 0, h] = _dot_tn((kf * kf_ref[h]).astype(BF16), v)
        sb_ref[0, 0, h] = _dot_tn((kf * kb_ref[h]).astype(BF16), v)


def _ret_ctx(q, k, v, sg, gn_w, tabs, batch, seq):
    assert seq == RET_CHUNK
    d, kf, kb = tabs[0], tabs[1], tabs[2]
    tile = pl.BlockSpec((seq, D_MODEL), lambda i: (i, 0))
    st = pl.BlockSpec((1, 1, N_HEADS, D_HEAD, D_HEAD), lambda i: (i, 0, 0, 0, 0))
    st_shape = jax.ShapeDtypeStruct((batch, 1, N_HEADS, D_HEAD, D_HEAD), F32)
    return pl.pallas_call(
        _ret_ctx_kernel,
        grid=(batch,),
        in_specs=[tile, tile, tile, tile, _resident((1, D_MODEL)),
                  _resident(d.shape), _resident(kf.shape), _resident(kb.shape)],
        out_specs=[tile, st, st],
        out_shape=[jax.ShapeDtypeStruct((batch * seq, D_MODEL), BF16), st_shape, st_shape],
        compiler_params=_params(("arbitrary",)),
        name="ret_ctx",
    )(q, k, v, sg, gn_w, d, kf, kb)


def _ret_lat_kernel(q_ref, k_ref, v_ref, sg_ref, gnw_ref, sf0_ref, sb0_ref,
                    d_ref, kf_ref, kb_ref, qf_ref, qb_ref, cf_ref, cb_ref,
                    ro_ref, sbs_ref, *, n_chunks):
    c = RET_CHUNK

    def rows(j):
        return pl.ds(pl.multiple_of(j * c, c), c)

    def bwd(jj, s):
        j = n_chunks - 1 - jj
        sbs_ref[j] = s
        kb = (k_ref[rows(j), :].astype(F32) * kb_ref[0]).astype(BF16)
        return s * cb_ref[0, 0:1, :] + _dot_tn(kb, v_ref[rows(j), :])

    lax.fori_loop(0, n_chunks, bwd, sb0_ref[0, 0, 0])

    def fwd(j, s):
        q, k, v = q_ref[rows(j), :], k_ref[rows(j), :], v_ref[rows(j), :]
        qf32, kf32 = q.astype(F32), k.astype(F32)
        p = (_dot_nt(q, k) * d_ref[0]).astype(BF16)
        o = jnp.dot(p, v, preferred_element_type=F32)
        o = o + jnp.dot((qf32 * qf_ref[0]).astype(BF16), s.astype(BF16), preferred_element_type=F32)
        o = o + jnp.dot((qf32 * qb_ref[0]).astype(BF16), sbs_ref[j].astype(BF16),
                        preferred_element_type=F32)
        ro_ref[rows(j), :] = _gn_gate(o, gnw_ref[...], sg_ref[rows(j), :])
        return s * cf_ref[0, 0:1, :] + _dot_tn((kf32 * kf_ref[0]).astype(BF16), v)

    lax.fori_loop(0, n_chunks, fwd, sf0_ref[0, 0, 0])


def _ret_lat(q, k, v, sg, gn_w, sf0, sb0, tabs, batch, seq):
    n_chunks = seq // RET_CHUNK
    tile = pl.BlockSpec((seq, D_HEAD), lambda s, h: (s, h))
    st = pl.BlockSpec((1, 1, 1, D_HEAD, D_HEAD), lambda s, h: (s, 0, h, 0, 0))
    big = pl.BlockSpec((1, RET_CHUNK, D_HEAD), lambda s, h: (h, 0, 0))
    small = pl.BlockSpec((1, 8, D_HEAD), lambda s, h: (h, 0, 0))
    return pl.pallas_call(
        functools.partial(_ret_lat_kernel, n_chunks=n_chunks),
        grid=(batch, N_HEADS),
        in_specs=[tile, tile, tile, tile, pl.BlockSpec((1, D_HEAD), lambda s, h: (0, h)),
                  st, st, big, big, big, big, big, small, small],
        out_specs=tile,
        out_shape=jax.ShapeDtypeStruct((batch * seq, D_MODEL), BF16),
        scratch_shapes=[pltpu.VMEM((n_chunks, D_HEAD, D_HEAD), F32)],
        compiler_params=_params(("arbitrary", "arbitrary")),
        name="ret_lat",
    )(q, k, v, sg, gn_w, sf0, sb0, *tabs)


def _mixer_kernel(x_ref, ro_ref, u_ref, up_ref, un_ref, gr_ref, gc_ref, mod_ref,
                  wro_ref, wco_ref, wmx_ref, cw_ref, cb_ref, lnw_ref, lnb_ref, n1p_ref, n2_ref,
                  x1_ref, h2_ref, pad_ref, uc_ref, *, tm, tiles_per_seq, per_seq_mod):
    i = pl.program_id(0)
    r = (1 + i // tiles_per_seq) if per_seq_mod else 0
    g1 = mod_ref[2, pl.ds(r, 1), :]
    sh2 = mod_ref[3, pl.ds(r, 1), :]
    sc2 = mod_ref[4, pl.ds(r, 1), :]
    first = (i % tiles_per_seq) == 0
    last = (i % tiles_per_seq) == tiles_per_seq - 1

    pad_ref[0:HALO, :] = jnp.where(first, 0.0, up_ref[...].astype(F32))
    pad_ref[HALO:HALO + tm, :] = u_ref[...].astype(F32)
    pad_ref[HALO + tm:, :] = jnp.where(last, 0.0, un_ref[...].astype(F32))

    off = HALO - CONV_WIDTH // 2

    def conv_rows(rb, carry):
        base = pl.multiple_of(rb * CONV_ROWS, CONV_ROWS)
        acc = jnp.broadcast_to(cb_ref[...], (CONV_ROWS, D_MODEL))
        win = pad_ref[pl.ds(base, CONV_ROWS + 2 * HALO), :]
        for j in range(CONV_WIDTH):
            acc = acc + win[j + off:j + off + CONV_ROWS, :] * cw_ref[j:j + 1, :]
        uc_ref[pl.ds(base, CONV_ROWS), :] = acc
        return carry

    lax.fori_loop(0, tm // CONV_ROWS, conv_rows, 0)

    uc = uc_ref[...]
    mu = jnp.mean(uc, axis=-1, keepdims=True)
    d = uc - mu
    var = jnp.mean(d * d, axis=-1, keepdims=True)
    ln = (d * lax.rsqrt(var + GN_EPS)) * lnw_ref[...] + lnb_ref[...]
    cu = (ln * _sigmoid(ln)).astype(BF16)
    conv_out = jnp.dot(cu, wco_ref[...], preferred_element_type=F32)
    ret_out = jnp.dot(ro_ref[...], wro_ref[...], preferred_element_type=F32)
    mixed = gr_ref[...].astype(F32) * ret_out + gc_ref[...].astype(F32) * conv_out
    m = jnp.dot(mixed.astype(BF16), wmx_ref[...], preferred_element_type=F32)
    x1 = x_ref[...] + g1 * _rms(m, n1p_ref[...])
    x1_ref[...] = x1
    h2_ref[...] = (_rms(x1, n2_ref[...]) * (1.0 + sc2) + sh2).astype(BF16)


def _mixer(x, ro, u, gr, gc, mod, wro, wco, wmx, cw, cb, lnw, lnb, n1p, n2, *, seq, tm, per_seq_mod):
    t = x.shape[0]
    tiles_per_seq = seq // tm
    hb = tm // HALO
    n_hb = t // HALO
    tile = pl.BlockSpec((tm, D_MODEL), lambda i: (i, 0))
    prev = pl.BlockSpec((HALO, D_MODEL), lambda i: (jnp.maximum(i * hb - 1, 0), 0))
    nxt = pl.BlockSpec((HALO, D_MODEL), lambda i: (jnp.minimum((i + 1) * hb, n_hb - 1), 0))
    vec = _resident((1, D_MODEL))
    mat = _resident((D_MODEL, D_MODEL))
    return pl.pallas_call(
        functools.partial(_mixer_kernel, tm=tm, tiles_per_seq=tiles_per_seq, per_seq_mod=per_seq_mod),
        grid=(t // tm,),
        in_specs=[tile, tile, tile, prev, nxt, tile, tile, _resident(mod.shape),
                  mat, mat, mat, _resident(cw.shape), vec, vec, vec, vec, vec],
        out_specs=[tile, tile],
        out_shape=[jax.ShapeDtypeStruct((t, D_MODEL), F32), jax.ShapeDtypeStruct((t, D_MODEL), BF16)],
        scratch_shapes=[pltpu.VMEM((tm + 2 * HALO, D_MODEL), F32), pltpu.VMEM((tm, D_MODEL), F32)],
        compiler_params=_params(("arbitrary",)),
        name="mixer_lat" if per_seq_mod else "mixer_ctx",
    )(x, ro, u, u, u, gr, gc, mod, wro, wco, wmx, cw, cb, lnw, lnb, n1p, n2)


def _ffn_kernel(x1_ref, h2_ref, hp_ref, hn_ref, mod_ref, wup_ref, fw_ref, fb_ref, wdn_ref, n2p_ref,
                y_ref, he_ref, upe_ref, act_ref, *, tm, tiles_per_seq, per_seq_mod):
    i = pl.program_id(0)
    r = (1 + i // tiles_per_seq) if per_seq_mod else 0
    g2 = mod_ref[5, pl.ds(r, 1), :]
    first = (i % tiles_per_seq) == 0
    last = (i % tiles_per_seq) == tiles_per_seq - 1

    he_ref[0:HALO, :] = jnp.where(first, jnp.zeros((), BF16), hp_ref[...])
    he_ref[HALO:HALO + tm, :] = h2_ref[...]
    he_ref[HALO + tm:, :] = jnp.where(last, jnp.zeros((), BF16), hn_ref[...])
    upe_ref[...] = jnp.dot(he_ref[...], wup_ref[...], preferred_element_type=F32)

    def rows(rb, carry):
        base = pl.multiple_of(rb * CONV_ROWS, CONV_ROWS)
        c = fb_ref[...]
        win = upe_ref[pl.ds(pl.multiple_of(base + HALO - 8, 8), CONV_ROWS + 16), :]
        for j in range(FFN_CONV_WIDTH):
            c = c + win[7 + j:7 + j + CONV_ROWS, :] * fw_ref[j:j + 1, :]
        a, gl = c[:, :D_FF], c[:, D_FF:]
        act_ref[pl.ds(base, CONV_ROWS), :] = ((a * _sigmoid(a)) * gl).astype(BF16)
        return carry

    lax.fori_loop(0, tm // CONV_ROWS, rows, 0)

    f = jnp.dot(act_ref[...], wdn_ref[...], preferred_element_type=F32)
    y_ref[...] = x1_ref[...] + g2 * _rms(f, n2p_ref[...])


def _ffn(x1, h2, mod, wup, fw, fb, wdn, n2p, *, seq, tm, per_seq_mod):
    t = x1.shape[0]
    tiles_per_seq = seq // tm
    hb = tm // HALO
    n_hb = t // HALO
    tile = pl.BlockSpec((tm, D_MODEL), lambda i: (i, 0))
    prev = pl.BlockSpec((HALO, D_MODEL), lambda i: (jnp.maximum(i * hb - 1, 0), 0))
    nxt = pl.BlockSpec((HALO, D_MODEL), lambda i: (jnp.minimum((i + 1) * hb, n_hb - 1), 0))
    return pl.pallas_call(
        functools.partial(_ffn_kernel, tm=tm, tiles_per_seq=tiles_per_seq, per_seq_mod=per_seq_mod),
        grid=(t // tm,),
        in_specs=[tile, tile, prev, nxt, _resident(mod.shape), _resident(wup.shape),
                  _resident(fw.shape), _resident(fb.shape), _resident(wdn.shape), _resident((1, D_MODEL))],
        out_specs=tile,
        out_shape=jax.ShapeDtypeStruct((t, D_MODEL), F32),
        scratch_shapes=[pltpu.VMEM((tm + 2 * HALO, D_MODEL), BF16),
                        pltpu.VMEM((tm + 2 * HALO, 2 * D_FF), F32),
                        pltpu.VMEM((tm, D_FF), BF16)],
        compiler_params=_params(("arbitrary",)),
        name="ffn_lat" if per_seq_mod else "ffn_ctx",
    )(x1, h2, h2, h2, mod, wup, fw, fb, wdn, n2p)


def kernel(x_prompt, x_sample, state_ret_fwd, state_ret_bwd, c, c_ctx, norm1_pre, norm1_post, norm2_pre, norm2_post, ada_w, ada_b, w_in, ret_decay_fwd, ret_decay_bwd, ret_gn_w, ret_w_out, conv_dw_w, conv_dw_b, conv_ln_w, conv_ln_b, conv_w_out, w_mix_out, ffn_w_up, ffn_dw_w, ffn_dw_b, ffn_w_down):
    batch, seq, _ = x_prompt.shape
    dec_batch, dec_seq, _ = x_sample.shape
    assert norm1_pre.shape[0] == 1 and dec_batch + 1 <= 8

    cond8 = jnp.zeros((8, D_MODEL), F32).at[0].set(c_ctx).at[1:1 + dec_batch].set(c)
    mod = _adaln(cond8, ada_w[0], ada_b[0])
    tabs = _decay_tables(ret_decay_fwd[0], ret_decay_bwd[0])

    w_in_b = w_in[0].astype(BF16)
    wro, wco, wmx = (w[0].astype(BF16) for w in (ret_w_out, conv_w_out, w_mix_out))
    wup, wdn = ffn_w_up[0].astype(BF16), ffn_w_down[0].astype(BF16)
    cw = jnp.zeros((32, D_MODEL), F32).at[:CONV_WIDTH].set(conv_dw_w[0])
    fw = jnp.zeros((8, 2 * D_FF), F32).at[:FFN_CONV_WIDTH].set(ffn_dw_w[0])

    def trunk(x, *, rope, seq_len, tm_proj, tm_mix, tm_ffn, retention):
        t = x.shape[0]
        q, k, v, sg, u, gr, gc = _proj(x, mod, norm1_pre, w_in_b, rope=rope, seq=seq_len, tm=tm_proj)
        ro, states = retention(q, k, v, sg)
        x1, h2 = _mixer(x, ro, u, gr, gc, mod, wro, wco, wmx, cw, conv_dw_b, conv_ln_w, conv_ln_b,
                        norm1_post, norm2_pre, seq=seq_len, tm=tm_mix, per_seq_mod=rope)
        y = _ffn(x1, h2, mod, wup, fw, ffn_dw_b, wdn, norm2_post, seq=seq_len, tm=tm_ffn, per_seq_mod=rope)
        return y, states

    def ret_ctx(q, k, v, sg):
        ro, sf, sb = _ret_ctx(q, k, v, sg, ret_gn_w, tabs, batch, seq)
        return ro, (sf, sb)

    def ret_lat(q, k, v, sg):
        return _ret_lat(q, k, v, sg, ret_gn_w, state_ret_fwd, state_ret_bwd, tabs, dec_batch, dec_seq), None

    yp, (sf, sb) = trunk(x_prompt.reshape(batch * seq, D_MODEL), rope=False, seq_len=seq,
                         tm_proj=256, tm_mix=256, tm_ffn=256, retention=ret_ctx)
    ys, _ = trunk(x_sample.reshape(dec_batch * dec_seq, D_MODEL), rope=True, seq_len=dec_seq,
                  tm_proj=512, tm_mix=256, tm_ffn=256, retention=ret_lat)
    return (yp.reshape(batch, seq, D_MODEL), ys.reshape(dec_batch, dec_seq, D_MODEL), sf, sb)
```

```python
import functools

import numpy as np
import jax
import jax.numpy as jnp
from jax import lax
from jax.experimental import pallas as pl
from jax.experimental.pallas import tpu as pltpu

F32 = jnp.float32
BF16 = jnp.bfloat16

D_MODEL = 1024
N_HEADS = 4
D_HEAD = 256
D_FF = 2816
CONV_WIDTH = 31
FFN_CONV_WIDTH = 3
GRID_W = 64
ROPE_BASE = 10000.0
EPS = 1e-6
GN_EPS = 1e-5
RET_CHUNK = 256
LAT_HEADS = 2
HALO = 16
CONV_ROWS = 16
MIX_CONV_ROWS = 32
FFN_COLS = 256
VMEM_LIMIT = 56 * 1024 * 1024


def _sigmoid(x):
    return 1.0 / (1.0 + jnp.exp(-x))


def _rms(x, w):
    ms = jnp.mean(x * x, axis=-1, keepdims=True)
    return (x * lax.rsqrt(ms + EPS)) * w


def _params(sem):
    return pltpu.CompilerParams(dimension_semantics=sem, vmem_limit_bytes=VMEM_LIMIT)


def _resident(shape):
    nd = len(shape)
    return pl.BlockSpec(shape, lambda *_: (0,) * nd, pipeline_mode=pl.Buffered(1))


def _adaln_kernel(cond_ref, w_ref, b_ref, o_ref):
    c = cond_ref[...]
    s = (c * _sigmoid(c)).astype(BF16)
    o_ref[0] = jnp.dot(s, w_ref[...].astype(BF16), preferred_element_type=F32) + b_ref[0]


def _adaln(cond8, ada_w, ada_b):
    return pl.pallas_call(
        _adaln_kernel,
        grid=(6,),
        in_specs=[pl.BlockSpec((8, D_MODEL), lambda j: (0, 0)),
                  pl.BlockSpec((D_MODEL, D_MODEL), lambda j: (0, j)),
                  pl.BlockSpec((1, 1, D_MODEL), lambda j: (j, 0, 0))],
        out_specs=pl.BlockSpec((1, 8, D_MODEL), lambda j: (j, 0, 0)),
        out_shape=jax.ShapeDtypeStruct((6, 8, D_MODEL), F32),
        compiler_params=_params(("arbitrary",)),
        name="adaln",
    )(cond8, ada_w, ada_b.reshape(6, 1, D_MODEL))


def _decay_kernel(df_ref, db_ref, d_ref, kf_ref, kb_ref, qf_ref, qb_ref, cf_ref, cb_ref):
    def log_sigmoid(x):
        return -(jnp.maximum(-x, 0.0) + jnp.log1p(jnp.exp(-jnp.abs(x))))

    lgf = log_sigmoid(df_ref[...])
    lgb = log_sigmoid(db_ref[...])
    c = RET_CHUNK
    row = lax.broadcasted_iota(jnp.int32, (c, 128), 0).astype(F32)
    col = lax.broadcasted_iota(jnp.int32, (c, 128), 1).astype(F32)
    for h in range(N_HEADS):
        f = lgf[h:h + 1, :]
        b = lgb[h:h + 1, :]
        for half in range(2):
            sl = slice(half * 128, (half + 1) * 128)
            rel = row - (col + 128.0 * half)
            fwd = jnp.exp(jnp.maximum(rel, 0.0) * f)
            bwd = jnp.exp(jnp.maximum(-rel, 0.0) * b)
            d_ref[h, :, sl] = jnp.where(rel > 0, fwd, jnp.where(rel < 0, bwd, 2.0))
            kf_ref[h, :, sl] = jnp.exp((c - 1.0 - row) * f)
            kb_ref[h, :, sl] = jnp.exp(row * b)
            qf_ref[h, :, sl] = jnp.exp((row + 1.0) * f)
            qb_ref[h, :, sl] = jnp.exp((c - row) * b)
            cf_ref[h, :, sl] = jnp.exp(jnp.broadcast_to(c * f, (8, 128)))
            cb_ref[h, :, sl] = jnp.exp(jnp.broadcast_to(c * b, (8, 128)))


def _decay_tables(dec_f, dec_b):
    def lanes(d):
        return jnp.zeros((8, 128), F32).at[:N_HEADS].set(jnp.broadcast_to(d[:, None], (N_HEADS, 128)))

    big = jax.ShapeDtypeStruct((N_HEADS, RET_CHUNK, D_HEAD), F32)
    small = jax.ShapeDtypeStruct((N_HEADS, 8, D_HEAD), F32)
    return pl.pallas_call(
        _decay_kernel,
        out_shape=(big, big, big, big, big, small, small),
        name="decay_tables",
    )(lanes(dec_f), lanes(dec_b))


def _rope_tables(seq):
    rows = seq // GRID_W
    nf = D_HEAD // 4
    r = np.repeat(np.arange(rows), GRID_W).astype(np.float64)
    c = np.tile(np.arange(GRID_W), rows).astype(np.float64)
    inv = ROPE_BASE ** (-np.arange(nf, dtype=np.float64) / nf)
    ar, ac = r[:, None] * inv[None, :], c[:, None] * inv[None, :]
    cos = np.concatenate([np.cos(ar), np.cos(ar), np.cos(ac), np.cos(ac)], axis=1)
    sin = np.concatenate([-np.sin(ar), np.sin(ar), -np.sin(ac), np.sin(ac)], axis=1)
    return jnp.asarray(cos, F32), jnp.asarray(sin, F32)


def _proj_kernel(*refs, rope, tiles_per_seq):
    if rope:
        x_ref, mod_ref, n1_ref, w_ref, cos_ref, sin_ref = refs[:6]
        outs = refs[6:]
    else:
        x_ref, mod_ref, n1_ref, w_ref = refs[:4]
        outs = refs[4:]
    q_ref, k_ref, v_ref, sg_ref, u_ref, gr_ref, gc_ref = outs
    r = (1 + pl.program_id(0) // tiles_per_seq) if rope else 0
    sh1 = mod_ref[0, pl.ds(r, 1), :]
    sc1 = mod_ref[1, pl.ds(r, 1), :]
    h = _rms(x_ref[...], n1_ref[...]) * (1.0 + sc1) + sh1
    hb = h.astype(BF16)

    def proj(g):
        return jnp.dot(hb, w_ref[:, g * D_MODEL:(g + 1) * D_MODEL], preferred_element_type=F32)

    def store_rot(dst, a, scale):
        for j in range(D_MODEL // 128):
            slab = a[:, j * 128:(j + 1) * 128]
            if rope:
                tsl = slice((j % 2) * 128, (j % 2 + 1) * 128)
                slab = slab * cos_ref[:, tsl] + pltpu.roll(slab, 64, axis=1) * sin_ref[:, tsl]
            if scale != 1.0:
                slab = slab * scale
            dst[:, j * 128:(j + 1) * 128] = slab.astype(BF16)

    store_rot(q_ref, proj(0), D_HEAD ** -0.5)
    store_rot(k_ref, proj(1), 1.0)
    v_ref[...] = proj(2).astype(BF16)
    g = proj(3)
    sg_ref[...] = (g * _sigmoid(g)).astype(BF16)
    ub = _sigmoid(proj(5))
    u_ref[...] = (proj(4) * ub).astype(BF16)
    gr_ref[...] = _sigmoid(proj(6)).astype(BF16)
    gc_ref[...] = _sigmoid(proj(7)).astype(BF16)


def _proj(x, mod, n1_pre, w_in, *, rope, seq, tm):
    t = x.shape[0]
    tiles_per_seq = seq // tm
    tile = pl.BlockSpec((tm, D_MODEL), lambda i: (i, 0))
    in_specs = [tile, _resident(mod.shape), _resident((1, D_MODEL)), _resident(w_in.shape)]
    args = [x, mod, n1_pre, w_in]
    if rope:
        cos, sin = _rope_tables(seq)
        tab = pl.BlockSpec((tm, D_HEAD), lambda i: (i % tiles_per_seq, 0))
        in_specs += [tab, tab]
        args += [cos, sin]
    out = jax.ShapeDtypeStruct((t, D_MODEL), BF16)
    return pl.pallas_call(
        functools.partial(_proj_kernel, rope=rope, tiles_per_seq=tiles_per_seq),
        grid=(t // tm,),
        in_specs=in_specs,
        out_specs=[tile] * 7,
        out_shape=[out] * 7,
        compiler_params=_params(("arbitrary",)),
        name="proj_lat" if rope else "proj_ctx",
    )(*args)


def _dot_nt(a, b):
    return lax.dot_general(a, b, (((1,), (1,)), ((), ())), preferred_element_type=F32)


def _dot_tn(a, b):
    return lax.dot_general(a, b, (((0,), (0,)), ((), ())), preferred_element_type=F32)


def _gn_gate(o, gnw, sg):
    mu = jnp.mean(o, axis=-1, keepdims=True)
    d = o - mu
    var = jnp.mean(d * d, axis=-1, keepdims=True)
    y = (d * lax.rsqrt(var + GN_EPS)) * gnw
    return (sg.astype(F32) * y).astype(BF16)


def _ret_ctx_kernel(q_ref, k_ref, v_ref, sg_ref, gnw_ref, d_ref, kf_ref, kb_ref,
                    ro_ref, sf_ref, sb_ref):
    for h in range(N_HEADS):
        sl = slice(h * D_HEAD, (h + 1) * D_HEAD)
        q, k, v = q_ref[:, sl], k_ref[:, sl], v_ref[:, sl]
        p = (_dot_nt(q, k) * d_ref[h]).astype(BF16)
        o = jnp.dot(p, v, preferred_element_type=F32)
        ro_ref[:, sl] = _gn_gate(o, gnw_ref[:, sl], sg_ref[:, sl])
        kf = k.astype(F32)
        sf_ref[0, 0, h] = _dot_tn((kf * kf_ref[h]).astype(BF16), v)
        sb_ref[0, 0, h] = _dot_tn((kf * kb_ref[h]).astype(BF16), v)


def _ret_ctx(q, k, v, sg, gn_w, tabs, batch, seq):
    assert seq == RET_CHUNK
    d, kf, kb = tabs[0], tabs[1], tabs[2]
    tile = pl.BlockSpec((seq, D_MODEL), lambda i: (i, 0))
    st = pl.BlockSpec((1, 1, N_HEADS, D_HEAD, D_HEAD), lambda i: (i, 0, 0, 0, 0))
    st_shape = jax.ShapeDtypeStruct((batch, 1, N_HEADS, D_HEAD, D_HEAD), F32)
    return pl.pallas_call(
        _ret_ctx_kernel,
        grid=(batch,),
        in_specs=[tile, tile, tile, tile, _resident((1, D_MODEL)),
                  _resident(d.shape), _resident(kf.shape), _resident(kb.shape)],
        out_specs=[tile, st, st],
        out_shape=[jax.ShapeDtypeStruct((batch * seq, D_MODEL), BF16), st_shape, st_shape],
        compiler_params=_params(("arbitrary",)),
        name="ret_ctx",
    )(q, k, v, sg, gn_w, d, kf, kb)


def _ret_lat_kernel(q_ref, k_ref, v_ref, sg_ref, gnw_ref, sf0_ref, sb0_ref,
                    d_ref, kf_ref, kb_ref, qf_ref, qb_ref, cf_ref, cb_ref,
                    ro_ref, sbs_ref, *, n_chunks):
    c = RET_CHUNK

    def rows(j):
        return pl.ds(pl.multiple_of(j * c, c), c)

    def cols(h):
        return slice(h * D_HEAD, (h + 1) * D_HEAD)

    def bwd(jj, states):
        j = n_chunks - 1 - jj
        new = []
        for h, s in enumerate(states):
            sbs_ref[h, j] = s
            kb = (k_ref[rows(j), cols(h)].astype(F32) * kb_ref[h]).astype(BF16)
            new.append(s * cb_ref[h, 0:1, :] + _dot_tn(kb, v_ref[rows(j), cols(h)]))
        return tuple(new)

    lax.fori_loop(0, n_chunks, bwd, tuple(sb0_ref[0, 0, h] for h in range(LAT_HEADS)))

    def fwd(j, states):
        new = []
        for h, s in enumerate(states):
            q, k, v = q_ref[rows(j), cols(h)], k_ref[rows(j), cols(h)], v_ref[rows(j), cols(h)]
            qf32, kf32 = q.astype(F32), k.astype(F32)
            p = (_dot_nt(q, k) * d_ref[h]).astype(BF16)
            o = jnp.dot(p, v, preferred_element_type=F32)
            o = o + jnp.dot((qf32 * qf_ref[h]).astype(BF16), s.astype(BF16), preferred_element_type=F32)
            o = o + jnp.dot((qf32 * qb_ref[h]).astype(BF16), sbs_ref[h, j].astype(BF16),
                            preferred_element_type=F32)
            ro_ref[rows(j), cols(h)] = _gn_gate(o, gnw_ref[:, cols(h)], sg_ref[rows(j), cols(h)])
            new.append(s * cf_ref[h, 0:1, :] + _dot_tn((kf32 * kf_ref[h]).astype(BF16), v))
        return tuple(new)

    lax.fori_loop(0, n_chunks, fwd, tuple(sf0_ref[0, 0, h] for h in range(LAT_HEADS)))


def _ret_lat(q, k, v, sg, gn_w, sf0, sb0, tabs, batch, seq):
    n_chunks = seq // RET_CHUNK
    w = LAT_HEADS * D_HEAD
    tile = pl.BlockSpec((seq, w), lambda s, h: (s, h))
    st = pl.BlockSpec((1, 1, LAT_HEADS, D_HEAD, D_HEAD), lambda s, h: (s, 0, h, 0, 0))
    big = pl.BlockSpec((LAT_HEADS, RET_CHUNK, D_HEAD), lambda s, h: (h, 0, 0))
    small = pl.BlockSpec((LAT_HEADS, 8, D_HEAD), lambda s, h: (h, 0, 0))
    return pl.pallas_call(
        functools.partial(_ret_lat_kernel, n_chunks=n_chunks),
        grid=(batch, N_HEADS // LAT_HEADS),
        in_specs=[tile, tile, tile, tile, pl.BlockSpec((1, w), lambda s, h: (0, h)),
                  st, st, big, big, big, big, big, small, small],
        out_specs=tile,
        out_shape=jax.ShapeDtypeStruct((batch * seq, D_MODEL), BF16),
        scratch_shapes=[pltpu.VMEM((LAT_HEADS, n_chunks, D_HEAD, D_HEAD), F32)],
        compiler_params=_params(("arbitrary", "arbitrary")),
        name="ret_lat",
    )(q, k, v, sg, gn_w, sf0, sb0, *tabs)


def _mixer_kernel(x_ref, ro_ref, u_ref, up_ref, un_ref, gr_ref, gc_ref, mod_ref,
                  wro_ref, wco_ref, wmx_ref, cw_ref, cb_ref, lnw_ref, lnb_ref, n1p_ref, n2_ref,
                  x1_ref, h2_ref, pad_ref, uc_ref, *, tm, tiles_per_seq, per_seq_mod):
    i = pl.program_id(0)
    r = (1 + i // tiles_per_seq) if per_seq_mod else 0
    g1 = mod_ref[2, pl.ds(r, 1), :]
    sh2 = mod_ref[3, pl.ds(r, 1), :]
    sc2 = mod_ref[4, pl.ds(r, 1), :]
    first = (i % tiles_per_seq) == 0
    last = (i % tiles_per_seq) == tiles_per_seq - 1

    n_pad = tm + 2 * HALO
    pad_ref[0, 0:HALO, :] = jnp.where(first, 0.0, up_ref[...].astype(F32))
    pad_ref[0, HALO:HALO + tm, :] = u_ref[...].astype(F32)
    pad_ref[0, HALO + tm:, :] = jnp.where(last, 0.0, un_ref[...].astype(F32))
    padded = pad_ref[0]
    for r in range(1, 8):
        pad_ref[r] = pltpu.roll(padded, n_pad - r, axis=0)

    off = HALO - CONV_WIDTH // 2

    def conv_rows(rb, carry):
        base = pl.multiple_of(rb * MIX_CONV_ROWS, MIX_CONV_ROWS)
        n_sub = MIX_CONV_ROWS // 8
        acc = [cb_ref[...]] * n_sub
        for j in range(CONV_WIDTH):
            s = j + off
            w = cw_ref[j]
            for i in range(n_sub):
                acc[i] = acc[i] + pad_ref[s % 8, pl.ds(base + (s // 8 + i) * 8, 8), :] * w
        for i in range(n_sub):
            uc_ref[pl.ds(base + i * 8, 8), :] = acc[i]
        return carry

    lax.fori_loop(0, tm // MIX_CONV_ROWS, conv_rows, 0)

    uc = uc_ref[...]
    mu = jnp.mean(uc, axis=-1, keepdims=True)
    d = uc - mu
    var = jnp.mean(d * d, axis=-1, keepdims=True)
    ln = (d * lax.rsqrt(var + GN_EPS)) * lnw_ref[...] + lnb_ref[...]
    cu = (ln * _sigmoid(ln)).astype(BF16)
    conv_out = jnp.dot(cu, wco_ref[...], preferred_element_type=F32)
    ret_out = jnp.dot(ro_ref[...], wro_ref[...], preferred_element_type=F32)
    mixed = gr_ref[...].astype(F32) * ret_out + gc_ref[...].astype(F32) * conv_out
    m = jnp.dot(mixed.astype(BF16), wmx_ref[...], preferred_element_type=F32)
    x1 = x_ref[...] + g1 * _rms(m, n1p_ref[...])
    x1_ref[...] = x1
    h2_ref[...] = (_rms(x1, n2_ref[...]) * (1.0 + sc2) + sh2).astype(BF16)


def _mixer(x, ro, u, gr, gc, mod, wro, wco, wmx, cw, cb, lnw, lnb, n1p, n2, *, seq, tm, per_seq_mod):
    t = x.shape[0]
    tiles_per_seq = seq // tm
    hb = tm // HALO
    n_hb = t // HALO
    tile = pl.BlockSpec((tm, D_MODEL), lambda i: (i, 0))
    prev = pl.BlockSpec((HALO, D_MODEL), lambda i: (jnp.maximum(i * hb - 1, 0), 0))
    nxt = pl.BlockSpec((HALO, D_MODEL), lambda i: (jnp.minimum((i + 1) * hb, n_hb - 1), 0))
    vec = _resident((1, D_MODEL))
    mat = _resident((D_MODEL, D_MODEL))
    return pl.pallas_call(
        functools.partial(_mixer_kernel, tm=tm, tiles_per_seq=tiles_per_seq, per_seq_mod=per_seq_mod),
        grid=(t // tm,),
        in_specs=[tile, tile, tile, prev, nxt, tile, tile, _resident(mod.shape),
                  mat, mat, mat, _resident(cw.shape), _resident(cb.shape), vec, vec, vec, vec],
        out_specs=[tile, tile],
        out_shape=[jax.ShapeDtypeStruct((t, D_MODEL), F32), jax.ShapeDtypeStruct((t, D_MODEL), BF16)],
        scratch_shapes=[pltpu.VMEM((8, tm + 2 * HALO, D_MODEL), F32), pltpu.VMEM((tm, D_MODEL), F32)],
        compiler_params=_params(("arbitrary",)),
        name="mixer_lat" if per_seq_mod else "mixer_ctx",
    )(x, ro, u, u, u, gr, gc, mod, wro, wco, wmx, cw, cb, lnw, lnb, n1p, n2)


def _ffn_kernel(x1_ref, h2_ref, hp_ref, hn_ref, mod_ref, wup_ref, fw_ref, fb_ref, wdn_ref, n2p_ref,
                y_ref, he_ref, upe_ref, act_ref, *, tm, tiles_per_seq, per_seq_mod):
    i = pl.program_id(0)
    r = (1 + i // tiles_per_seq) if per_seq_mod else 0
    g2 = mod_ref[5, pl.ds(r, 1), :]
    first = (i % tiles_per_seq) == 0
    last = (i % tiles_per_seq) == tiles_per_seq - 1

    he_ref[0:HALO, :] = jnp.where(first, jnp.zeros((), BF16), hp_ref[...])
    he_ref[HALO:HALO + tm, :] = h2_ref[...]
    he_ref[HALO + tm:, :] = jnp.where(last, jnp.zeros((), BF16), hn_ref[...])
    upe_ref[...] = jnp.dot(he_ref[...], wup_ref[...], preferred_element_type=F32)

    sub = lax.broadcasted_iota(jnp.int32, (8, FFN_COLS), 0)

    def conv3(base, col):
        cs = slice(col, col + FFN_COLS)
        v = [upe_ref[pl.ds(base + 8 * (i - 1), 8), cs] for i in range(4)]
        dn = [pltpu.roll(v[i], 1, axis=0) for i in range(3)]
        up = [pltpu.roll(v[i], 7, axis=0) for i in range(1, 4)]
        out = []
        for i in range(2):
            before = jnp.where(sub == 0, dn[i], dn[i + 1])
            after = jnp.where(sub == 7, up[i + 1], up[i])
            out.append(fb_ref[:, cs] + before * fw_ref[0, :, cs] + v[i + 1] * fw_ref[1, :, cs]
                       + after * fw_ref[2, :, cs])
        return out

    def rows(rb, carry):
        base = pl.multiple_of(rb * CONV_ROWS, CONV_ROWS)
        for cc in range(D_FF // FFN_COLS):
            a = conv3(base + HALO, cc * FFN_COLS)
            gl = conv3(base + HALO, D_FF + cc * FFN_COLS)
            act = jnp.concatenate([(a[i] * _sigmoid(a[i])) * gl[i] for i in range(2)], axis=0)
            act_ref[pl.ds(base, CONV_ROWS), cc * FFN_COLS:(cc + 1) * FFN_COLS] = act.astype(BF16)
        return carry

    lax.fori_loop(0, tm // CONV_ROWS, rows, 0)

    f = jnp.dot(act_ref[...], wdn_ref[...], preferred_element_type=F32)
    y_ref[...] = x1_ref[...] + g2 * _rms(f, n2p_ref[...])


def _ffn(x1, h2, mod, wup, fw, fb, wdn, n2p, *, seq, tm, per_seq_mod):
    t = x1.shape[0]
    tiles_per_seq = seq // tm
    hb = tm // HALO
    n_hb = t // HALO
    tile = pl.BlockSpec((tm, D_MODEL), lambda i: (i, 0))
    prev = pl.BlockSpec((HALO, D_MODEL), lambda i: (jnp.maximum(i * hb - 1, 0), 0))
    nxt = pl.BlockSpec((HALO, D_MODEL), lambda i: (jnp.minimum((i + 1) * hb, n_hb - 1), 0))
    return pl.pallas_call(
        functools.partial(_ffn_kernel, tm=tm, tiles_per_seq=tiles_per_seq, per_seq_mod=per_seq_mod),
        grid=(t // tm,),
        in_specs=[tile, tile, prev, nxt, _resident(mod.shape), _resident(wup.shape),
                  _resident(fw.shape), _resident(fb.shape), _resident(wdn.shape), _resident((1, D_MODEL))],
        out_specs=tile,
        out_shape=jax.ShapeDtypeStruct((t, D_MODEL), F32),
        scratch_shapes=[pltpu.VMEM((tm + 2 * HALO, D_MODEL), BF16),
                        pltpu.VMEM((tm + 2 * HALO, 2 * D_FF), F32),
                        pltpu.VMEM((tm, D_FF), BF16)],
        compiler_params=_params(("arbitrary",)),
        name="ffn_lat" if per_seq_mod else "ffn_ctx",
    )(x1, h2, h2, h2, mod, wup, fw, fb, wdn, n2p)


def kernel(x_prompt, x_sample, state_ret_fwd, state_ret_bwd, c, c_ctx, norm1_pre, norm1_post, norm2_pre, norm2_post, ada_w, ada_b, w_in, ret_decay_fwd, ret_decay_bwd, ret_gn_w, ret_w_out, conv_dw_w, conv_dw_b, conv_ln_w, conv_ln_b, conv_w_out, w_mix_out, ffn_w_up, ffn_dw_w, ffn_dw_b, ffn_w_down):
    batch, seq, _ = x_prompt.shape
    dec_batch, dec_seq, _ = x_sample.shape
    assert norm1_pre.shape[0] == 1 and dec_batch + 1 <= 8

    cond8 = jnp.zeros((8, D_MODEL), F32).at[0].set(c_ctx).at[1:1 + dec_batch].set(c)
    mod = _adaln(cond8, ada_w[0], ada_b[0])
    tabs = _decay_tables(ret_decay_fwd[0], ret_decay_bwd[0])

    w_in_b = w_in[0].astype(BF16)
    wro, wco, wmx = (w[0].astype(BF16) for w in (ret_w_out, conv_w_out, w_mix_out))
    wup, wdn = ffn_w_up[0].astype(BF16), ffn_w_down[0].astype(BF16)
    cw = jnp.broadcast_to(conv_dw_w[0][:, None, :], (CONV_WIDTH, 8, D_MODEL))
    cb = jnp.broadcast_to(conv_dw_b, (8, D_MODEL))
    fw = jnp.broadcast_to(ffn_dw_w[0][:, None, :], (FFN_CONV_WIDTH, 8, 2 * D_FF))
    fb = jnp.broadcast_to(ffn_dw_b, (8, 2 * D_FF))

    def trunk(x, *, rope, seq_len, tm_proj, tm_mix, tm_ffn, retention):
        t = x.shape[0]
        q, k, v, sg, u, gr, gc = _proj(x, mod, norm1_pre, w_in_b, rope=rope, seq=seq_len, tm=tm_proj)
        ro, states = retention(q, k, v, sg)
        x1, h2 = _mixer(x, ro, u, gr, gc, mod, wro, wco, wmx, cw, cb, conv_ln_w, conv_ln_b,
                        norm1_post, norm2_pre, seq=seq_len, tm=tm_mix, per_seq_mod=rope)
        y = _ffn(x1, h2, mod, wup, fw, fb, wdn, norm2_post, seq=seq_len, tm=tm_ffn, per_seq_mod=rope)
        return y, states

    def ret_ctx(q, k, v, sg):
        ro, sf, sb = _ret_ctx(q, k, v, sg, ret_gn_w, tabs, batch, seq)
        return ro, (sf, sb)

    def ret_lat(q, k, v, sg):
        return _ret_lat(q, k, v, sg, ret_gn_w, state_ret_fwd, state_ret_bwd, tabs, dec_batch, dec_seq), None

    yp, (sf, sb) = trunk(x_prompt.reshape(batch * seq, D_MODEL), rope=False, seq_len=seq,
                         tm_proj=256, tm_mix=256, tm_ffn=256, retention=ret_ctx)
    ys, _ = trunk(x_sample.reshape(dec_batch * dec_seq, D_MODEL), rope=True, seq_len=dec_seq,
                  tm_proj=512, tm_mix=256, tm_ffn=256, retention=ret_lat)
    return (yp.reshape(batch, seq, D_MODEL), ys.reshape(dec_batch, dec_seq, D_MODEL), sf, sb)
```

```python
import functools

import numpy as np
import jax
import jax.numpy as jnp
from jax import lax
from jax.experimental import pallas as pl
from jax.experimental.pallas import tpu as pltpu

F32 = jnp.float32
BF16 = jnp.bfloat16

D_MODEL = 1024
N_HEADS = 4
D_HEAD = 256
D_FF = 2816
CONV_WIDTH = 31
FFN_CONV_WIDTH = 3
GRID_W = 64
ROPE_BASE = 10000.0
EPS = 1e-6
GN_EPS = 1e-5
RET_CHUNK = 256
LAT_HEADS = 2
HALO = 16
CONV_ROWS = 16
MIX_CONV_ROWS = 32
FFN_COLS = 256
VMEM_LIMIT = 56 * 1024 * 1024


def _sigmoid(x):
    return 1.0 / (1.0 + jnp.exp(-x))


def _rms(x, w):
    ms = jnp.mean(x * x, axis=-1, keepdims=True)
    return (x * lax.rsqrt(ms + EPS)) * w


def _params(sem):
    return pltpu.CompilerParams(dimension_semantics=sem, vmem_limit_bytes=VMEM_LIMIT)


def _resident(shape):
    nd = len(shape)
    return pl.BlockSpec(shape, lambda *_: (0,) * nd, pipeline_mode=pl.Buffered(1))


def _adaln_kernel(cond_ref, w_ref, b_ref, o_ref):
    c = cond_ref[...]
    s = (c * _sigmoid(c)).astype(BF16)
    o_ref[0] = jnp.dot(s, w_ref[...].astype(BF16), preferred_element_type=F32) + b_ref[0]


def _adaln(cond8, ada_w, ada_b):
    return pl.pallas_call(
        _adaln_kernel,
        grid=(6,),
        in_specs=[pl.BlockSpec((8, D_MODEL), lambda j: (0, 0)),
                  pl.BlockSpec((D_MODEL, D_MODEL), lambda j: (0, j)),
                  pl.BlockSpec((1, 1, D_MODEL), lambda j: (j, 0, 0))],
        out_specs=pl.BlockSpec((1, 8, D_MODEL), lambda j: (j, 0, 0)),
        out_shape=jax.ShapeDtypeStruct((6, 8, D_MODEL), F32),
        compiler_params=_params(("arbitrary",)),
        name="adaln",
    )(cond8, ada_w, ada_b.reshape(6, 1, D_MODEL))


def _decay_kernel(df_ref, db_ref, d_ref, kf_ref, kb_ref, qf_ref, qb_ref, cf_ref, cb_ref):
    def log_sigmoid(x):
        return -(jnp.maximum(-x, 0.0) + jnp.log1p(jnp.exp(-jnp.abs(x))))

    lgf = log_sigmoid(df_ref[...])
    lgb = log_sigmoid(db_ref[...])
    c = RET_CHUNK
    row = lax.broadcasted_iota(jnp.int32, (c, 128), 0).astype(F32)
    col = lax.broadcasted_iota(jnp.int32, (c, 128), 1).astype(F32)
    for h in range(N_HEADS):
        f = lgf[h:h + 1, :]
        b = lgb[h:h + 1, :]
        for half in range(2):
            sl = slice(half * 128, (half + 1) * 128)
            rel = row - (col + 128.0 * half)
            fwd = jnp.exp(jnp.maximum(rel, 0.0) * f)
            bwd = jnp.exp(jnp.maximum(-rel, 0.0) * b)
            d_ref[h, :, sl] = jnp.where(rel > 0, fwd, jnp.where(rel < 0, bwd, 2.0))
            kf_ref[h, :, sl] = jnp.exp((c - 1.0 - row) * f)
            kb_ref[h, :, sl] = jnp.exp(row * b)
            qf_ref[h, :, sl] = jnp.exp((row + 1.0) * f)
            qb_ref[h, :, sl] = jnp.exp((c - row) * b)
            cf_ref[h, :, sl] = jnp.exp(jnp.broadcast_to(c * f, (8, 128)))
            cb_ref[h, :, sl] = jnp.exp(jnp.broadcast_to(c * b, (8, 128)))


def _decay_tables(dec_f, dec_b):
    def lanes(d):
        return jnp.zeros((8, 128), F32).at[:N_HEADS].set(jnp.broadcast_to(d[:, None], (N_HEADS, 128)))

    big = jax.ShapeDtypeStruct((N_HEADS, RET_CHUNK, D_HEAD), F32)
    small = jax.ShapeDtypeStruct((N_HEADS, 8, D_HEAD), F32)
    return pl.pallas_call(
        _decay_kernel,
        out_shape=(big, big, big, big, big, small, small),
        name="decay_tables",
    )(lanes(dec_f), lanes(dec_b))


def _rope_tables(seq):
    rows = seq // GRID_W
    nf = D_HEAD // 4
    r = np.repeat(np.arange(rows), GRID_W).astype(np.float64)
    c = np.tile(np.arange(GRID_W), rows).astype(np.float64)
    inv = ROPE_BASE ** (-np.arange(nf, dtype=np.float64) / nf)
    ar, ac = r[:, None] * inv[None, :], c[:, None] * inv[None, :]
    cos = np.concatenate([np.cos(ar), np.cos(ar), np.cos(ac), np.cos(ac)], axis=1)
    sin = np.concatenate([-np.sin(ar), np.sin(ar), -np.sin(ac), np.sin(ac)], axis=1)
    return jnp.asarray(cos, F32), jnp.asarray(sin, F32)


def _proj_kernel(*refs, rope, tiles_per_seq):
    if rope:
        x_ref, mod_ref, n1_ref, w_ref, cos_ref, sin_ref = refs[:6]
        outs = refs[6:]
    else:
        x_ref, mod_ref, n1_ref, w_ref = refs[:4]
        outs = refs[4:]
    q_ref, k_ref, v_ref, sg_ref, u_ref, gr_ref, gc_ref = outs
    r = (1 + pl.program_id(0) // tiles_per_seq) if rope else 0
    sh1 = mod_ref[0, pl.ds(r, 1), :]
    sc1 = mod_ref[1, pl.ds(r, 1), :]
    h = _rms(x_ref[...], n1_ref[...]) * (1.0 + sc1) + sh1
    hb = h.astype(BF16)

    def proj(g):
        return jnp.dot(hb, w_ref[:, g * D_MODEL:(g + 1) * D_MODEL], preferred_element_type=F32)

    def store_rot(dst, a, scale):
        for j in range(D_MODEL // 128):
            slab = a[:, j * 128:(j + 1) * 128]
            if rope:
                tsl = slice((j % 2) * 128, (j % 2 + 1) * 128)
                slab = slab * cos_ref[:, tsl] + pltpu.roll(slab, 64, axis=1) * sin_ref[:, tsl]
            if scale != 1.0:
                slab = slab * scale
            dst[:, j * 128:(j + 1) * 128] = slab.astype(BF16)

    store_rot(q_ref, proj(0), D_HEAD ** -0.5)
    store_rot(k_ref, proj(1), 1.0)
    v_ref[...] = proj(2).astype(BF16)
    g = proj(3)
    sg_ref[...] = (g * _sigmoid(g)).astype(BF16)
    ub = _sigmoid(proj(5))
    u_ref[...] = (proj(4) * ub).astype(BF16)
    gr_ref[...] = _sigmoid(proj(6)).astype(BF16)
    gc_ref[...] = _sigmoid(proj(7)).astype(BF16)


def _proj(x, mod, n1_pre, w_in, *, rope, seq, tm):
    t = x.shape[0]
    tiles_per_seq = seq // tm
    tile = pl.BlockSpec((tm, D_MODEL), lambda i: (i, 0))
    in_specs = [tile, _resident(mod.shape), _resident((1, D_MODEL)), _resident(w_in.shape)]
    args = [x, mod, n1_pre, w_in]
    if rope:
        cos, sin = _rope_tables(seq)
        tab = pl.BlockSpec((tm, D_HEAD), lambda i: (i % tiles_per_seq, 0))
        in_specs += [tab, tab]
        args += [cos, sin]
    out = jax.ShapeDtypeStruct((t, D_MODEL), BF16)
    return pl.pallas_call(
        functools.partial(_proj_kernel, rope=rope, tiles_per_seq=tiles_per_seq),
        grid=(t // tm,),
        in_specs=in_specs,
        out_specs=[tile] * 7,
        out_shape=[out] * 7,
        compiler_params=_params(("arbitrary",)),
        name="proj_lat" if rope else "proj_ctx",
    )(*args)


def _dot_nt(a, b):
    return lax.dot_general(a, b, (((1,), (1,)), ((), ())), preferred_element_type=F32)


def _dot_tn(a, b):
    return lax.dot_general(a, b, (((0,), (0,)), ((), ())), preferred_element_type=F32)


def _gn_gate(o, gnw, sg):
    mu = jnp.mean(o, axis=-1, keepdims=True)
    d = o - mu
    var = jnp.mean(d * d, axis=-1, keepdims=True)
    y = (d * lax.rsqrt(var + GN_EPS)) * gnw
    return (sg.astype(F32) * y).astype(BF16)


def _proj_ret_ctx_kernel(x_ref, mod_ref, n1_ref, w_ref, gnw_ref, d_ref, kf_ref, kb_ref,
                         ro_ref, u_ref, gr_ref, gc_ref, sf_ref, sb_ref):
    sh1 = mod_ref[0, 0:1, :]
    sc1 = mod_ref[1, 0:1, :]
    hb = (_rms(x_ref[...], n1_ref[...]) * (1.0 + sc1) + sh1).astype(BF16)

    def proj(g):
        return jnp.dot(hb, w_ref[:, g * D_MODEL:(g + 1) * D_MODEL], preferred_element_type=F32)

    q = (proj(0) * D_HEAD ** -0.5).astype(BF16)
    k = proj(1).astype(BF16)
    v = proj(2).astype(BF16)
    g = proj(3)
    sg = g * _sigmoid(g)

    def head(h):
        sl = slice(h * D_HEAD, (h + 1) * D_HEAD)
        qh, kh, vh = q[:, sl], k[:, sl], v[:, sl]
        p = (_dot_nt(qh, kh) * d_ref[h]).astype(BF16)
        o = jnp.dot(p, vh, preferred_element_type=F32)
        ro_ref[:, sl] = _gn_gate(o, gnw_ref[:, sl], sg[:, sl])
        kf = kh.astype(F32)
        sf_ref[0, 0, h] = _dot_tn((kf * kf_ref[h]).astype(BF16), vh)
        sb_ref[0, 0, h] = _dot_tn((kf * kb_ref[h]).astype(BF16), vh)

    head(0)
    ub = _sigmoid(proj(5))
    head(1)
    u_ref[...] = (proj(4) * ub).astype(BF16)
    head(2)
    gr_ref[...] = _sigmoid(proj(6)).astype(BF16)
    head(3)
    gc_ref[...] = _sigmoid(proj(7)).astype(BF16)


def _proj_ret_ctx(x, mod, n1_pre, w_in, gn_w, tabs, batch, seq):
    assert seq == RET_CHUNK
    d, kf, kb = tabs[0], tabs[1], tabs[2]
    tile = pl.BlockSpec((seq, D_MODEL), lambda i: (i, 0))
    vec = _resident((1, D_MODEL))
    st = pl.BlockSpec((1, 1, N_HEADS, D_HEAD, D_HEAD), lambda i: (i, 0, 0, 0, 0))
    st_shape = jax.ShapeDtypeStruct((batch, 1, N_HEADS, D_HEAD, D_HEAD), F32)
    out = jax.ShapeDtypeStruct((batch * seq, D_MODEL), BF16)
    return pl.pallas_call(
        _proj_ret_ctx_kernel,
        grid=(batch,),
        in_specs=[tile, _resident(mod.shape), vec, _resident(w_in.shape), vec,
                  _resident(d.shape), _resident(kf.shape), _resident(kb.shape)],
        out_specs=[tile, tile, tile, tile, st, st],
        out_shape=[out, out, out, out, st_shape, st_shape],
        compiler_params=_params(("arbitrary",)),
        name="proj_ret_ctx",
    )(x, mod, n1_pre, w_in, gn_w, d, kf, kb)


def _ret_lat_kernel(q_ref, k_ref, v_ref, sg_ref, gnw_ref, sf0_ref, sb0_ref,
                    d_ref, kf_ref, kb_ref, qf_ref, qb_ref, cf_ref, cb_ref,
                    ro_ref, sbs_ref, *, n_chunks):
    c = RET_CHUNK

    def rows(j):
        return pl.ds(pl.multiple_of(j * c, c), c)

    def cols(h):
        return slice(h * D_HEAD, (h + 1) * D_HEAD)

    def bwd(jj, states):
        j = n_chunks - 1 - jj
        new = []
        for h, s in enumerate(states):
            sbs_ref[h, j] = s
            kb = (k_ref[rows(j), cols(h)].astype(F32) * kb_ref[h]).astype(BF16)
            new.append(s * cb_ref[h, 0:1, :] + _dot_tn(kb, v_ref[rows(j), cols(h)]))
        return tuple(new)

    lax.fori_loop(0, n_chunks, bwd, tuple(sb0_ref[0, 0, h] for h in range(LAT_HEADS)))

    def fwd(j, states):
        new = []
        for h, s in enumerate(states):
            q, k, v = q_ref[rows(j), cols(h)], k_ref[rows(j), cols(h)], v_ref[rows(j), cols(h)]
            qf32, kf32 = q.astype(F32), k.astype(F32)
            p = (_dot_nt(q, k) * d_ref[h]).astype(BF16)
            o = jnp.dot(p, v, preferred_element_type=F32)
            o = o + jnp.dot((qf32 * qf_ref[h]).astype(BF16), s.astype(BF16), preferred_element_type=F32)
            o = o + jnp.dot((qf32 * qb_ref[h]).astype(BF16), sbs_ref[h, j].astype(BF16),
                            preferred_element_type=F32)
            ro_ref[rows(j), cols(h)] = _gn_gate(o, gnw_ref[:, cols(h)], sg_ref[rows(j), cols(h)])
            new.append(s * cf_ref[h, 0:1, :] + _dot_tn((kf32 * kf_ref[h]).astype(BF16), v))
        return tuple(new)

    lax.fori_loop(0, n_chunks, fwd, tuple(sf0_ref[0, 0, h] for h in range(LAT_HEADS)))


def _ret_lat(q, k, v, sg, gn_w, sf0, sb0, tabs, batch, seq):
    n_chunks = seq // RET_CHUNK
    w = LAT_HEADS * D_HEAD
    tile = pl.BlockSpec((seq, w), lambda s, h: (s, h))
    st = pl.BlockSpec((1, 1, LAT_HEADS, D_HEAD, D_HEAD), lambda s, h: (s, 0, h, 0, 0))
    big = pl.BlockSpec((LAT_HEADS, RET_CHUNK, D_HEAD), lambda s, h: (h, 0, 0))
    small = pl.BlockSpec((LAT_HEADS, 8, D_HEAD), lambda s, h: (h, 0, 0))
    return pl.pallas_call(
        functools.partial(_ret_lat_kernel, n_chunks=n_chunks),
        grid=(batch, N_HEADS // LAT_HEADS),
        in_specs=[tile, tile, tile, tile, pl.BlockSpec((1, w), lambda s, h: (0, h)),
                  st, st, big, big, big, big, big, small, small],
        out_specs=tile,
        out_shape=jax.ShapeDtypeStruct((batch * seq, D_MODEL), BF16),
        scratch_shapes=[pltpu.VMEM((LAT_HEADS, n_chunks, D_HEAD, D_HEAD), F32)],
        compiler_params=_params(("arbitrary", "arbitrary")),
        name="ret_lat",
    )(q, k, v, sg, gn_w, sf0, sb0, *tabs)


def _mixer_kernel(x_ref, ro_ref, u_ref, up_ref, un_ref, gr_ref, gc_ref, mod_ref,
                  wro_ref, wco_ref, wmx_ref, cw_ref, cb_ref, lnw_ref, lnb_ref, n1p_ref, n2_ref,
                  x1_ref, h2_ref, pad_ref, uc_ref, *, tm, tiles_per_seq, per_seq_mod):
    i = pl.program_id(0)
    r = (1 + i // tiles_per_seq) if per_seq_mod else 0
    g1 = mod_ref[2, pl.ds(r, 1), :]
    sh2 = mod_ref[3, pl.ds(r, 1), :]
    sc2 = mod_ref[4, pl.ds(r, 1), :]
    first = (i % tiles_per_seq) == 0
    last = (i % tiles_per_seq) == tiles_per_seq - 1

    n_pad = tm + 2 * HALO
    pad_ref[0, 0:HALO, :] = jnp.where(first, 0.0, up_ref[...].astype(F32))
    pad_ref[0, HALO:HALO + tm, :] = u_ref[...].astype(F32)
    pad_ref[0, HALO + tm:, :] = jnp.where(last, 0.0, un_ref[...].astype(F32))
    padded = pad_ref[0]
    for r in range(1, 8):
        pad_ref[r] = pltpu.roll(padded, n_pad - r, axis=0)

    off = HALO - CONV_WIDTH // 2

    def conv_rows(rb, carry):
        base = pl.multiple_of(rb * MIX_CONV_ROWS, MIX_CONV_ROWS)
        n_sub = MIX_CONV_ROWS // 8
        acc = [cb_ref[...]] * n_sub
        for j in range(CONV_WIDTH):
            s = j + off
            w = cw_ref[j]
            for i in range(n_sub):
                acc[i] = acc[i] + pad_ref[s % 8, pl.ds(base + (s // 8 + i) * 8, 8), :] * w
        for i in range(n_sub):
            uc_ref[pl.ds(base + i * 8, 8), :] = acc[i]
        return carry

    lax.fori_loop(0, tm // MIX_CONV_ROWS, conv_rows, 0)

    uc = uc_ref[...]
    mu = jnp.mean(uc, axis=-1, keepdims=True)
    d = uc - mu
    var = jnp.mean(d * d, axis=-1, keepdims=True)
    ln = (d * lax.rsqrt(var + GN_EPS)) * lnw_ref[...] + lnb_ref[...]
    cu = (ln * _sigmoid(ln)).astype(BF16)
    conv_out = jnp.dot(cu, wco_ref[...], preferred_element_type=F32)
    ret_out = jnp.dot(ro_ref[...], wro_ref[...], preferred_element_type=F32)
    mixed = gr_ref[...].astype(F32) * ret_out + gc_ref[...].astype(F32) * conv_out
    m = jnp.dot(mixed.astype(BF16), wmx_ref[...], preferred_element_type=F32)
    x1 = x_ref[...] + g1 * _rms(m, n1p_ref[...])
    x1_ref[...] = x1
    h2_ref[...] = (_rms(x1, n2_ref[...]) * (1.0 + sc2) + sh2).astype(BF16)


def _mixer(x, ro, u, gr, gc, mod, wro, wco, wmx, cw, cb, lnw, lnb, n1p, n2, *, seq, tm, per_seq_mod):
    t = x.shape[0]
    tiles_per_seq = seq // tm
    hb = tm // HALO
    n_hb = t // HALO
    tile = pl.BlockSpec((tm, D_MODEL), lambda i: (i, 0))
    prev = pl.BlockSpec((HALO, D_MODEL), lambda i: (jnp.maximum(i * hb - 1, 0), 0))
    nxt = pl.BlockSpec((HALO, D_MODEL), lambda i: (jnp.minimum((i + 1) * hb, n_hb - 1), 0))
    vec = _resident((1, D_MODEL))
    mat = _resident((D_MODEL, D_MODEL))
    return pl.pallas_call(
        functools.partial(_mixer_kernel, tm=tm, tiles_per_seq=tiles_per_seq, per_seq_mod=per_seq_mod),
        grid=(t // tm,),
        in_specs=[tile, tile, tile, prev, nxt, tile, tile, _resident(mod.shape),
                  mat, mat, mat, _resident(cw.shape), _resident(cb.shape), vec, vec, vec, vec],
        out_specs=[tile, tile],
        out_shape=[jax.ShapeDtypeStruct((t, D_MODEL), F32), jax.ShapeDtypeStruct((t, D_MODEL), BF16)],
        scratch_shapes=[pltpu.VMEM((8, tm + 2 * HALO, D_MODEL), F32), pltpu.VMEM((tm, D_MODEL), F32)],
        compiler_params=_params(("arbitrary",)),
        name="mixer_lat" if per_seq_mod else "mixer_ctx",
    )(x, ro, u, u, u, gr, gc, mod, wro, wco, wmx, cw, cb, lnw, lnb, n1p, n2)


def _ffn_kernel(x1_ref, h2_ref, hp_ref, hn_ref, mod_ref, wup_ref, fw_ref, fb_ref, wdn_ref, n2p_ref,
                y_ref, *, tm, tiles_per_seq, per_seq_mod):
    i = pl.program_id(0)
    r = (1 + i // tiles_per_seq) if per_seq_mod else 0
    g2 = mod_ref[5, pl.ds(r, 1), :]
    first = (i % tiles_per_seq) == 0
    last = (i % tiles_per_seq) == tiles_per_seq - 1

    n_ext = tm + HALO
    row = lax.broadcasted_iota(jnp.int32, (HALO, 1), 0)
    zero = jnp.zeros((), BF16)
    halo = jnp.where(row < HALO // 2, jnp.where(last, zero, hn_ref[...]), jnp.where(first, zero, hp_ref[...]))
    he = jnp.concatenate([h2_ref[...], halo], axis=0)

    inner = slice(0, tm)
    n_chunks = D_FF // FFN_COLS

    def up_proj(cc):
        return jnp.dot(he, wup_ref[:, cc * 2 * FFN_COLS:(cc + 1) * 2 * FFN_COLS], preferred_element_type=F32)

    f = None
    up_next = up_proj(0)
    for cc in range(n_chunks):
        cs = slice(cc * 2 * FFN_COLS, (cc + 1) * 2 * FFN_COLS)
        up = up_next
        if cc + 1 < n_chunks:
            up_next = up_proj(cc + 1)
        before = pltpu.roll(up, 1, axis=0)[inner]
        after = pltpu.roll(up, n_ext - 1, axis=0)[inner]
        c = (fb_ref[:, cs] + before * fw_ref[0:1, cs] + up[inner] * fw_ref[1:2, cs]
             + after * fw_ref[2:3, cs])
        a, gl = c[:, :FFN_COLS], c[:, FFN_COLS:]
        act = ((a * _sigmoid(a)) * gl).astype(BF16)
        part = jnp.dot(act, wdn_ref[cc * FFN_COLS:(cc + 1) * FFN_COLS, :], preferred_element_type=F32)
        f = part if f is None else f + part
    y_ref[...] = x1_ref[...] + g2 * _rms(f, n2p_ref[...])


def _ffn(x1, h2, mod, wup, fw, fb, wdn, n2p, *, seq, tm, per_seq_mod):
    t = x1.shape[0]
    tiles_per_seq = seq // tm
    hb = tm // HALO
    n_hb = t // HALO
    tile = pl.BlockSpec((tm, D_MODEL), lambda i: (i, 0))
    prev = pl.BlockSpec((HALO, D_MODEL), lambda i: (jnp.maximum(i * hb - 1, 0), 0))
    nxt = pl.BlockSpec((HALO, D_MODEL), lambda i: (jnp.minimum((i + 1) * hb, n_hb - 1), 0))
    return pl.pallas_call(
        functools.partial(_ffn_kernel, tm=tm, tiles_per_seq=tiles_per_seq, per_seq_mod=per_seq_mod),
        grid=(t // tm,),
        in_specs=[tile, tile, prev, nxt, _resident(mod.shape), _resident(wup.shape),
                  _resident(fw.shape), _resident(fb.shape), _resident(wdn.shape), _resident((1, D_MODEL))],
        out_specs=tile,
        out_shape=jax.ShapeDtypeStruct((t, D_MODEL), F32),
        compiler_params=_params(("arbitrary",)),
        name="ffn_lat" if per_seq_mod else "ffn_ctx",
    )(x1, h2, h2, h2, mod, wup, fw, fb, wdn, n2p)


def kernel(x_prompt, x_sample, state_ret_fwd, state_ret_bwd, c, c_ctx, norm1_pre, norm1_post, norm2_pre, norm2_post, ada_w, ada_b, w_in, ret_decay_fwd, ret_decay_bwd, ret_gn_w, ret_w_out, conv_dw_w, conv_dw_b, conv_ln_w, conv_ln_b, conv_w_out, w_mix_out, ffn_w_up, ffn_dw_w, ffn_dw_b, ffn_w_down):
    batch, seq, _ = x_prompt.shape
    dec_batch, dec_seq, _ = x_sample.shape
    assert norm1_pre.shape[0] == 1 and dec_batch + 1 <= 8

    cond8 = jnp.zeros((8, D_MODEL), F32).at[0].set(c_ctx).at[1:1 + dec_batch].set(c)
    mod = _adaln(cond8, ada_w[0], ada_b[0])
    tabs = _decay_tables(ret_decay_fwd[0], ret_decay_bwd[0])

    w_in_b = w_in[0].astype(BF16)
    wro, wco, wmx = (w[0].astype(BF16) for w in (ret_w_out, conv_w_out, w_mix_out))
    wdn = ffn_w_down[0].astype(BF16)
    cw = jnp.broadcast_to(conv_dw_w[0][:, None, :], (CONV_WIDTH, 8, D_MODEL))
    cb = jnp.broadcast_to(conv_dw_b, (8, D_MODEL))

    def pair_chunks(a):
        lead = a.shape[:-1]
        n = D_FF // FFN_COLS
        return jnp.swapaxes(a.reshape(*lead, 2, n, FFN_COLS), -3, -2).reshape(*lead, 2 * D_FF)

    wup = pair_chunks(ffn_w_up[0]).astype(BF16)
    fw = jnp.zeros((8, 2 * D_FF), F32).at[:FFN_CONV_WIDTH].set(pair_chunks(ffn_dw_w[0]))
    fb = pair_chunks(ffn_dw_b)

    def tail(x, ro, u, gr, gc, *, latent, seq_len):
        x1, h2 = _mixer(x, ro, u, gr, gc, mod, wro, wco, wmx, cw, cb, conv_ln_w, conv_ln_b,
                        norm1_post, norm2_pre, seq=seq_len, tm=256, per_seq_mod=latent)
        return _ffn(x1, h2, mod, wup, fw, fb, wdn, norm2_post, seq=seq_len, tm=256, per_seq_mod=latent)

    xp = x_prompt.reshape(batch * seq, D_MODEL)
    ro, u, gr, gc, sf, sb = _proj_ret_ctx(xp, mod, norm1_pre, w_in_b, ret_gn_w, tabs, batch, seq)
    yp = tail(xp, ro, u, gr, gc, latent=False, seq_len=seq)

    xs = x_sample.reshape(dec_batch * dec_seq, D_MODEL)
    q, k, v, sg, u, gr, gc = _proj(xs, mod, norm1_pre, w_in_b, rope=True, seq=dec_seq, tm=512)
    ro = _ret_lat(q, k, v, sg, ret_gn_w, state_ret_fwd, state_ret_bwd, tabs, dec_batch, dec_seq)
    ys = tail(xs, ro, u, gr, gc, latent=True, seq_len=dec_seq)
    return (yp.reshape(batch, seq, D_MODEL), ys.reshape(dec_batch, dec_seq, D_MODEL), sf, sb)
```

```python
import functools

import numpy as np
import jax
import jax.numpy as jnp
from jax import lax
from jax.experimental import pallas as pl
from jax.experimental.pallas import tpu as pltpu

F32 = jnp.float32
BF16 = jnp.bfloat16

D_MODEL = 1024
N_HEADS = 4
D_HEAD = 256
D_FF = 2816
CONV_WIDTH = 31
FFN_CONV_WIDTH = 3
GRID_W = 64
ROPE_BASE = 10000.0
EPS = 1e-6
GN_EPS = 1e-5
RET_CHUNK = 256
LAT_HEADS = 2
HALO = 16
CONV_COLS = 256
PROJ_CONV_ROWS = 64
FFN_COLS = 256
VMEM_LIMIT = 56 * 1024 * 1024


def _sigmoid(x):
    return 1.0 / (1.0 + jnp.exp(-x))


def _rms(x, w):
    ms = jnp.mean(x * x, axis=-1, keepdims=True)
    return (x * lax.rsqrt(ms + EPS)) * w


def _params(sem):
    return pltpu.CompilerParams(dimension_semantics=sem, vmem_limit_bytes=VMEM_LIMIT)


def _resident(shape):
    nd = len(shape)
    return pl.BlockSpec(shape, lambda *_: (0,) * nd, pipeline_mode=pl.Buffered(1))


def _adaln_kernel(cond_ref, w_ref, b_ref, o_ref):
    c = cond_ref[...]
    s = (c * _sigmoid(c)).astype(BF16)
    o_ref[0] = jnp.dot(s, w_ref[...].astype(BF16), preferred_element_type=F32) + b_ref[0]


def _adaln(cond8, ada_w, ada_b):
    return pl.pallas_call(
        _adaln_kernel,
        grid=(6,),
        in_specs=[pl.BlockSpec((8, D_MODEL), lambda j: (0, 0)),
                  pl.BlockSpec((D_MODEL, D_MODEL), lambda j: (0, j)),
                  pl.BlockSpec((1, 1, D_MODEL), lambda j: (j, 0, 0))],
        out_specs=pl.BlockSpec((1, 8, D_MODEL), lambda j: (j, 0, 0)),
        out_shape=jax.ShapeDtypeStruct((6, 8, D_MODEL), F32),
        compiler_params=_params(("arbitrary",)),
        name="adaln",
    )(cond8, ada_w, ada_b.reshape(6, 1, D_MODEL))


def _decay_kernel(df_ref, db_ref, d_ref, kf_ref, kb_ref, qf_ref, qb_ref, cf_ref, cb_ref):
    def log_sigmoid(x):
        return -(jnp.maximum(-x, 0.0) + jnp.log1p(jnp.exp(-jnp.abs(x))))

    lgf = log_sigmoid(df_ref[...])
    lgb = log_sigmoid(db_ref[...])
    c = RET_CHUNK
    row = lax.broadcasted_iota(jnp.int32, (c, 128), 0).astype(F32)
    col = lax.broadcasted_iota(jnp.int32, (c, 128), 1).astype(F32)
    for h in range(N_HEADS):
        f = lgf[h:h + 1, :]
        b = lgb[h:h + 1, :]
        for half in range(2):
            sl = slice(half * 128, (half + 1) * 128)
            rel = row - (col + 128.0 * half)
            fwd = jnp.exp(jnp.maximum(rel, 0.0) * f)
            bwd = jnp.exp(jnp.maximum(-rel, 0.0) * b)
            d_ref[h, :, sl] = jnp.where(rel > 0, fwd, jnp.where(rel < 0, bwd, 2.0))
            kf_ref[h, :, sl] = jnp.exp((c - 1.0 - row) * f)
            kb_ref[h, :, sl] = jnp.exp(row * b)
            qf_ref[h, :, sl] = jnp.exp((row + 1.0) * f)
            qb_ref[h, :, sl] = jnp.exp((c - row) * b)
            cf_ref[h, :, sl] = jnp.exp(jnp.broadcast_to(c * f, (8, 128)))
            cb_ref[h, :, sl] = jnp.exp(jnp.broadcast_to(c * b, (8, 128)))


def _decay_tables(dec_f, dec_b):
    def lanes(d):
        return jnp.zeros((8, 128), F32).at[:N_HEADS].set(jnp.broadcast_to(d[:, None], (N_HEADS, 128)))

    big = jax.ShapeDtypeStruct((N_HEADS, RET_CHUNK, D_HEAD), F32)
    small = jax.ShapeDtypeStruct((N_HEADS, 8, D_HEAD), F32)
    return pl.pallas_call(
        _decay_kernel,
        out_shape=(big, big, big, big, big, small, small),
        name="decay_tables",
    )(lanes(dec_f), lanes(dec_b))


def _rope_tables(seq):
    rows = seq // GRID_W
    nf = D_HEAD // 4
    r = np.repeat(np.arange(rows), GRID_W).astype(np.float64)
    c = np.tile(np.arange(GRID_W), rows).astype(np.float64)
    inv = ROPE_BASE ** (-np.arange(nf, dtype=np.float64) / nf)
    ar, ac = r[:, None] * inv[None, :], c[:, None] * inv[None, :]
    cos = np.concatenate([np.cos(ar), np.cos(ar), np.cos(ac), np.cos(ac)], axis=1)
    sin = np.concatenate([-np.sin(ar), np.sin(ar), -np.sin(ac), np.sin(ac)], axis=1)
    return jnp.asarray(cos, F32), jnp.asarray(sin, F32)


def _proj_kernel(*refs, latent, tm, tiles_per_seq):
    it = iter(refs)
    x_ref = next(it)
    xp_ref, xn_ref = (next(it), next(it)) if latent else (None, None)
    mod_ref, n1_ref, w_ref, cw_ref, cb_ref, lnw_ref, lnb_ref = (next(it) for _ in range(7))
    if latent:
        cos_ref, sin_ref = next(it), next(it)
        q_ref, k_ref, v_ref, sg_ref, cu_ref, gr_ref, gc_ref = (next(it) for _ in range(7))
    else:
        gnw_ref, d_ref, kf_ref, kb_ref = (next(it) for _ in range(4))
        ro_ref, cu_ref, gr_ref, gc_ref, sf_ref, sb_ref = (next(it) for _ in range(6))
    pad_ref, uc_ref = it

    i = pl.program_id(0)
    r = (1 + i // tiles_per_seq) if latent else 0
    sh1 = mod_ref[0, pl.ds(r, 1), :]
    sc1 = mod_ref[1, pl.ds(r, 1), :]

    def norm(x):
        return (_rms(x, n1_ref[...]) * (1.0 + sc1) + sh1).astype(BF16)

    hb = norm(x_ref[...])
    n_pad = tm + 2 * HALO
    if latent:
        he = jnp.concatenate([norm(xp_ref[...]), hb, norm(xn_ref[...])], axis=0)
        row = lax.broadcasted_iota(jnp.int32, (n_pad, 1), 0)
        first = (i % tiles_per_seq) == 0
        last = (i % tiles_per_seq) == tiles_per_seq - 1
        outside = (first & (row < HALO)) | (last & (row >= HALO + tm))
    else:
        he = hb

    def proj(lhs, col, width=D_MODEL):
        return jnp.dot(lhs, w_ref[:, col:col + width], preferred_element_type=F32)

    off = HALO - CONV_WIDTH // 2
    n_sub = PROJ_CONV_ROWS // 8

    def conv_chunk(c):
        cs = slice(c * CONV_COLS, (c + 1) * CONV_COLS)
        u = (proj(he, 4 * D_MODEL + c * CONV_COLS, CONV_COLS)
             * _sigmoid(proj(he, 5 * D_MODEL + c * CONV_COLS, CONV_COLS)))
        if latent:
            pad_ref[c, 0] = jnp.where(outside, 0.0, u)
        else:
            pad_ref[c, 0, 0:HALO, :] = jnp.zeros((HALO, CONV_COLS), F32)
            pad_ref[c, 0, HALO:HALO + tm, :] = u
            pad_ref[c, 0, HALO + tm:, :] = jnp.zeros((HALO, CONV_COLS), F32)
        padded = pad_ref[c, 0]
        for s in range(1, 8):
            pad_ref[c, s] = pltpu.roll(padded, n_pad - s, axis=0)
        for rb in range(tm // PROJ_CONV_ROWS):
            base = rb * PROJ_CONV_ROWS
            acc = [cb_ref[:, cs]] * n_sub
            for j in range(CONV_WIDTH):
                s = j + off
                w = cw_ref[j, :, cs]
                for a in range(n_sub):
                    lo = base + (s // 8 + a) * 8
                    acc[a] = acc[a] + pad_ref[c, s % 8, lo:lo + 8, :] * w
            for a in range(n_sub):
                uc_ref[base + a * 8:base + (a + 1) * 8, cs] = acc[a]

    def conv_norm():
        uc = uc_ref[...]
        mu = jnp.mean(uc, axis=-1, keepdims=True)
        d = uc - mu
        var = jnp.mean(d * d, axis=-1, keepdims=True)
        ln = (d * lax.rsqrt(var + GN_EPS)) * lnw_ref[...] + lnb_ref[...]
        cu_ref[...] = (ln * _sigmoid(ln)).astype(BF16)

    def rotate(a):
        slabs = []
        for j in range(D_MODEL // 128):
            slab = a[:, j * 128:(j + 1) * 128]
            tsl = slice((j % 2) * 128, (j % 2 + 1) * 128)
            slabs.append(slab * cos_ref[:, tsl] + pltpu.roll(slab, 64, axis=1) * sin_ref[:, tsl])
        return jnp.concatenate(slabs, axis=1)

    a_q = proj(hb, 0)
    conv_chunk(0)
    q = ((rotate(a_q) if latent else a_q) * D_HEAD ** -0.5).astype(BF16)
    a_k = proj(hb, D_MODEL)
    conv_chunk(1)
    k = (rotate(a_k) if latent else a_k).astype(BF16)
    a_v = proj(hb, 2 * D_MODEL)
    conv_chunk(2)
    v = a_v.astype(BF16)
    a_g = proj(hb, 3 * D_MODEL)
    conv_chunk(3)
    sg = a_g * _sigmoid(a_g)
    a_gr = proj(hb, 6 * D_MODEL)
    conv_norm()
    gr_ref[...] = _sigmoid(a_gr).astype(BF16)
    a_gc = proj(hb, 7 * D_MODEL)

    if latent:
        q_ref[...] = q
        k_ref[...] = k
        v_ref[...] = v
        sg_ref[...] = sg.astype(BF16)
    else:
        for h in range(N_HEADS):
            sl = slice(h * D_HEAD, (h + 1) * D_HEAD)
            qh, kh, vh = q[:, sl], k[:, sl], v[:, sl]
            p = (_dot_nt(qh, kh) * d_ref[h]).astype(BF16)
            o = jnp.dot(p, vh, preferred_element_type=F32)
            ro_ref[:, sl] = _gn_gate(o, gnw_ref[:, sl], sg[:, sl])
            kf = kh.astype(F32)
            sf_ref[0, 0, h] = _dot_tn((kf * kf_ref[h]).astype(BF16), vh)
            sb_ref[0, 0, h] = _dot_tn((kf * kb_ref[h]).astype(BF16), vh)
    gc_ref[...] = _sigmoid(a_gc).astype(BF16)


def _proj(x, mod, n1_pre, w_in, cw, cb, lnw, lnb, *, latent, seq, tm, gn_w=None, tabs=None):
    t = x.shape[0]
    tiles_per_seq = seq // tm
    hb = tm // HALO
    n_hb = t // HALO
    tile = pl.BlockSpec((tm, D_MODEL), lambda i: (i, 0))
    vec = _resident((1, D_MODEL))
    out = jax.ShapeDtypeStruct((t, D_MODEL), BF16)
    in_specs, args = [tile], [x]
    if latent:
        in_specs += [pl.BlockSpec((HALO, D_MODEL), lambda i: (jnp.maximum(i * hb - 1, 0), 0)),
                     pl.BlockSpec((HALO, D_MODEL), lambda i: (jnp.minimum((i + 1) * hb, n_hb - 1), 0))]
        args += [x, x]
    in_specs += [_resident(mod.shape), vec, _resident(w_in.shape), _resident(cw.shape), _resident(cb.shape), vec, vec]
    args += [mod, n1_pre, w_in, cw, cb, lnw, lnb]
    if latent:
        cos, sin = _rope_tables(seq)
        tab = pl.BlockSpec((tm, D_HEAD), lambda i: (i % tiles_per_seq, 0))
        in_specs += [tab, tab]
        args += [cos, sin]
        out_specs, out_shape = [tile] * 7, [out] * 7
    else:
        assert tiles_per_seq == 1 and tm == RET_CHUNK
        d, kf, kb = tabs[0], tabs[1], tabs[2]
        in_specs += [vec, _resident(d.shape), _resident(kf.shape), _resident(kb.shape)]
        args += [gn_w, d, kf, kb]
        st = pl.BlockSpec((1, 1, N_HEADS, D_HEAD, D_HEAD), lambda i: (i, 0, 0, 0, 0))
        st_shape = jax.ShapeDtypeStruct((t // tm, 1, N_HEADS, D_HEAD, D_HEAD), F32)
        out_specs, out_shape = [tile] * 4 + [st, st], [out] * 4 + [st_shape, st_shape]
    return pl.pallas_call(
        functools.partial(_proj_kernel, latent=latent, tm=tm, tiles_per_seq=tiles_per_seq),
        grid=(t // tm,),
        in_specs=in_specs,
        out_specs=out_specs,
        out_shape=out_shape,
        scratch_shapes=[pltpu.VMEM((D_MODEL // CONV_COLS, 8, tm + 2 * HALO, CONV_COLS), F32),
                        pltpu.VMEM((tm, D_MODEL), F32)],
        compiler_params=_params(("arbitrary",)),
        name="proj_lat" if latent else "proj_ctx",
    )(*args)


def _dot_nt(a, b):
    return lax.dot_general(a, b, (((1,), (1,)), ((), ())), preferred_element_type=F32)


def _dot_tn(a, b):
    return lax.dot_general(a, b, (((0,), (0,)), ((), ())), preferred_element_type=F32)


def _gn_gate(o, gnw, sg):
    mu = jnp.mean(o, axis=-1, keepdims=True)
    d = o - mu
    var = jnp.mean(d * d, axis=-1, keepdims=True)
    y = (d * lax.rsqrt(var + GN_EPS)) * gnw
    return (sg.astype(F32) * y).astype(BF16)


def _ret_lat_kernel(q_ref, k_ref, v_ref, sg_ref, gnw_ref, sf0_ref, sb0_ref,
                    d_ref, kf_ref, kb_ref, qf_ref, qb_ref, cf_ref, cb_ref,
                    ro_ref, sbs_ref, *, n_chunks):
    c = RET_CHUNK

    def rows(j):
        return pl.ds(pl.multiple_of(j * c, c), c)

    def cols(h):
        return slice(h * D_HEAD, (h + 1) * D_HEAD)

    def bwd(jj, states):
        j = n_chunks - 1 - jj
        new = []
        for h, s in enumerate(states):
            sbs_ref[h, j] = s
            kb = (k_ref[rows(j), cols(h)].astype(F32) * kb_ref[h]).astype(BF16)
            new.append(s * cb_ref[h, 0:1, :] + _dot_tn(kb, v_ref[rows(j), cols(h)]))
        return tuple(new)

    lax.fori_loop(0, n_chunks, bwd, tuple(sb0_ref[0, 0, h] for h in range(LAT_HEADS)))

    def fwd(j, states):
        new = []
        for h, s in enumerate(states):
            q, k, v = q_ref[rows(j), cols(h)], k_ref[rows(j), cols(h)], v_ref[rows(j), cols(h)]
            qf32, kf32 = q.astype(F32), k.astype(F32)
            p = (_dot_nt(q, k) * d_ref[h]).astype(BF16)
            o = jnp.dot(p, v, preferred_element_type=F32)
            o = o + jnp.dot((qf32 * qf_ref[h]).astype(BF16), s.astype(BF16), preferred_element_type=F32)
            o = o + jnp.dot((qf32 * qb_ref[h]).astype(BF16), sbs_ref[h, j].astype(BF16),
                            preferred_element_type=F32)
            ro_ref[rows(j), cols(h)] = _gn_gate(o, gnw_ref[:, cols(h)], sg_ref[rows(j), cols(h)])
            new.append(s * cf_ref[h, 0:1, :] + _dot_tn((kf32 * kf_ref[h]).astype(BF16), v))
        return tuple(new)

    lax.fori_loop(0, n_chunks, fwd, tuple(sf0_ref[0, 0, h] for h in range(LAT_HEADS)))


def _ret_lat(q, k, v, sg, gn_w, sf0, sb0, tabs, batch, seq):
    n_chunks = seq // RET_CHUNK
    w = LAT_HEADS * D_HEAD
    tile = pl.BlockSpec((seq, w), lambda s, h: (s, h))
    st = pl.BlockSpec((1, 1, LAT_HEADS, D_HEAD, D_HEAD), lambda s, h: (s, 0, h, 0, 0))
    big = pl.BlockSpec((LAT_HEADS, RET_CHUNK, D_HEAD), lambda s, h: (h, 0, 0))
    small = pl.BlockSpec((LAT_HEADS, 8, D_HEAD), lambda s, h: (h, 0, 0))
    return pl.pallas_call(
        functools.partial(_ret_lat_kernel, n_chunks=n_chunks),
        grid=(batch, N_HEADS // LAT_HEADS),
        in_specs=[tile, tile, tile, tile, pl.BlockSpec((1, w), lambda s, h: (0, h)),
                  st, st, big, big, big, big, big, small, small],
        out_specs=tile,
        out_shape=jax.ShapeDtypeStruct((batch * seq, D_MODEL), BF16),
        scratch_shapes=[pltpu.VMEM((LAT_HEADS, n_chunks, D_HEAD, D_HEAD), F32)],
        compiler_params=_params(("arbitrary", "arbitrary")),
        name="ret_lat",
    )(q, k, v, sg, gn_w, sf0, sb0, *tabs)


def _mixer_kernel(x_ref, ro_ref, cu_ref, gr_ref, gc_ref, mod_ref, wro_ref, wco_ref, wmx_ref, n1p_ref, n2_ref,
                  x1_ref, h2_ref, *, tiles_per_seq, per_seq_mod):
    r = (1 + pl.program_id(0) // tiles_per_seq) if per_seq_mod else 0
    g1 = mod_ref[2, pl.ds(r, 1), :]
    sh2 = mod_ref[3, pl.ds(r, 1), :]
    sc2 = mod_ref[4, pl.ds(r, 1), :]
    conv_out = jnp.dot(cu_ref[...], wco_ref[...], preferred_element_type=F32)
    ret_out = jnp.dot(ro_ref[...], wro_ref[...], preferred_element_type=F32)
    mixed = gr_ref[...].astype(F32) * ret_out + gc_ref[...].astype(F32) * conv_out
    m = jnp.dot(mixed.astype(BF16), wmx_ref[...], preferred_element_type=F32)
    x1 = x_ref[...] + g1 * _rms(m, n1p_ref[...])
    x1_ref[...] = x1
    h2_ref[...] = (_rms(x1, n2_ref[...]) * (1.0 + sc2) + sh2).astype(BF16)


def _mixer(x, ro, cu, gr, gc, mod, wro, wco, wmx, n1p, n2, *, seq, tm, per_seq_mod):
    t = x.shape[0]
    tile = pl.BlockSpec((tm, D_MODEL), lambda i: (i, 0))
    vec = _resident((1, D_MODEL))
    mat = _resident((D_MODEL, D_MODEL))
    return pl.pallas_call(
        functools.partial(_mixer_kernel, tiles_per_seq=max(seq // tm, 1), per_seq_mod=per_seq_mod),
        grid=(t // tm,),
        in_specs=[tile, tile, tile, tile, tile, _resident(mod.shape), mat, mat, mat, vec, vec],
        out_specs=[tile, tile],
        out_shape=[jax.ShapeDtypeStruct((t, D_MODEL), F32), jax.ShapeDtypeStruct((t, D_MODEL), BF16)],
        compiler_params=_params(("arbitrary",)),
        name="mixer_lat" if per_seq_mod else "mixer_ctx",
    )(x, ro, cu, gr, gc, mod, wro, wco, wmx, n1p, n2)


def _ffn_kernel(x1_ref, h2_ref, hp_ref, hn_ref, mod_ref, wup_ref, fw_ref, fb_ref, wdn_ref, n2p_ref,
                y_ref, *, tm, tiles_per_seq, per_seq_mod):
    i = pl.program_id(0)
    r = (1 + i // tiles_per_seq) if per_seq_mod else 0
    g2 = mod_ref[5, pl.ds(r, 1), :]
    first = (i % tiles_per_seq) == 0
    last = (i % tiles_per_seq) == tiles_per_seq - 1

    n_ext = tm + HALO
    row = lax.broadcasted_iota(jnp.int32, (HALO, 1), 0)
    zero = jnp.zeros((), BF16)
    halo = jnp.where(row < HALO // 2, jnp.where(last, zero, hn_ref[...]), jnp.where(first, zero, hp_ref[...]))
    he = jnp.concatenate([h2_ref[...], halo], axis=0)

    inner = slice(0, tm)
    n_chunks = D_FF // FFN_COLS

    def cols(cc, half):
        return slice(half * D_FF + cc * FFN_COLS, half * D_FF + (cc + 1) * FFN_COLS)

    def up_proj(cc):
        return [jnp.dot(he, wup_ref[:, cols(cc, half)], preferred_element_type=F32) for half in range(2)]

    def conv3(up, cs):
        before = pltpu.roll(up, 1, axis=0)[inner]
        after = pltpu.roll(up, n_ext - 1, axis=0)[inner]
        return (fb_ref[:, cs] + before * fw_ref[0:1, cs] + up[inner] * fw_ref[1:2, cs]
                + after * fw_ref[2:3, cs])

    f = None
    up_next = up_proj(0)
    for cc in range(n_chunks):
        up = up_next
        if cc + 1 < n_chunks:
            up_next = up_proj(cc + 1)
        a, gl = conv3(up[0], cols(cc, 0)), conv3(up[1], cols(cc, 1))
        act = ((a * _sigmoid(a)) * gl).astype(BF16)
        part = jnp.dot(act, wdn_ref[cc * FFN_COLS:(cc + 1) * FFN_COLS, :], preferred_element_type=F32)
        f = part if f is None else f + part
    y_ref[...] = x1_ref[...] + g2 * _rms(f, n2p_ref[...])


def _ffn(x1, h2, mod, wup, fw, fb, wdn, n2p, *, seq, tm, per_seq_mod):
    t = x1.shape[0]
    tiles_per_seq = seq // tm
    hb = tm // HALO
    n_hb = t // HALO
    tile = pl.BlockSpec((tm, D_MODEL), lambda i: (i, 0))
    prev = pl.BlockSpec((HALO, D_MODEL), lambda i: (jnp.maximum(i * hb - 1, 0), 0))
    nxt = pl.BlockSpec((HALO, D_MODEL), lambda i: (jnp.minimum((i + 1) * hb, n_hb - 1), 0))
    return pl.pallas_call(
        functools.partial(_ffn_kernel, tm=tm, tiles_per_seq=tiles_per_seq, per_seq_mod=per_seq_mod),
        grid=(t // tm,),
        in_specs=[tile, tile, prev, nxt, _resident(mod.shape), _resident(wup.shape),
                  _resident(fw.shape), _resident(fb.shape), _resident(wdn.shape), _resident((1, D_MODEL))],
        out_specs=tile,
        out_shape=jax.ShapeDtypeStruct((t, D_MODEL), F32),
        compiler_params=_params(("arbitrary",)),
        name="ffn_lat" if per_seq_mod else "ffn_ctx",
    )(x1, h2, h2, h2, mod, wup, fw, fb, wdn, n2p)


def kernel(x_prompt, x_sample, state_ret_fwd, state_ret_bwd, c, c_ctx, norm1_pre, norm1_post, norm2_pre, norm2_post, ada_w, ada_b, w_in, ret_decay_fwd, ret_decay_bwd, ret_gn_w, ret_w_out, conv_dw_w, conv_dw_b, conv_ln_w, conv_ln_b, conv_w_out, w_mix_out, ffn_w_up, ffn_dw_w, ffn_dw_b, ffn_w_down):
    batch, seq, _ = x_prompt.shape
    dec_batch, dec_seq, _ = x_sample.shape
    assert norm1_pre.shape[0] == 1 and dec_batch + 1 <= 8

    cond8 = jnp.zeros((8, D_MODEL), F32).at[0].set(c_ctx).at[1:1 + dec_batch].set(c)
    mod = _adaln(cond8, ada_w[0], ada_b[0])
    tabs = _decay_tables(ret_decay_fwd[0], ret_decay_bwd[0])

    w_in_b = w_in[0].astype(BF16)
    wro, wco, wmx = (w[0].astype(BF16) for w in (ret_w_out, conv_w_out, w_mix_out))
    wdn = ffn_w_down[0].astype(BF16)
    cw = jnp.broadcast_to(conv_dw_w[0][:, None, :], (CONV_WIDTH, 8, D_MODEL))
    cb = jnp.broadcast_to(conv_dw_b, (8, D_MODEL))
    wup = ffn_w_up[0].astype(BF16)
    fw = jnp.zeros((8, 2 * D_FF), F32).at[:FFN_CONV_WIDTH].set(ffn_dw_w[0])
    fb = ffn_dw_b

    def tail(x, ro, cu, gr, gc, *, latent, seq_len):
        x1, h2 = _mixer(x, ro, cu, gr, gc, mod, wro, wco, wmx, norm1_post, norm2_pre,
                        seq=seq_len, tm=512, per_seq_mod=latent)
        return _ffn(x1, h2, mod, wup, fw, fb, wdn, norm2_post, seq=seq_len, tm=256, per_seq_mod=latent)

    conv = (cw, cb, conv_ln_w, conv_ln_b)
    xp = x_prompt.reshape(batch * seq, D_MODEL)
    ro, cu, gr, gc, sf, sb = _proj(xp, mod, norm1_pre, w_in_b, *conv, latent=False, seq=seq, tm=seq,
                                   gn_w=ret_gn_w, tabs=tabs)
    yp = tail(xp, ro, cu, gr, gc, latent=False, seq_len=seq)

    xs = x_sample.reshape(dec_batch * dec_seq, D_MODEL)
    q, k, v, sg, cu, gr, gc = _proj(xs, mod, norm1_pre, w_in_b, *conv, latent=True, seq=dec_seq, tm=256)
    ro = _ret_lat(q, k, v, sg, ret_gn_w, state_ret_fwd, state_ret_bwd, tabs, dec_batch, dec_seq)
    ys = tail(xs, ro, cu, gr, gc, latent=True, seq_len=dec_seq)
    return (yp.reshape(batch, seq, D_MODEL), ys.reshape(dec_batch, dec_seq, D_MODEL), sf, sb)
```

```python
import functools

import numpy as np
import jax
import jax.numpy as jnp
from jax import lax
from jax.experimental import pallas as pl
from jax.experimental.pallas import tpu as pltpu

F32 = jnp.float32
BF16 = jnp.bfloat16

D_MODEL = 1024
N_HEADS = 4
D_HEAD = 256
D_FF = 2816
CONV_WIDTH = 31
FFN_CONV_WIDTH = 3
GRID_W = 64
ROPE_BASE = 10000.0
EPS = 1e-6
GN_EPS = 1e-5
RET_CHUNK = 256
LAT_HEADS = 2
HALO = 16
CONV_COLS = 256
PROJ_CONV_ROWS = 64
FFN_COLS = 256
FFN_AHEAD = 3
VMEM_LIMIT = 56 * 1024 * 1024


def _sigmoid(x):
    return 1.0 / (1.0 + jnp.exp(-x))


def _rms(x, w):
    ms = jnp.mean(x * x, axis=-1, keepdims=True)
    return (x * lax.rsqrt(ms + EPS)) * w


def _params(sem):
    return pltpu.CompilerParams(dimension_semantics=sem, vmem_limit_bytes=VMEM_LIMIT)


def _resident(shape):
    nd = len(shape)
    return pl.BlockSpec(shape, lambda *_: (0,) * nd, pipeline_mode=pl.Buffered(1))


def _adaln_kernel(cond_ref, w_ref, b_ref, o_ref):
    c = cond_ref[...]
    s = (c * _sigmoid(c)).astype(BF16)
    o_ref[0] = jnp.dot(s, w_ref[...].astype(BF16), preferred_element_type=F32) + b_ref[0]


def _adaln(cond8, ada_w, ada_b):
    return pl.pallas_call(
        _adaln_kernel,
        grid=(6,),
        in_specs=[pl.BlockSpec((8, D_MODEL), lambda j: (0, 0)),
                  pl.BlockSpec((D_MODEL, D_MODEL), lambda j: (0, j)),
                  pl.BlockSpec((1, 1, D_MODEL), lambda j: (j, 0, 0))],
        out_specs=pl.BlockSpec((1, 8, D_MODEL), lambda j: (j, 0, 0)),
        out_shape=jax.ShapeDtypeStruct((6, 8, D_MODEL), F32),
        compiler_params=_params(("arbitrary",)),
        name="adaln",
    )(cond8, ada_w, ada_b.reshape(6, 1, D_MODEL))


def _decay_kernel(df_ref, db_ref, d_ref, kf_ref, kb_ref, qf_ref, qb_ref, cf_ref, cb_ref):
    def log_sigmoid(x):
        return -(jnp.maximum(-x, 0.0) + jnp.log1p(jnp.exp(-jnp.abs(x))))

    lgf = log_sigmoid(df_ref[...])
    lgb = log_sigmoid(db_ref[...])
    c = RET_CHUNK
    row = lax.broadcasted_iota(jnp.int32, (c, 128), 0).astype(F32)
    col = lax.broadcasted_iota(jnp.int32, (c, 128), 1).astype(F32)
    for h in range(N_HEADS):
        f = lgf[h:h + 1, :]
        b = lgb[h:h + 1, :]
        for half in range(2):
            sl = slice(half * 128, (half + 1) * 128)
            rel = row - (col + 128.0 * half)
            fwd = jnp.exp(jnp.maximum(rel, 0.0) * f)
            bwd = jnp.exp(jnp.maximum(-rel, 0.0) * b)
            d_ref[h, :, sl] = jnp.where(rel > 0, fwd, jnp.where(rel < 0, bwd, 2.0))
            kf_ref[h, :, sl] = jnp.exp((c - 1.0 - row) * f)
            kb_ref[h, :, sl] = jnp.exp(row * b)
            qf_ref[h, :, sl] = jnp.exp((row + 1.0) * f)
            qb_ref[h, :, sl] = jnp.exp((c - row) * b)
            cf_ref[h, :, sl] = jnp.exp(jnp.broadcast_to(c * f, (8, 128)))
            cb_ref[h, :, sl] = jnp.exp(jnp.broadcast_to(c * b, (8, 128)))


def _decay_tables(dec_f, dec_b):
    def lanes(d):
        return jnp.zeros((8, 128), F32).at[:N_HEADS].set(jnp.broadcast_to(d[:, None], (N_HEADS, 128)))

    big = jax.ShapeDtypeStruct((N_HEADS, RET_CHUNK, D_HEAD), F32)
    small = jax.ShapeDtypeStruct((N_HEADS, 8, D_HEAD), F32)
    return pl.pallas_call(
        _decay_kernel,
        out_shape=(big, big, big, big, big, small, small),
        name="decay_tables",
    )(lanes(dec_f), lanes(dec_b))


def _rope_tables(seq):
    rows = seq // GRID_W
    nf = D_HEAD // 4
    r = np.repeat(np.arange(rows), GRID_W).astype(np.float64)
    c = np.tile(np.arange(GRID_W), rows).astype(np.float64)
    inv = ROPE_BASE ** (-np.arange(nf, dtype=np.float64) / nf)
    ar, ac = r[:, None] * inv[None, :], c[:, None] * inv[None, :]
    cos = np.concatenate([np.cos(ar), np.cos(ar), np.cos(ac), np.cos(ac)], axis=1)
    sin = np.concatenate([-np.sin(ar), np.sin(ar), -np.sin(ac), np.sin(ac)], axis=1)
    return jnp.asarray(cos, F32), jnp.asarray(sin, F32)


def _proj_kernel(*refs, latent, tm, tiles_per_seq):
    it = iter(refs)
    x_ref = next(it)
    xp_ref, xn_ref = (next(it), next(it)) if latent else (None, None)
    mod_ref, n1_ref, w_ref, cw_ref, cb_ref, lnw_ref, lnb_ref = (next(it) for _ in range(7))
    if latent:
        cos_ref, sin_ref = next(it), next(it)
        q_ref, k_ref, v_ref, sg_ref, cu_ref, gr_ref, gc_ref = (next(it) for _ in range(7))
    else:
        gnw_ref, d_ref, kf_ref, kb_ref = (next(it) for _ in range(4))
        ro_ref, cu_ref, gr_ref, gc_ref, sf_ref, sb_ref = (next(it) for _ in range(6))
    pad_ref, uc_ref = it

    i = pl.program_id(0)
    r = (1 + i // tiles_per_seq) if latent else 0
    sh1 = mod_ref[0, pl.ds(r, 1), :]
    sc1 = mod_ref[1, pl.ds(r, 1), :]

    def norm(x):
        return (_rms(x, n1_ref[...]) * (1.0 + sc1) + sh1).astype(BF16)

    hb = norm(x_ref[...])
    n_pad = tm + 2 * HALO
    if latent:
        he = jnp.concatenate([norm(xp_ref[...]), hb, norm(xn_ref[...])], axis=0)
        row = lax.broadcasted_iota(jnp.int32, (n_pad, 1), 0)
        first = (i % tiles_per_seq) == 0
        last = (i % tiles_per_seq) == tiles_per_seq - 1
        outside = (first & (row < HALO)) | (last & (row >= HALO + tm))
    else:
        he = hb

    def proj(lhs, col, width=D_MODEL):
        return jnp.dot(lhs, w_ref[:, col:col + width], preferred_element_type=F32)

    off = HALO - CONV_WIDTH // 2
    n_sub = PROJ_CONV_ROWS // 8

    def conv_chunk(c):
        cs = slice(c * CONV_COLS, (c + 1) * CONV_COLS)
        u = (proj(he, 4 * D_MODEL + c * CONV_COLS, CONV_COLS)
             * _sigmoid(proj(he, 5 * D_MODEL + c * CONV_COLS, CONV_COLS)))
        if latent:
            pad_ref[c, 0] = jnp.where(outside, 0.0, u)
        else:
            pad_ref[c, 0, 0:HALO, :] = jnp.zeros((HALO, CONV_COLS), F32)
            pad_ref[c, 0, HALO:HALO + tm, :] = u
            pad_ref[c, 0, HALO + tm:, :] = jnp.zeros((HALO, CONV_COLS), F32)
        padded = pad_ref[c, 0]
        for s in range(1, 8):
            pad_ref[c, s] = pltpu.roll(padded, n_pad - s, axis=0)
        for rb in range(tm // PROJ_CONV_ROWS):
            base = rb * PROJ_CONV_ROWS
            acc = [cb_ref[:, cs]] * n_sub
            for j in range(CONV_WIDTH):
                s = j + off
                w = cw_ref[j, :, cs]
                for a in range(n_sub):
                    lo = base + (s // 8 + a) * 8
                    acc[a] = acc[a] + pad_ref[c, s % 8, lo:lo + 8, :] * w
            for a in range(n_sub):
                uc_ref[base + a * 8:base + (a + 1) * 8, cs] = acc[a]

    def conv_norm():
        uc = uc_ref[...]
        mu = jnp.mean(uc, axis=-1, keepdims=True)
        d = uc - mu
        var = jnp.mean(d * d, axis=-1, keepdims=True)
        ln = (d * lax.rsqrt(var + GN_EPS)) * lnw_ref[...] + lnb_ref[...]
        cu_ref[...] = (ln * _sigmoid(ln)).astype(BF16)

    def rotate(a):
        slabs = []
        for j in range(D_MODEL // 128):
            slab = a[:, j * 128:(j + 1) * 128]
            tsl = slice((j % 2) * 128, (j % 2 + 1) * 128)
            slabs.append(slab * cos_ref[:, tsl] + pltpu.roll(slab, 64, axis=1) * sin_ref[:, tsl])
        return jnp.concatenate(slabs, axis=1)

    def head(h):
        sl = slice(h * D_HEAD, (h + 1) * D_HEAD)
        qh, kh, vh = q[:, sl], k[:, sl], v[:, sl]
        p = (_dot_nt(qh, kh) * d_ref[h]).astype(BF16)
        o = jnp.dot(p, vh, preferred_element_type=F32)
        ro_ref[:, sl] = _gn_gate(o, gnw_ref[:, sl], sg[:, sl])
        kf = kh.astype(F32)
        sf_ref[0, 0, h] = _dot_tn((kf * kf_ref[h]).astype(BF16), vh)
        sb_ref[0, 0, h] = _dot_tn((kf * kb_ref[h]).astype(BF16), vh)

    conv_chunk(0)
    conv_chunk(1)
    a_q = proj(hb, 0)
    q = ((rotate(a_q) if latent else a_q) * D_HEAD ** -0.5).astype(BF16)
    a_k = proj(hb, D_MODEL)
    k = (rotate(a_k) if latent else a_k).astype(BF16)
    a_v = proj(hb, 2 * D_MODEL)
    v = a_v.astype(BF16)
    a_g = proj(hb, 3 * D_MODEL)
    sg = a_g * _sigmoid(a_g)
    if latent:
        q_ref[...] = q
        k_ref[...] = k
        v_ref[...] = v
        sg_ref[...] = sg.astype(BF16)
    else:
        for h in range(N_HEADS):
            head(h)
    conv_chunk(2)
    a_gr = proj(hb, 6 * D_MODEL)
    gr_ref[...] = _sigmoid(a_gr).astype(BF16)
    conv_chunk(3)
    a_gc = proj(hb, 7 * D_MODEL)
    gc_ref[...] = _sigmoid(a_gc).astype(BF16)
    conv_norm()


def _proj(x, mod, n1_pre, w_in, cw, cb, lnw, lnb, *, latent, seq, tm, gn_w=None, tabs=None):
    t = x.shape[0]
    tiles_per_seq = seq // tm
    hb = tm // HALO
    n_hb = t // HALO
    tile = pl.BlockSpec((tm, D_MODEL), lambda i: (i, 0))
    vec = _resident((1, D_MODEL))
    out = jax.ShapeDtypeStruct((t, D_MODEL), BF16)
    in_specs, args = [tile], [x]
    if latent:
        in_specs += [pl.BlockSpec((HALO, D_MODEL), lambda i: (jnp.maximum(i * hb - 1, 0), 0)),
                     pl.BlockSpec((HALO, D_MODEL), lambda i: (jnp.minimum((i + 1) * hb, n_hb - 1), 0))]
        args += [x, x]
    in_specs += [_resident(mod.shape), vec, _resident(w_in.shape), _resident(cw.shape), _resident(cb.shape), vec, vec]
    args += [mod, n1_pre, w_in, cw, cb, lnw, lnb]
    if latent:
        cos, sin = _rope_tables(seq)
        tab = pl.BlockSpec((tm, D_HEAD), lambda i: (i % tiles_per_seq, 0))
        in_specs += [tab, tab]
        args += [cos, sin]
        out_specs, out_shape = [tile] * 7, [out] * 7
    else:
        assert tiles_per_seq == 1 and tm == RET_CHUNK
        d, kf, kb = tabs[0], tabs[1], tabs[2]
        in_specs += [vec, _resident(d.shape), _resident(kf.shape), _resident(kb.shape)]
        args += [gn_w, d, kf, kb]
        st = pl.BlockSpec((1, 1, N_HEADS, D_HEAD, D_HEAD), lambda i: (i, 0, 0, 0, 0))
        st_shape = jax.ShapeDtypeStruct((t // tm, 1, N_HEADS, D_HEAD, D_HEAD), F32)
        out_specs, out_shape = [tile] * 4 + [st, st], [out] * 4 + [st_shape, st_shape]
    return pl.pallas_call(
        functools.partial(_proj_kernel, latent=latent, tm=tm, tiles_per_seq=tiles_per_seq),
        grid=(t // tm,),
        in_specs=in_specs,
        out_specs=out_specs,
        out_shape=out_shape,
        scratch_shapes=[pltpu.VMEM((D_MODEL // CONV_COLS, 8, tm + 2 * HALO, CONV_COLS), F32),
                        pltpu.VMEM((tm, D_MODEL), F32)],
        compiler_params=_params(("arbitrary",)),
        name="proj_lat" if latent else "proj_ctx",
    )(*args)


def _dot_nt(a, b):
    return lax.dot_general(a, b, (((1,), (1,)), ((), ())), preferred_element_type=F32)


def _dot_tn(a, b):
    return lax.dot_general(a, b, (((0,), (0,)), ((), ())), preferred_element_type=F32)


def _gn_gate(o, gnw, sg):
    mu = jnp.mean(o, axis=-1, keepdims=True)
    d = o - mu
    var = jnp.mean(d * d, axis=-1, keepdims=True)
    y = (d * lax.rsqrt(var + GN_EPS)) * gnw
    return (sg.astype(F32) * y).astype(BF16)


def _ret_lat_kernel(q_ref, k_ref, v_ref, sg_ref, gnw_ref, sf0_ref, sb0_ref,
                    d_ref, kf_ref, kb_ref, qf_ref, qb_ref, cf_ref, cb_ref,
                    ro_ref, sbs_ref, *, n_chunks):
    c = RET_CHUNK

    def rows(j):
        return pl.ds(pl.multiple_of(j * c, c), c)

    def cols(h):
        return slice(h * D_HEAD, (h + 1) * D_HEAD)

    def bwd(jj, states):
        j = n_chunks - 1 - jj
        new = []
        for h, s in enumerate(states):
            sbs_ref[h, j] = s
            kb = (k_ref[rows(j), cols(h)].astype(F32) * kb_ref[h]).astype(BF16)
            new.append(s * cb_ref[h, 0:1, :] + _dot_tn(kb, v_ref[rows(j), cols(h)]))
        return tuple(new)

    lax.fori_loop(0, n_chunks, bwd, tuple(sb0_ref[0, 0, h] for h in range(LAT_HEADS)))

    def fwd(j, states):
        new = []
        for h, s in enumerate(states):
            q, k, v = q_ref[rows(j), cols(h)], k_ref[rows(j), cols(h)], v_ref[rows(j), cols(h)]
            qf32, kf32 = q.astype(F32), k.astype(F32)
            p = (_dot_nt(q, k) * d_ref[h]).astype(BF16)
            o = jnp.dot(p, v, preferred_element_type=F32)
            o = o + jnp.dot((qf32 * qf_ref[h]).astype(BF16), s.astype(BF16), preferred_element_type=F32)
            o = o + jnp.dot((qf32 * qb_ref[h]).astype(BF16), sbs_ref[h, j].astype(BF16),
                            preferred_element_type=F32)
            ro_ref[rows(j), cols(h)] = _gn_gate(o, gnw_ref[:, cols(h)], sg_ref[rows(j), cols(h)])
            new.append(s * cf_ref[h, 0:1, :] + _dot_tn((kf32 * kf_ref[h]).astype(BF16), v))
        return tuple(new)

    lax.fori_loop(0, n_chunks, fwd, tuple(sf0_ref[0, 0, h] for h in range(LAT_HEADS)))


def _ret_lat(q, k, v, sg, gn_w, sf0, sb0, tabs, batch, seq):
    n_chunks = seq // RET_CHUNK
    w = LAT_HEADS * D_HEAD
    tile = pl.BlockSpec((seq, w), lambda s, h: (s, h))
    st = pl.BlockSpec((1, 1, LAT_HEADS, D_HEAD, D_HEAD), lambda s, h: (s, 0, h, 0, 0))
    big = pl.BlockSpec((LAT_HEADS, RET_CHUNK, D_HEAD), lambda s, h: (h, 0, 0))
    small = pl.BlockSpec((LAT_HEADS, 8, D_HEAD), lambda s, h: (h, 0, 0))
    return pl.pallas_call(
        functools.partial(_ret_lat_kernel, n_chunks=n_chunks),
        grid=(batch, N_HEADS // LAT_HEADS),
        in_specs=[tile, tile, tile, tile, pl.BlockSpec((1, w), lambda s, h: (0, h)),
                  st, st, big, big, big, big, big, small, small],
        out_specs=tile,
        out_shape=jax.ShapeDtypeStruct((batch * seq, D_MODEL), BF16),
        scratch_shapes=[pltpu.VMEM((LAT_HEADS, n_chunks, D_HEAD, D_HEAD), F32)],
        compiler_params=_params(("arbitrary", "arbitrary")),
        name="ret_lat",
    )(q, k, v, sg, gn_w, sf0, sb0, *tabs)


def _mixer_kernel(x_ref, ro_ref, cu_ref, gr_ref, gc_ref, mod_ref, wro_ref, wco_ref, wmx_ref, n1p_ref, n2_ref,
                  x1_ref, h2_ref, *, tiles_per_seq, per_seq_mod):
    r = (1 + pl.program_id(0) // tiles_per_seq) if per_seq_mod else 0
    g1 = mod_ref[2, pl.ds(r, 1), :]
    sh2 = mod_ref[3, pl.ds(r, 1), :]
    sc2 = mod_ref[4, pl.ds(r, 1), :]
    conv_out = jnp.dot(cu_ref[...], wco_ref[...], preferred_element_type=F32)
    ret_out = jnp.dot(ro_ref[...], wro_ref[...], preferred_element_type=F32)
    mixed = gr_ref[...].astype(F32) * ret_out + gc_ref[...].astype(F32) * conv_out
    m = jnp.dot(mixed.astype(BF16), wmx_ref[...], preferred_element_type=F32)
    x1 = x_ref[...] + g1 * _rms(m, n1p_ref[...])
    x1_ref[...] = x1
    h2_ref[...] = (_rms(x1, n2_ref[...]) * (1.0 + sc2) + sh2).astype(BF16)


def _mixer(x, ro, cu, gr, gc, mod, wro, wco, wmx, n1p, n2, *, seq, tm, per_seq_mod):
    t = x.shape[0]
    tile = pl.BlockSpec((tm, D_MODEL), lambda i: (i, 0))
    vec = _resident((1, D_MODEL))
    mat = _resident((D_MODEL, D_MODEL))
    return pl.pallas_call(
        functools.partial(_mixer_kernel, tiles_per_seq=max(seq // tm, 1), per_seq_mod=per_seq_mod),
        grid=(t // tm,),
        in_specs=[tile, tile, tile, tile, tile, _resident(mod.shape), mat, mat, mat, vec, vec],
        out_specs=[tile, tile],
        out_shape=[jax.ShapeDtypeStruct((t, D_MODEL), F32), jax.ShapeDtypeStruct((t, D_MODEL), BF16)],
        compiler_params=_params(("arbitrary",)),
        name="mixer_lat" if per_seq_mod else "mixer_ctx",
    )(x, ro, cu, gr, gc, mod, wro, wco, wmx, n1p, n2)


def _ffn_kernel(x1_ref, h2_ref, hp_ref, hn_ref, mod_ref, wup_ref, fw_ref, fb_ref, wdn_ref, n2p_ref,
                y_ref, *, tm, tiles_per_seq, per_seq_mod):
    i = pl.program_id(0)
    r = (1 + i // tiles_per_seq) if per_seq_mod else 0
    g2 = mod_ref[5, pl.ds(r, 1), :]
    first = (i % tiles_per_seq) == 0
    last = (i % tiles_per_seq) == tiles_per_seq - 1

    n_ext = tm + HALO
    row = lax.broadcasted_iota(jnp.int32, (HALO, 1), 0)
    zero = jnp.zeros((), BF16)
    halo = jnp.where(row < HALO // 2, jnp.where(last, zero, hn_ref[...]), jnp.where(first, zero, hp_ref[...]))
    he = jnp.concatenate([h2_ref[...], halo], axis=0)

    inner = slice(0, tm)
    n_chunks = D_FF // FFN_COLS

    def cols(cc, half):
        return slice(half * D_FF + cc * FFN_COLS, half * D_FF + (cc + 1) * FFN_COLS)

    def up_proj(cc):
        return [jnp.dot(he, wup_ref[:, cols(cc, half)], preferred_element_type=F32) for half in range(2)]

    def conv3(up, cs):
        before = pltpu.roll(up, 1, axis=0)[inner]
        after = pltpu.roll(up, n_ext - 1, axis=0)[inner]
        return (fb_ref[:, cs] + before * fw_ref[0:1, cs] + up[inner] * fw_ref[1:2, cs]
                + after * fw_ref[2:3, cs])

    f = None
    ups = [up_proj(cc) for cc in range(FFN_AHEAD)]
    for cc in range(n_chunks):
        up = ups[cc]
        if cc + FFN_AHEAD < n_chunks:
            ups.append(up_proj(cc + FFN_AHEAD))
        a, gl = conv3(up[0], cols(cc, 0)), conv3(up[1], cols(cc, 1))
        act = ((a * _sigmoid(a)) * gl).astype(BF16)
        part = jnp.dot(act, wdn_ref[cc * FFN_COLS:(cc + 1) * FFN_COLS, :], preferred_element_type=F32)
        f = part if f is None else f + part
    y_ref[...] = x1_ref[...] + g2 * _rms(f, n2p_ref[...])


def _ffn(x1, h2, mod, wup, fw, fb, wdn, n2p, *, seq, tm, per_seq_mod):
    t = x1.shape[0]
    tiles_per_seq = seq // tm
    hb = tm // HALO
    n_hb = t // HALO
    tile = pl.BlockSpec((tm, D_MODEL), lambda i: (i, 0))
    prev = pl.BlockSpec((HALO, D_MODEL), lambda i: (jnp.maximum(i * hb - 1, 0), 0))
    nxt = pl.BlockSpec((HALO, D_MODEL), lambda i: (jnp.minimum((i + 1) * hb, n_hb - 1), 0))
    return pl.pallas_call(
        functools.partial(_ffn_kernel, tm=tm, tiles_per_seq=tiles_per_seq, per_seq_mod=per_seq_mod),
        grid=(t // tm,),
        in_specs=[tile, tile, prev, nxt, _resident(mod.shape), _resident(wup.shape),
                  _resident(fw.shape), _resident(fb.shape), _resident(wdn.shape), _resident((1, D_MODEL))],
        out_specs=tile,
        out_shape=jax.ShapeDtypeStruct((t, D_MODEL), F32),
        compiler_params=_params(("arbitrary",)),
        name="ffn_lat" if per_seq_mod else "ffn_ctx",
    )(x1, h2, h2, h2, mod, wup, fw, fb, wdn, n2p)


def kernel(x_prompt, x_sample, state_ret_fwd, state_ret_bwd, c, c_ctx, norm1_pre, norm1_post, norm2_pre, norm2_post, ada_w, ada_b, w_in, ret_decay_fwd, ret_decay_bwd, ret_gn_w, ret_w_out, conv_dw_w, conv_dw_b, conv_ln_w, conv_ln_b, conv_w_out, w_mix_out, ffn_w_up, ffn_dw_w, ffn_dw_b, ffn_w_down):
    batch, seq, _ = x_prompt.shape
    dec_batch, dec_seq, _ = x_sample.shape
    assert norm1_pre.shape[0] == 1 and dec_batch + 1 <= 8

    cond8 = jnp.zeros((8, D_MODEL), F32).at[0].set(c_ctx).at[1:1 + dec_batch].set(c)
    mod = _adaln(cond8, ada_w[0], ada_b[0])
    tabs = _decay_tables(ret_decay_fwd[0], ret_decay_bwd[0])

    w_in_b = w_in[0].astype(BF16)
    wro, wco, wmx = (w[0].astype(BF16) for w in (ret_w_out, conv_w_out, w_mix_out))
    wdn = ffn_w_down[0].astype(BF16)
    cw = jnp.broadcast_to(conv_dw_w[0][:, None, :], (CONV_WIDTH, 8, D_MODEL))
    cb = jnp.broadcast_to(conv_dw_b, (8, D_MODEL))
    wup = ffn_w_up[0].astype(BF16)
    fw = jnp.zeros((8, 2 * D_FF), F32).at[:FFN_CONV_WIDTH].set(ffn_dw_w[0])
    fb = ffn_dw_b

    def tail(x, ro, cu, gr, gc, *, latent, seq_len):
        x1, h2 = _mixer(x, ro, cu, gr, gc, mod, wro, wco, wmx, norm1_post, norm2_pre,
                        seq=seq_len, tm=512, per_seq_mod=latent)
        return _ffn(x1, h2, mod, wup, fw, fb, wdn, norm2_post, seq=seq_len, tm=256, per_seq_mod=latent)

    conv = (cw, cb, conv_ln_w, conv_ln_b)
    xp = x_prompt.reshape(batch * seq, D_MODEL)
    ro, cu, gr, gc, sf, sb = _proj(xp, mod, norm1_pre, w_in_b, *conv, latent=False, seq=seq, tm=seq,
                                   gn_w=ret_gn_w, tabs=tabs)
    yp = tail(xp, ro, cu, gr, gc, latent=False, seq_len=seq)

    xs = x_sample.reshape(dec_batch * dec_seq, D_MODEL)
    q, k, v, sg, cu, gr, gc = _proj(xs, mod, norm1_pre, w_in_b, *conv, latent=True, seq=dec_seq, tm=256)
    ro = _ret_lat(q, k, v, sg, ret_gn_w, state_ret_fwd, state_ret_bwd, tabs, dec_batch, dec_seq)
    ys = tail(xs, ro, cu, gr, gc, latent=True, seq_len=dec_seq)
    return (yp.reshape(batch, seq, D_MODEL), ys.reshape(dec_batch, dec_seq, D_MODEL), sf, sb)
```

```python
import functools

import numpy as np
import jax
import jax.numpy as jnp
from jax import lax
from jax.experimental import pallas as pl
from jax.experimental.pallas import tpu as pltpu

F32 = jnp.float32
BF16 = jnp.bfloat16

D_MODEL = 1024
N_HEADS = 4
D_HEAD = 256
D_FF = 2816
CONV_WIDTH = 31
FFN_CONV_WIDTH = 3
GRID_W = 64
ROPE_BASE = 10000.0
EPS = 1e-6
GN_EPS = 1e-5
RET_CHUNK = 256
ADALN_CHUNKS = 3
LAT_HEADS = 2
LAT_UNROLL = 4
HALO = 16
CONV_COLS = 256
PROJ_CONV_ROWS = 64
FFN_COLS = 256
FFN_AHEAD = 3
VMEM_LIMIT = 56 * 1024 * 1024


def _sigmoid(x):
    return 1.0 / (1.0 + jnp.exp(-x))


def _rms(x, w):
    ms = jnp.mean(x * x, axis=-1, keepdims=True)
    return (x * lax.rsqrt(ms + EPS)) * w


def _params(sem):
    return pltpu.CompilerParams(dimension_semantics=sem, vmem_limit_bytes=VMEM_LIMIT)


def _resident(shape):
    nd = len(shape)
    return pl.BlockSpec(shape, lambda *_: (0,) * nd, pipeline_mode=pl.Buffered(1))


def _adaln_kernel(cond_ref, w_ref, b_ref, o_ref):
    c = cond_ref[...]
    s = (c * _sigmoid(c)).astype(BF16)
    for j in range(ADALN_CHUNKS):
        w = w_ref[:, j * D_MODEL:(j + 1) * D_MODEL].astype(BF16)
        o_ref[j] = jnp.dot(s, w, preferred_element_type=F32) + b_ref[j]


def _adaln(cond8, ada_w, ada_b):
    n = ADALN_CHUNKS
    return pl.pallas_call(
        _adaln_kernel,
        grid=(6 // n,),
        in_specs=[pl.BlockSpec((8, D_MODEL), lambda j: (0, 0)),
                  pl.BlockSpec((D_MODEL, n * D_MODEL), lambda j: (0, j)),
                  pl.BlockSpec((n, 1, D_MODEL), lambda j: (j, 0, 0))],
        out_specs=pl.BlockSpec((n, 8, D_MODEL), lambda j: (j, 0, 0)),
        out_shape=jax.ShapeDtypeStruct((6, 8, D_MODEL), F32),
        compiler_params=_params(("arbitrary",)),
        name="adaln",
    )(cond8, ada_w, ada_b.reshape(6, 1, D_MODEL))


def _decay_kernel(df_ref, db_ref, d_ref, kf_ref, kb_ref, qf_ref, qb_ref, cf_ref, cb_ref):
    def log_sigmoid(x):
        return -(jnp.maximum(-x, 0.0) + jnp.log1p(jnp.exp(-jnp.abs(x))))

    lgf = log_sigmoid(df_ref[...])
    lgb = log_sigmoid(db_ref[...])
    c = RET_CHUNK
    row = lax.broadcasted_iota(jnp.int32, (c, 128), 0).astype(F32)
    col = lax.broadcasted_iota(jnp.int32, (c, 128), 1).astype(F32)
    for h in range(N_HEADS):
        f = lgf[h:h + 1, :]
        b = lgb[h:h + 1, :]
        for half in range(2):
            sl = slice(half * 128, (half + 1) * 128)
            rel = row - (col + 128.0 * half)
            fwd = jnp.exp(jnp.maximum(rel, 0.0) * f)
            bwd = jnp.exp(jnp.maximum(-rel, 0.0) * b)
            d_ref[h, :, sl] = jnp.where(rel > 0, fwd, jnp.where(rel < 0, bwd, 2.0))
            kf_ref[h, :, sl] = jnp.exp((c - 1.0 - row) * f)
            kb_ref[h, :, sl] = jnp.exp(row * b)
            qf_ref[h, :, sl] = jnp.exp((row + 1.0) * f)
            qb_ref[h, :, sl] = jnp.exp((c - row) * b)
            cf_ref[h, :, sl] = jnp.exp(jnp.broadcast_to(c * f, (8, 128)))
            cb_ref[h, :, sl] = jnp.exp(jnp.broadcast_to(c * b, (8, 128)))


def _decay_tables(dec_f, dec_b):
    def lanes(d):
        return jnp.zeros((8, 128), F32).at[:N_HEADS].set(jnp.broadcast_to(d[:, None], (N_HEADS, 128)))

    big = jax.ShapeDtypeStruct((N_HEADS, RET_CHUNK, D_HEAD), F32)
    small = jax.ShapeDtypeStruct((N_HEADS, 8, D_HEAD), F32)
    return pl.pallas_call(
        _decay_kernel,
        out_shape=(big, big, big, big, big, small, small),
        name="decay_tables",
    )(lanes(dec_f), lanes(dec_b))


def _rope_tables(seq):
    rows = seq // GRID_W
    nf = D_HEAD // 4
    r = np.repeat(np.arange(rows), GRID_W).astype(np.float64)
    c = np.tile(np.arange(GRID_W), rows).astype(np.float64)
    inv = ROPE_BASE ** (-np.arange(nf, dtype=np.float64) / nf)
    ar, ac = r[:, None] * inv[None, :], c[:, None] * inv[None, :]
    cos = np.concatenate([np.cos(ar), np.cos(ar), np.cos(ac), np.cos(ac)], axis=1)
    sin = np.concatenate([-np.sin(ar), np.sin(ar), -np.sin(ac), np.sin(ac)], axis=1)
    return jnp.asarray(cos, F32), jnp.asarray(sin, F32)


def _proj_kernel(*refs, latent, tm, tiles_per_seq):
    it = iter(refs)
    x_ref = next(it)
    xp_ref, xn_ref = (next(it), next(it)) if latent else (None, None)
    mod_ref, n1_ref, w_ref, cw_ref, cb_ref, lnw_ref, lnb_ref = (next(it) for _ in range(7))
    if latent:
        cos_ref, sin_ref = next(it), next(it)
        q_ref, k_ref, v_ref, sg_ref, cu_ref, gr_ref, gc_ref = (next(it) for _ in range(7))
    else:
        gnw_ref, d_ref, kf_ref, kb_ref = (next(it) for _ in range(4))
        ro_ref, cu_ref, gr_ref, gc_ref, sf_ref, sb_ref = (next(it) for _ in range(6))
    pad_ref, uc_ref = it

    i = pl.program_id(0)
    r = (1 + i // tiles_per_seq) if latent else 0
    sh1 = mod_ref[0, pl.ds(r, 1), :]
    sc1 = mod_ref[1, pl.ds(r, 1), :]

    def norm(x):
        return (_rms(x, n1_ref[...]) * (1.0 + sc1) + sh1).astype(BF16)

    hb = norm(x_ref[...])
    n_pad = tm + 2 * HALO
    if latent:
        he = jnp.concatenate([norm(xp_ref[...]), hb, norm(xn_ref[...])], axis=0)
        row = lax.broadcasted_iota(jnp.int32, (n_pad, 1), 0)
        first = (i % tiles_per_seq) == 0
        last = (i % tiles_per_seq) == tiles_per_seq - 1
        outside = (first & (row < HALO)) | (last & (row >= HALO + tm))
    else:
        he = hb

    def proj(lhs, col, width=D_MODEL):
        return jnp.dot(lhs, w_ref[:, col:col + width], preferred_element_type=F32)

    off = HALO - CONV_WIDTH // 2
    n_sub = PROJ_CONV_ROWS // 8

    def conv_chunk(c):
        cs = slice(c * CONV_COLS, (c + 1) * CONV_COLS)
        u = (proj(he, 4 * D_MODEL + c * CONV_COLS, CONV_COLS)
             * _sigmoid(proj(he, 5 * D_MODEL + c * CONV_COLS, CONV_COLS)))
        if latent:
            pad_ref[c, 0] = jnp.where(outside, 0.0, u)
        else:
            pad_ref[c, 0, 0:HALO, :] = jnp.zeros((HALO, CONV_COLS), F32)
            pad_ref[c, 0, HALO:HALO + tm, :] = u
            pad_ref[c, 0, HALO + tm:, :] = jnp.zeros((HALO, CONV_COLS), F32)
        padded = pad_ref[c, 0]
        for s in range(1, 8):
            pad_ref[c, s] = pltpu.roll(padded, n_pad - s, axis=0)
        for rb in range(tm // PROJ_CONV_ROWS):
            base = rb * PROJ_CONV_ROWS
            acc = [cb_ref[:, cs]] * n_sub
            for j in range(CONV_WIDTH):
                s = j + off
                w = cw_ref[j, :, cs]
                for a in range(n_sub):
                    lo = base + (s // 8 + a) * 8
                    acc[a] = acc[a] + pad_ref[c, s % 8, lo:lo + 8, :] * w
            for a in range(n_sub):
                uc_ref[base + a * 8:base + (a + 1) * 8, cs] = acc[a]

    def conv_norm():
        uc = uc_ref[...]
        mu = jnp.mean(uc, axis=-1, keepdims=True)
        d = uc - mu
        var = jnp.mean(d * d, axis=-1, keepdims=True)
        ln = (d * lax.rsqrt(var + GN_EPS)) * lnw_ref[...] + lnb_ref[...]
        cu_ref[...] = (ln * _sigmoid(ln)).astype(BF16)

    def rotate(a):
        slabs = []
        for j in range(D_MODEL // 128):
            slab = a[:, j * 128:(j + 1) * 128]
            tsl = slice((j % 2) * 128, (j % 2 + 1) * 128)
            slabs.append(slab * cos_ref[:, tsl] + pltpu.roll(slab, 64, axis=1) * sin_ref[:, tsl])
        return jnp.concatenate(slabs, axis=1)

    def head(h):
        sl = slice(h * D_HEAD, (h + 1) * D_HEAD)
        qh, kh, vh = q[:, sl], k[:, sl], v[:, sl]
        p = (_dot_nt(qh, kh) * d_ref[h]).astype(BF16)
        o = jnp.dot(p, vh, preferred_element_type=F32)
        ro_ref[:, sl] = _gn_gate(o, gnw_ref[:, sl], sg[:, sl])
        kf = kh.astype(F32)
        sf_ref[0, 0, h] = _dot_tn((kf * kf_ref[h]).astype(BF16), vh)
        sb_ref[0, 0, h] = _dot_tn((kf * kb_ref[h]).astype(BF16), vh)

    conv_chunk(0)
    conv_chunk(1)
    a_q = proj(hb, 0)
    q = ((rotate(a_q) if latent else a_q) * D_HEAD ** -0.5).astype(BF16)
    a_k = proj(hb, D_MODEL)
    k = (rotate(a_k) if latent else a_k).astype(BF16)
    a_v = proj(hb, 2 * D_MODEL)
    v = a_v.astype(BF16)
    a_g = proj(hb, 3 * D_MODEL)
    sg = a_g * _sigmoid(a_g)
    if latent:
        q_ref[...] = q
        k_ref[...] = k
        v_ref[...] = v
        sg_ref[...] = sg.astype(BF16)
    else:
        for h in range(N_HEADS):
            head(h)
    conv_chunk(2)
    a_gr = proj(hb, 6 * D_MODEL)
    gr_ref[...] = _sigmoid(a_gr).astype(BF16)
    conv_chunk(3)
    a_gc = proj(hb, 7 * D_MODEL)
    gc_ref[...] = _sigmoid(a_gc).astype(BF16)
    conv_norm()


def _proj(x, mod, n1_pre, w_in, cw, cb, lnw, lnb, *, latent, seq, tm, gn_w=None, tabs=None):
    t = x.shape[0]
    tiles_per_seq = seq // tm
    hb = tm // HALO
    n_hb = t // HALO
    tile = pl.BlockSpec((tm, D_MODEL), lambda i: (i, 0))
    vec = _resident((1, D_MODEL))
    out = jax.ShapeDtypeStruct((t, D_MODEL), BF16)
    in_specs, args = [tile], [x]
    if latent:
        in_specs += [pl.BlockSpec((HALO, D_MODEL), lambda i: (jnp.maximum(i * hb - 1, 0), 0)),
                     pl.BlockSpec((HALO, D_MODEL), lambda i: (jnp.minimum((i + 1) * hb, n_hb - 1), 0))]
        args += [x, x]
    in_specs += [_resident(mod.shape), vec, _resident(w_in.shape), _resident(cw.shape), _resident(cb.shape), vec, vec]
    args += [mod, n1_pre, w_in, cw, cb, lnw, lnb]
    if latent:
        cos, sin = _rope_tables(seq)
        tab = pl.BlockSpec((tm, D_HEAD), lambda i: (i % tiles_per_seq, 0))
        in_specs += [tab, tab]
        args += [cos, sin]
        out_specs, out_shape = [tile] * 7, [out] * 7
    else:
        assert tiles_per_seq == 1 and tm == RET_CHUNK
        d, kf, kb = tabs[0], tabs[1], tabs[2]
        in_specs += [vec, _resident(d.shape), _resident(kf.shape), _resident(kb.shape)]
        args += [gn_w, d, kf, kb]
        st = pl.BlockSpec((1, 1, N_HEADS, D_HEAD, D_HEAD), lambda i: (i, 0, 0, 0, 0))
        st_shape = jax.ShapeDtypeStruct((t // tm, 1, N_HEADS, D_HEAD, D_HEAD), F32)
        out_specs, out_shape = [tile] * 4 + [st, st], [out] * 4 + [st_shape, st_shape]
    return pl.pallas_call(
        functools.partial(_proj_kernel, latent=latent, tm=tm, tiles_per_seq=tiles_per_seq),
        grid=(t // tm,),
        in_specs=in_specs,
        out_specs=out_specs,
        out_shape=out_shape,
        scratch_shapes=[pltpu.VMEM((D_MODEL // CONV_COLS, 8, tm + 2 * HALO, CONV_COLS), F32),
                        pltpu.VMEM((tm, D_MODEL), F32)],
        compiler_params=_params(("arbitrary",)),
        name="proj_lat" if latent else "proj_ctx",
    )(*args)


def _dot_nt(a, b):
    return lax.dot_general(a, b, (((1,), (1,)), ((), ())), preferred_element_type=F32)


def _dot_tn(a, b):
    return lax.dot_general(a, b, (((0,), (0,)), ((), ())), preferred_element_type=F32)


def _gn_gate(o, gnw, sg):
    mu = jnp.mean(o, axis=-1, keepdims=True)
    d = o - mu
    var = jnp.mean(d * d, axis=-1, keepdims=True)
    y = (d * lax.rsqrt(var + GN_EPS)) * gnw
    return (sg.astype(F32) * y).astype(BF16)


def _ret_lat_kernel(q_ref, k_ref, v_ref, sg_ref, gnw_ref, sf0_ref, sb0_ref,
                    d_ref, kf_ref, kb_ref, qf_ref, qb_ref, cf_ref, cb_ref,
                    ro_ref, sbs_ref, *, n_chunks):
    c = RET_CHUNK

    def rows(j):
        return pl.ds(pl.multiple_of(j * c, c), c)

    def cols(h):
        return slice(h * D_HEAD, (h + 1) * D_HEAD)

    def bwd(jj, states):
        j = n_chunks - 1 - jj
        new = []
        for h, s in enumerate(states):
            sbs_ref[h, j] = s
            kb = (k_ref[rows(j), cols(h)].astype(F32) * kb_ref[h]).astype(BF16)
            new.append(s * cb_ref[h, 0:1, :] + _dot_tn(kb, v_ref[rows(j), cols(h)]))
        return tuple(new)

    lax.fori_loop(0, n_chunks, bwd, tuple(sb0_ref[0, 0, h] for h in range(LAT_HEADS)), unroll=LAT_UNROLL)

    def fwd(j, states):
        new = []
        for h, s in enumerate(states):
            q, k, v = q_ref[rows(j), cols(h)], k_ref[rows(j), cols(h)], v_ref[rows(j), cols(h)]
            qf32, kf32 = q.astype(F32), k.astype(F32)
            p = (_dot_nt(q, k) * d_ref[h]).astype(BF16)
            o = jnp.dot(p, v, preferred_element_type=F32)
            o = o + jnp.dot((qf32 * qf_ref[h]).astype(BF16), s.astype(BF16), preferred_element_type=F32)
            o = o + jnp.dot((qf32 * qb_ref[h]).astype(BF16), sbs_ref[h, j].astype(BF16),
                            preferred_element_type=F32)
            ro_ref[rows(j), cols(h)] = _gn_gate(o, gnw_ref[:, cols(h)], sg_ref[rows(j), cols(h)])
            new.append(s * cf_ref[h, 0:1, :] + _dot_tn((kf32 * kf_ref[h]).astype(BF16), v))
        return tuple(new)

    lax.fori_loop(0, n_chunks, fwd, tuple(sf0_ref[0, 0, h] for h in range(LAT_HEADS)), unroll=LAT_UNROLL)


def _ret_lat(q, k, v, sg, gn_w, sf0, sb0, tabs, batch, seq):
    n_chunks = seq // RET_CHUNK
    w = LAT_HEADS * D_HEAD
    tile = pl.BlockSpec((seq, w), lambda s, h: (s, h))
    st = pl.BlockSpec((1, 1, LAT_HEADS, D_HEAD, D_HEAD), lambda s, h: (s, 0, h, 0, 0))
    big = pl.BlockSpec((LAT_HEADS, RET_CHUNK, D_HEAD), lambda s, h: (h, 0, 0))
    small = pl.BlockSpec((LAT_HEADS, 8, D_HEAD), lambda s, h: (h, 0, 0))
    return pl.pallas_call(
        functools.partial(_ret_lat_kernel, n_chunks=n_chunks),
        grid=(batch, N_HEADS // LAT_HEADS),
        in_specs=[tile, tile, tile, tile, pl.BlockSpec((1, w), lambda s, h: (0, h)),
                  st, st, big, big, big, big, big, small, small],
        out_specs=tile,
        out_shape=jax.ShapeDtypeStruct((batch * seq, D_MODEL), BF16),
        scratch_shapes=[pltpu.VMEM((LAT_HEADS, n_chunks, D_HEAD, D_HEAD), F32)],
        compiler_params=_params(("arbitrary", "arbitrary")),
        name="ret_lat",
    )(q, k, v, sg, gn_w, sf0, sb0, *tabs)


def _mixer_kernel(x_ref, ro_ref, cu_ref, gr_ref, gc_ref, mod_ref, wro_ref, wco_ref, wmx_ref, n1p_ref, n2_ref,
                  x1_ref, h2_ref, *, tiles_per_seq, per_seq_mod):
    r = (1 + pl.program_id(0) // tiles_per_seq) if per_seq_mod else 0
    g1 = mod_ref[2, pl.ds(r, 1), :]
    sh2 = mod_ref[3, pl.ds(r, 1), :]
    sc2 = mod_ref[4, pl.ds(r, 1), :]
    conv_out = jnp.dot(cu_ref[...], wco_ref[...], preferred_element_type=F32)
    ret_out = jnp.dot(ro_ref[...], wro_ref[...], preferred_element_type=F32)
    mixed = gr_ref[...].astype(F32) * ret_out + gc_ref[...].astype(F32) * conv_out
    m = jnp.dot(mixed.astype(BF16), wmx_ref[...], preferred_element_type=F32)
    x1 = x_ref[...] + g1 * _rms(m, n1p_ref[...])
    x1_ref[...] = x1
    h2_ref[...] = (_rms(x1, n2_ref[...]) * (1.0 + sc2) + sh2).astype(BF16)


def _mixer(x, ro, cu, gr, gc, mod, wro, wco, wmx, n1p, n2, *, seq, tm, per_seq_mod):
    t = x.shape[0]
    tile = pl.BlockSpec((tm, D_MODEL), lambda i: (i, 0))
    vec = _resident((1, D_MODEL))
    mat = _resident((D_MODEL, D_MODEL))
    return pl.pallas_call(
        functools.partial(_mixer_kernel, tiles_per_seq=max(seq // tm, 1), per_seq_mod=per_seq_mod),
        grid=(t // tm,),
        in_specs=[tile, tile, tile, tile, tile, _resident(mod.shape), mat, mat, mat, vec, vec],
        out_specs=[tile, tile],
        out_shape=[jax.ShapeDtypeStruct((t, D_MODEL), F32), jax.ShapeDtypeStruct((t, D_MODEL), BF16)],
        compiler_params=_params(("arbitrary",)),
        name="mixer_lat" if per_seq_mod else "mixer_ctx",
    )(x, ro, cu, gr, gc, mod, wro, wco, wmx, n1p, n2)


def _ffn_kernel(x1_ref, h2_ref, hp_ref, hn_ref, mod_ref, wup_ref, fw_ref, fb_ref, wdn_ref, n2p_ref,
                y_ref, *, tm, tiles_per_seq, per_seq_mod):
    i = pl.program_id(0)
    r = (1 + i // tiles_per_seq) if per_seq_mod else 0
    g2 = mod_ref[5, pl.ds(r, 1), :]
    first = (i % tiles_per_seq) == 0
    last = (i % tiles_per_seq) == tiles_per_seq - 1

    n_ext = tm + HALO
    row = lax.broadcasted_iota(jnp.int32, (HALO, 1), 0)
    zero = jnp.zeros((), BF16)
    halo = jnp.where(row < HALO // 2, jnp.where(last, zero, hn_ref[...]), jnp.where(first, zero, hp_ref[...]))
    he = jnp.concatenate([h2_ref[...], halo], axis=0)

    inner = slice(0, tm)
    n_chunks = D_FF // FFN_COLS

    def cols(cc, half):
        return slice(half * D_FF + cc * FFN_COLS, half * D_FF + (cc + 1) * FFN_COLS)

    def up_proj(cc):
        return [jnp.dot(he, wup_ref[:, cols(cc, half)], preferred_element_type=F32) for half in range(2)]

    def conv3(up, cs):
        before = pltpu.roll(up, 1, axis=0)[inner]
        after = pltpu.roll(up, n_ext - 1, axis=0)[inner]
        return (fb_ref[:, cs] + before * fw_ref[0:1, cs] + up[inner] * fw_ref[1:2, cs]
                + after * fw_ref[2:3, cs])

    f = None
    ups = [up_proj(cc) for cc in range(FFN_AHEAD)]
    for cc in range(n_chunks):
        up = ups[cc]
        if cc + FFN_AHEAD < n_chunks:
            ups.append(up_proj(cc + FFN_AHEAD))
        a, gl = conv3(up[0], cols(cc, 0)), conv3(up[1], cols(cc, 1))
        act = ((a * _sigmoid(a)) * gl).astype(BF16)
        part = jnp.dot(act, wdn_ref[cc * FFN_COLS:(cc + 1) * FFN_COLS, :], preferred_element_type=F32)
        f = part if f is None else f + part
    y_ref[...] = x1_ref[...] + g2 * _rms(f, n2p_ref[...])


def _ffn(x1, h2, mod, wup, fw, fb, wdn, n2p, *, seq, tm, per_seq_mod):
    t = x1.shape[0]
    tiles_per_seq = seq // tm
    hb = tm // HALO
    n_hb = t // HALO
    tile = pl.BlockSpec((tm, D_MODEL), lambda i: (i, 0))
    prev = pl.BlockSpec((HALO, D_MODEL), lambda i: (jnp.maximum(i * hb - 1, 0), 0))
    nxt = pl.BlockSpec((HALO, D_MODEL), lambda i: (jnp.minimum((i + 1) * hb, n_hb - 1), 0))
    return pl.pallas_call(
        functools.partial(_ffn_kernel, tm=tm, tiles_per_seq=tiles_per_seq, per_seq_mod=per_seq_mod),
        grid=(t // tm,),
        in_specs=[tile, tile, prev, nxt, _resident(mod.shape), _resident(wup.shape),
                  _resident(fw.shape), _resident(fb.shape), _resident(wdn.shape), _resident((1, D_MODEL))],
        out_specs=tile,
        out_shape=jax.ShapeDtypeStruct((t, D_MODEL), F32),
        compiler_params=_params(("arbitrary",)),
        name="ffn_lat" if per_seq_mod else "ffn_ctx",
    )(x1, h2, h2, h2, mod, wup, fw, fb, wdn, n2p)


def kernel(x_prompt, x_sample, state_ret_fwd, state_ret_bwd, c, c_ctx, norm1_pre, norm1_post, norm2_pre, norm2_post, ada_w, ada_b, w_in, ret_decay_fwd, ret_decay_bwd, ret_gn_w, ret_w_out, conv_dw_w, conv_dw_b, conv_ln_w, conv_ln_b, conv_w_out, w_mix_out, ffn_w_up, ffn_dw_w, ffn_dw_b, ffn_w_down):
    batch, seq, _ = x_prompt.shape
    dec_batch, dec_seq, _ = x_sample.shape
    assert norm1_pre.shape[0] == 1 and dec_batch + 1 <= 8

    cond8 = jnp.zeros((8, D_MODEL), F32).at[0].set(c_ctx).at[1:1 + dec_batch].set(c)
    mod = _adaln(cond8, ada_w[0], ada_b[0])
    tabs = _decay_tables(ret_decay_fwd[0], ret_decay_bwd[0])

    w_in_b = w_in[0].astype(BF16)
    wro, wco, wmx = (w[0].astype(BF16) for w in (ret_w_out, conv_w_out, w_mix_out))
    wdn = ffn_w_down[0].astype(BF16)
    cw = jnp.broadcast_to(conv_dw_w[0][:, None, :], (CONV_WIDTH, 8, D_MODEL))
    cb = jnp.broadcast_to(conv_dw_b, (8, D_MODEL))
    wup = ffn_w_up[0].astype(BF16)
    fw = jnp.zeros((8, 2 * D_FF), F32).at[:FFN_CONV_WIDTH].set(ffn_dw_w[0])
    fb = ffn_dw_b

    def tail(x, ro, cu, gr, gc, *, latent, seq_len):
        x1, h2 = _mixer(x, ro, cu, gr, gc, mod, wro, wco, wmx, norm1_post, norm2_pre,
                        seq=seq_len, tm=512, per_seq_mod=latent)
        return _ffn(x1, h2, mod, wup, fw, fb, wdn, norm2_post, seq=seq_len, tm=256, per_seq_mod=latent)

    conv = (cw, cb, conv_ln_w, conv_ln_b)
    xp = x_prompt.reshape(batch * seq, D_MODEL)
    ro, cu, gr, gc, sf, sb = _proj(xp, mod, norm1_pre, w_in_b, *conv, latent=False, seq=seq, tm=seq,
                                   gn_w=ret_gn_w, tabs=tabs)
    yp = tail(xp, ro, cu, gr, gc, latent=False, seq_len=seq)

    xs = x_sample.reshape(dec_batch * dec_seq, D_MODEL)
    q, k, v, sg, cu, gr, gc = _proj(xs, mod, norm1_pre, w_in_b, *conv, latent=True, seq=dec_seq, tm=256)
    ro = _ret_lat(q, k, v, sg, ret_gn_w, state_ret_fwd, state_ret_bwd, tabs, dec_batch, dec_seq)
    ys = tail(xs, ro, cu, gr, gc, latent=True, seq_len=dec_seq)
    return (yp.reshape(batch, seq, D_MODEL), ys.reshape(dec_batch, dec_seq, D_MODEL), sf, sb)
```

```python
import functools

import numpy as np
import jax
import jax.numpy as jnp
from jax import lax
from jax.experimental import pallas as pl
from jax.experimental.pallas import tpu as pltpu

F32 = jnp.float32
BF16 = jnp.bfloat16

D_MODEL = 1024
N_HEADS = 4
D_HEAD = 256
D_FF = 2816
CONV_WIDTH = 31
FFN_CONV_WIDTH = 3
GRID_W = 64
ROPE_BASE = 10000.0
EPS = 1e-6
GN_EPS = 1e-5
RET_CHUNK = 256
ADALN_CHUNKS = 3
LAT_HEADS = 2
LAT_UNROLL = 4
HALO = 16
CONV_COLS = 256
PROJ_CONV_ROWS = 64
FFN_COLS = 256
FFN_AHEAD = 3
VMEM_LIMIT = 56 * 1024 * 1024


def _sigmoid(x):
    return 1.0 / (1.0 + jnp.exp(-x))


def _rms(x, w):
    ms = jnp.mean(x * x, axis=-1, keepdims=True)
    return (x * lax.rsqrt(ms + EPS)) * w


def _params(sem):
    return pltpu.CompilerParams(dimension_semantics=sem, vmem_limit_bytes=VMEM_LIMIT)


def _resident(shape):
    nd = len(shape)
    return pl.BlockSpec(shape, lambda *_: (0,) * nd, pipeline_mode=pl.Buffered(1))


def _adaln_kernel(cond_ref, w_ref, b_ref, o_ref):
    c = cond_ref[...]
    s = (c * _sigmoid(c)).astype(BF16)
    for j in range(ADALN_CHUNKS):
        w = w_ref[:, j * D_MODEL:(j + 1) * D_MODEL].astype(BF16)
        o_ref[j] = jnp.dot(s, w, preferred_element_type=F32) + b_ref[j]


def _adaln(cond8, ada_w, ada_b):
    n = ADALN_CHUNKS
    return pl.pallas_call(
        _adaln_kernel,
        grid=(6 // n,),
        in_specs=[pl.BlockSpec((8, D_MODEL), lambda j: (0, 0)),
                  pl.BlockSpec((D_MODEL, n * D_MODEL), lambda j: (0, j)),
                  pl.BlockSpec((n, 1, D_MODEL), lambda j: (j, 0, 0))],
        out_specs=pl.BlockSpec((n, 8, D_MODEL), lambda j: (j, 0, 0)),
        out_shape=jax.ShapeDtypeStruct((6, 8, D_MODEL), F32),
        compiler_params=_params(("arbitrary",)),
        name="adaln",
    )(cond8, ada_w, ada_b.reshape(6, 1, D_MODEL))


def _decay_kernel(df_ref, db_ref, d_ref, kf_ref, kb_ref, qf_ref, qb_ref, cf_ref, cb_ref):
    def log_sigmoid(x):
        return -(jnp.maximum(-x, 0.0) + jnp.log1p(jnp.exp(-jnp.abs(x))))

    lgf = log_sigmoid(df_ref[...])
    lgb = log_sigmoid(db_ref[...])
    c = RET_CHUNK
    row = lax.broadcasted_iota(jnp.int32, (c, 128), 0).astype(F32)
    col = lax.broadcasted_iota(jnp.int32, (c, 128), 1).astype(F32)
    for h in range(N_HEADS):
        f = lgf[h:h + 1, :]
        b = lgb[h:h + 1, :]
        for half in range(2):
            sl = slice(half * 128, (half + 1) * 128)
            rel = row - (col + 128.0 * half)
            fwd = jnp.exp(jnp.maximum(rel, 0.0) * f)
            bwd = jnp.exp(jnp.maximum(-rel, 0.0) * b)
            d_ref[h, :, sl] = jnp.where(rel > 0, fwd, jnp.where(rel < 0, bwd, 2.0))
            kf_ref[h, :, sl] = jnp.exp((c - 1.0 - row) * f)
            kb_ref[h, :, sl] = jnp.exp(row * b)
            qf_ref[h, :, sl] = jnp.exp((row + 1.0) * f)
            qb_ref[h, :, sl] = jnp.exp((c - row) * b)
            cf_ref[h, :, sl] = jnp.exp(jnp.broadcast_to(c * f, (8, 128)))
            cb_ref[h, :, sl] = jnp.exp(jnp.broadcast_to(c * b, (8, 128)))


def _decay_tables(dec_f, dec_b):
    def lanes(d):
        return jnp.zeros((8, 128), F32).at[:N_HEADS].set(jnp.broadcast_to(d[:, None], (N_HEADS, 128)))

    big = jax.ShapeDtypeStruct((N_HEADS, RET_CHUNK, D_HEAD), F32)
    small = jax.ShapeDtypeStruct((N_HEADS, 8, D_HEAD), F32)
    return pl.pallas_call(
        _decay_kernel,
        out_shape=(big, big, big, big, big, small, small),
        name="decay_tables",
    )(lanes(dec_f), lanes(dec_b))


def _rope_tables(seq):
    rows = seq // GRID_W
    nf = D_HEAD // 4
    r = np.repeat(np.arange(rows), GRID_W).astype(np.float64)
    c = np.tile(np.arange(GRID_W), rows).astype(np.float64)
    inv = ROPE_BASE ** (-np.arange(nf, dtype=np.float64) / nf)
    ar, ac = r[:, None] * inv[None, :], c[:, None] * inv[None, :]
    cos = np.concatenate([np.cos(ar), np.cos(ar), np.cos(ac), np.cos(ac)], axis=1)
    sin = np.concatenate([-np.sin(ar), np.sin(ar), -np.sin(ac), np.sin(ac)], axis=1)
    return jnp.asarray(cos, F32), jnp.asarray(sin, F32)


def _proj_kernel(*refs, latent, tm, tiles_per_seq, n_later=0):
    it = iter(refs)
    x_ref = next(it)
    xp_ref, xn_ref = (next(it), next(it)) if latent else (None, None)
    mod_ref, n1_ref, w_ref, cw_ref, cb_ref, lnw_ref, lnb_ref = (next(it) for _ in range(7))
    if latent:
        cos_ref, sin_ref = next(it), next(it)
        q_ref, k_ref, v_ref, sg_ref, cu_ref, gr_ref, gc_ref = (next(it) for _ in range(7))
    else:
        gnw_ref, d_ref, kf_ref, kb_ref = (next(it) for _ in range(4))
        later_f32 = [next(it) for _ in range(n_later)]
        ro_ref, cu_ref, gr_ref, gc_ref, sf_ref, sb_ref = (next(it) for _ in range(6))
        for src, dst in zip(later_f32, [next(it) for _ in range(n_later)]):
            dst[...] = src[...].astype(BF16)
    pad_ref, uc_ref = it

    i = pl.program_id(0)
    r = (1 + i // tiles_per_seq) if latent else 0
    sh1 = mod_ref[0, pl.ds(r, 1), :]
    sc1 = mod_ref[1, pl.ds(r, 1), :]

    def norm(x):
        return (_rms(x, n1_ref[...]) * (1.0 + sc1) + sh1).astype(BF16)

    hb = norm(x_ref[...])
    n_pad = tm + 2 * HALO
    if latent:
        he = jnp.concatenate([norm(xp_ref[...]), hb, norm(xn_ref[...])], axis=0)
        row = lax.broadcasted_iota(jnp.int32, (n_pad, 1), 0)
        first = (i % tiles_per_seq) == 0
        last = (i % tiles_per_seq) == tiles_per_seq - 1
        outside = (first & (row < HALO)) | (last & (row >= HALO + tm))
    else:
        he = hb

    def proj(lhs, col, width=D_MODEL):
        return jnp.dot(lhs, w_ref[:, col:col + width], preferred_element_type=F32)

    off = HALO - CONV_WIDTH // 2
    n_sub = PROJ_CONV_ROWS // 8

    def conv_chunk(c):
        cs = slice(c * CONV_COLS, (c + 1) * CONV_COLS)
        u = (proj(he, 4 * D_MODEL + c * CONV_COLS, CONV_COLS)
             * _sigmoid(proj(he, 5 * D_MODEL + c * CONV_COLS, CONV_COLS)))
        if latent:
            pad_ref[c, 0] = jnp.where(outside, 0.0, u)
        else:
            pad_ref[c, 0, 0:HALO, :] = jnp.zeros((HALO, CONV_COLS), F32)
            pad_ref[c, 0, HALO:HALO + tm, :] = u
            pad_ref[c, 0, HALO + tm:, :] = jnp.zeros((HALO, CONV_COLS), F32)
        padded = pad_ref[c, 0]
        for s in range(1, 8):
            pad_ref[c, s] = pltpu.roll(padded, n_pad - s, axis=0)
        for rb in range(tm // PROJ_CONV_ROWS):
            base = rb * PROJ_CONV_ROWS
            acc = [cb_ref[:, cs]] * n_sub
            for j in range(CONV_WIDTH):
                s = j + off
                w = cw_ref[j, :, cs]
                for a in range(n_sub):
                    lo = base + (s // 8 + a) * 8
                    acc[a] = acc[a] + pad_ref[c, s % 8, lo:lo + 8, :] * w
            for a in range(n_sub):
                uc_ref[base + a * 8:base + (a + 1) * 8, cs] = acc[a]

    def conv_norm():
        uc = uc_ref[...]
        mu = jnp.mean(uc, axis=-1, keepdims=True)
        d = uc - mu
        var = jnp.mean(d * d, axis=-1, keepdims=True)
        ln = (d * lax.rsqrt(var + GN_EPS)) * lnw_ref[...] + lnb_ref[...]
        cu_ref[...] = (ln * _sigmoid(ln)).astype(BF16)

    def rotate(a):
        slabs = []
        for j in range(D_MODEL // 128):
            slab = a[:, j * 128:(j + 1) * 128]
            tsl = slice((j % 2) * 128, (j % 2 + 1) * 128)
            slabs.append(slab * cos_ref[:, tsl] + pltpu.roll(slab, 64, axis=1) * sin_ref[:, tsl])
        return jnp.concatenate(slabs, axis=1)

    def head(h):
        sl = slice(h * D_HEAD, (h + 1) * D_HEAD)
        qh, kh, vh = q[:, sl], k[:, sl], v[:, sl]
        p = (_dot_nt(qh, kh) * d_ref[h]).astype(BF16)
        o = jnp.dot(p, vh, preferred_element_type=F32)
        ro_ref[:, sl] = _gn_gate(o, gnw_ref[:, sl], sg[:, sl])
        kf = kh.astype(F32)
        sf_ref[0, 0, h] = _dot_tn((kf * kf_ref[h]).astype(BF16), vh)
        sb_ref[0, 0, h] = _dot_tn((kf * kb_ref[h]).astype(BF16), vh)

    conv_chunk(0)
    conv_chunk(1)
    a_q = proj(hb, 0)
    q = ((rotate(a_q) if latent else a_q) * D_HEAD ** -0.5).astype(BF16)
    a_k = proj(hb, D_MODEL)
    k = (rotate(a_k) if latent else a_k).astype(BF16)
    a_v = proj(hb, 2 * D_MODEL)
    v = a_v.astype(BF16)
    a_g = proj(hb, 3 * D_MODEL)
    sg = a_g * _sigmoid(a_g)
    if latent:
        q_ref[...] = q
        k_ref[...] = k
        v_ref[...] = v
        sg_ref[...] = sg.astype(BF16)
    else:
        for h in range(N_HEADS):
            head(h)
    conv_chunk(2)
    a_gr = proj(hb, 6 * D_MODEL)
    gr_ref[...] = _sigmoid(a_gr).astype(BF16)
    conv_chunk(3)
    a_gc = proj(hb, 7 * D_MODEL)
    gc_ref[...] = _sigmoid(a_gc).astype(BF16)
    conv_norm()


def _proj(x, mod, n1_pre, w_in, cw, cb, lnw, lnb, *, latent, seq, tm, gn_w=None, tabs=None, later=()):
    t = x.shape[0]
    tiles_per_seq = seq // tm
    hb = tm // HALO
    n_hb = t // HALO
    tile = pl.BlockSpec((tm, D_MODEL), lambda i: (i, 0))
    vec = _resident((1, D_MODEL))
    out = jax.ShapeDtypeStruct((t, D_MODEL), BF16)
    in_specs, args = [tile], [x]
    if latent:
        in_specs += [pl.BlockSpec((HALO, D_MODEL), lambda i: (jnp.maximum(i * hb - 1, 0), 0)),
                     pl.BlockSpec((HALO, D_MODEL), lambda i: (jnp.minimum((i + 1) * hb, n_hb - 1), 0))]
        args += [x, x]
    in_specs += [_resident(mod.shape), vec, _resident(w_in.shape), _resident(cw.shape), _resident(cb.shape), vec, vec]
    args += [mod, n1_pre, w_in, cw, cb, lnw, lnb]
    if latent:
        cos, sin = _rope_tables(seq)
        tab = pl.BlockSpec((tm, D_HEAD), lambda i: (i % tiles_per_seq, 0))
        in_specs += [tab, tab]
        args += [cos, sin]
        out_specs, out_shape = [tile] * 7, [out] * 7
    else:
        assert tiles_per_seq == 1 and tm == RET_CHUNK
        d, kf, kb = tabs[0], tabs[1], tabs[2]
        in_specs += [vec, _resident(d.shape), _resident(kf.shape), _resident(kb.shape)]
        args += [gn_w, d, kf, kb]
        st = pl.BlockSpec((1, 1, N_HEADS, D_HEAD, D_HEAD), lambda i: (i, 0, 0, 0, 0))
        st_shape = jax.ShapeDtypeStruct((t // tm, 1, N_HEADS, D_HEAD, D_HEAD), F32)
        out_specs, out_shape = [tile] * 4 + [st, st], [out] * 4 + [st_shape, st_shape]
        steps = t // tm
        for w in later:
            rows, cols = w.shape
            n = max(n for n in range(1, steps + 1) if rows % n == 0 and (rows // n) % HALO == 0)
            blk = pl.BlockSpec((rows // n, cols), lambda i, n=n: (jnp.minimum(i, n - 1), 0))
            in_specs.append(blk)
            args.append(w)
            out_specs.append(blk)
            out_shape.append(jax.ShapeDtypeStruct(w.shape, BF16))
    return pl.pallas_call(
        functools.partial(_proj_kernel, latent=latent, tm=tm, tiles_per_seq=tiles_per_seq, n_later=len(later)),
        grid=(t // tm,),
        in_specs=in_specs,
        out_specs=out_specs,
        out_shape=out_shape,
        scratch_shapes=[pltpu.VMEM((D_MODEL // CONV_COLS, 8, tm + 2 * HALO, CONV_COLS), F32),
                        pltpu.VMEM((tm, D_MODEL), F32)],
        compiler_params=_params(("arbitrary",)),
        name="proj_lat" if latent else "proj_ctx",
    )(*args)


def _dot_nt(a, b):
    return lax.dot_general(a, b, (((1,), (1,)), ((), ())), preferred_element_type=F32)


def _dot_tn(a, b):
    return lax.dot_general(a, b, (((0,), (0,)), ((), ())), preferred_element_type=F32)


def _gn_gate(o, gnw, sg):
    mu = jnp.mean(o, axis=-1, keepdims=True)
    d = o - mu
    var = jnp.mean(d * d, axis=-1, keepdims=True)
    y = (d * lax.rsqrt(var + GN_EPS)) * gnw
    return (sg.astype(F32) * y).astype(BF16)


def _ret_lat_kernel(q_ref, k_ref, v_ref, sg_ref, gnw_ref, sf0_ref, sb0_ref,
                    d_ref, kf_ref, kb_ref, qf_ref, qb_ref, cf_ref, cb_ref,
                    ro_ref, sbs_ref, *, n_chunks):
    c = RET_CHUNK

    def rows(j):
        return pl.ds(pl.multiple_of(j * c, c), c)

    def cols(h):
        return slice(h * D_HEAD, (h + 1) * D_HEAD)

    def bwd(jj, states):
        j = n_chunks - 1 - jj
        new = []
        for h, s in enumerate(states):
            sbs_ref[h, j] = s
            kb = (k_ref[rows(j), cols(h)].astype(F32) * kb_ref[h]).astype(BF16)
            new.append(s * cb_ref[h, 0:1, :] + _dot_tn(kb, v_ref[rows(j), cols(h)]))
        return tuple(new)

    lax.fori_loop(0, n_chunks, bwd, tuple(sb0_ref[0, 0, h] for h in range(LAT_HEADS)), unroll=LAT_UNROLL)

    def fwd(j, states):
        new = []
        for h, s in enumerate(states):
            q, k, v = q_ref[rows(j), cols(h)], k_ref[rows(j), cols(h)], v_ref[rows(j), cols(h)]
            qf32, kf32 = q.astype(F32), k.astype(F32)
            p = (_dot_nt(q, k) * d_ref[h]).astype(BF16)
            o = jnp.dot(p, v, preferred_element_type=F32)
            o = o + jnp.dot((qf32 * qf_ref[h]).astype(BF16), s.astype(BF16), preferred_element_type=F32)
            o = o + jnp.dot((qf32 * qb_ref[h]).astype(BF16), sbs_ref[h, j].astype(BF16),
                            preferred_element_type=F32)
            ro_ref[rows(j), cols(h)] = _gn_gate(o, gnw_ref[:, cols(h)], sg_ref[rows(j), cols(h)])
            new.append(s * cf_ref[h, 0:1, :] + _dot_tn((kf32 * kf_ref[h]).astype(BF16), v))
        return tuple(new)

    lax.fori_loop(0, n_chunks, fwd, tuple(sf0_ref[0, 0, h] for h in range(LAT_HEADS)), unroll=LAT_UNROLL)


def _ret_lat(q, k, v, sg, gn_w, sf0, sb0, tabs, batch, seq):
    n_chunks = seq // RET_CHUNK
    w = LAT_HEADS * D_HEAD
    tile = pl.BlockSpec((seq, w), lambda s, h: (s, h))
    st = pl.BlockSpec((1, 1, LAT_HEADS, D_HEAD, D_HEAD), lambda s, h: (s, 0, h, 0, 0))
    big = pl.BlockSpec((LAT_HEADS, RET_CHUNK, D_HEAD), lambda s, h: (h, 0, 0))
    small = pl.BlockSpec((LAT_HEADS, 8, D_HEAD), lambda s, h: (h, 0, 0))
    return pl.pallas_call(
        functools.partial(_ret_lat_kernel, n_chunks=n_chunks),
        grid=(batch, N_HEADS // LAT_HEADS),
        in_specs=[tile, tile, tile, tile, pl.BlockSpec((1, w), lambda s, h: (0, h)),
                  st, st, big, big, big, big, big, small, small],
        out_specs=tile,
        out_shape=jax.ShapeDtypeStruct((batch * seq, D_MODEL), BF16),
        scratch_shapes=[pltpu.VMEM((LAT_HEADS, n_chunks, D_HEAD, D_HEAD), F32)],
        compiler_params=_params(("arbitrary", "arbitrary")),
        name="ret_lat",
    )(q, k, v, sg, gn_w, sf0, sb0, *tabs)


def _mixer_kernel(x_ref, ro_ref, cu_ref, gr_ref, gc_ref, mod_ref, wro_ref, wco_ref, wmx_ref, n1p_ref, n2_ref,
                  x1_ref, h2_ref, *, tiles_per_seq, per_seq_mod):
    r = (1 + pl.program_id(0) // tiles_per_seq) if per_seq_mod else 0
    g1 = mod_ref[2, pl.ds(r, 1), :]
    sh2 = mod_ref[3, pl.ds(r, 1), :]
    sc2 = mod_ref[4, pl.ds(r, 1), :]
    conv_out = jnp.dot(cu_ref[...], wco_ref[...], preferred_element_type=F32)
    ret_out = jnp.dot(ro_ref[...], wro_ref[...], preferred_element_type=F32)
    mixed = gr_ref[...].astype(F32) * ret_out + gc_ref[...].astype(F32) * conv_out
    m = jnp.dot(mixed.astype(BF16), wmx_ref[...], preferred_element_type=F32)
    x1 = x_ref[...] + g1 * _rms(m, n1p_ref[...])
    x1_ref[...] = x1
    h2_ref[...] = (_rms(x1, n2_ref[...]) * (1.0 + sc2) + sh2).astype(BF16)


def _mixer(x, ro, cu, gr, gc, mod, wro, wco, wmx, n1p, n2, *, seq, tm, per_seq_mod):
    t = x.shape[0]
    tile = pl.BlockSpec((tm, D_MODEL), lambda i: (i, 0))
    vec = _resident((1, D_MODEL))
    mat = _resident((D_MODEL, D_MODEL))
    return pl.pallas_call(
        functools.partial(_mixer_kernel, tiles_per_seq=max(seq // tm, 1), per_seq_mod=per_seq_mod),
        grid=(t // tm,),
        in_specs=[tile, tile, tile, tile, tile, _resident(mod.shape), mat, mat, mat, vec, vec],
        out_specs=[tile, tile],
        out_shape=[jax.ShapeDtypeStruct((t, D_MODEL), F32), jax.ShapeDtypeStruct((t, D_MODEL), BF16)],
        compiler_params=_params(("arbitrary",)),
        name="mixer_lat" if per_seq_mod else "mixer_ctx",
    )(x, ro, cu, gr, gc, mod, wro, wco, wmx, n1p, n2)


def _ffn_kernel(x1_ref, h2_ref, hp_ref, hn_ref, mod_ref, wup_ref, fw_ref, fb_ref, wdn_ref, n2p_ref,
                y_ref, *, tm, tiles_per_seq, per_seq_mod):
    i = pl.program_id(0)
    r = (1 + i // tiles_per_seq) if per_seq_mod else 0
    g2 = mod_ref[5, pl.ds(r, 1), :]
    first = (i % tiles_per_seq) == 0
    last = (i % tiles_per_seq) == tiles_per_seq - 1

    n_ext = tm + HALO
    row = lax.broadcasted_iota(jnp.int32, (HALO, 1), 0)
    zero = jnp.zeros((), BF16)
    halo = jnp.where(row < HALO // 2, jnp.where(last, zero, hn_ref[...]), jnp.where(first, zero, hp_ref[...]))
    he = jnp.concatenate([h2_ref[...], halo], axis=0)

    inner = slice(0, tm)
    n_chunks = D_FF // FFN_COLS

    def cols(cc, half):
        return slice(half * D_FF + cc * FFN_COLS, half * D_FF + (cc + 1) * FFN_COLS)

    def up_proj(cc):
        return [jnp.dot(he, wup_ref[:, cols(cc, half)], preferred_element_type=F32) for half in range(2)]

    def conv3(up, cs):
        before = pltpu.roll(up, 1, axis=0)[inner]
        after = pltpu.roll(up, n_ext - 1, axis=0)[inner]
        return (fb_ref[:, cs] + before * fw_ref[0:1, cs] + up[inner] * fw_ref[1:2, cs]
                + after * fw_ref[2:3, cs])

    f = None
    ups = [up_proj(cc) for cc in range(FFN_AHEAD)]
    for cc in range(n_chunks):
        up = ups[cc]
        if cc + FFN_AHEAD < n_chunks:
            ups.append(up_proj(cc + FFN_AHEAD))
        a, gl = conv3(up[0], cols(cc, 0)), conv3(up[1], cols(cc, 1))
        act = ((a * _sigmoid(a)) * gl).astype(BF16)
        part = jnp.dot(act, wdn_ref[cc * FFN_COLS:(cc + 1) * FFN_COLS, :], preferred_element_type=F32)
        f = part if f is None else f + part
    y_ref[...] = x1_ref[...] + g2 * _rms(f, n2p_ref[...])


def _ffn(x1, h2, mod, wup, fw, fb, wdn, n2p, *, seq, tm, per_seq_mod):
    t = x1.shape[0]
    tiles_per_seq = seq // tm
    hb = tm // HALO
    n_hb = t // HALO
    tile = pl.BlockSpec((tm, D_MODEL), lambda i: (i, 0))
    prev = pl.BlockSpec((HALO, D_MODEL), lambda i: (jnp.maximum(i * hb - 1, 0), 0))
    nxt = pl.BlockSpec((HALO, D_MODEL), lambda i: (jnp.minimum((i + 1) * hb, n_hb - 1), 0))
    return pl.pallas_call(
        functools.partial(_ffn_kernel, tm=tm, tiles_per_seq=tiles_per_seq, per_seq_mod=per_seq_mod),
        grid=(t // tm,),
        in_specs=[tile, tile, prev, nxt, _resident(mod.shape), _resident(wup.shape),
                  _resident(fw.shape), _resident(fb.shape), _resident(wdn.shape), _resident((1, D_MODEL))],
        out_specs=tile,
        out_shape=jax.ShapeDtypeStruct((t, D_MODEL), F32),
        compiler_params=_params(("arbitrary",)),
        name="ffn_lat" if per_seq_mod else "ffn_ctx",
    )(x1, h2, h2, h2, mod, wup, fw, fb, wdn, n2p)


def kernel(x_prompt, x_sample, state_ret_fwd, state_ret_bwd, c, c_ctx, norm1_pre, norm1_post, norm2_pre, norm2_post, ada_w, ada_b, w_in, ret_decay_fwd, ret_decay_bwd, ret_gn_w, ret_w_out, conv_dw_w, conv_dw_b, conv_ln_w, conv_ln_b, conv_w_out, w_mix_out, ffn_w_up, ffn_dw_w, ffn_dw_b, ffn_w_down):
    batch, seq, _ = x_prompt.shape
    dec_batch, dec_seq, _ = x_sample.shape
    assert norm1_pre.shape[0] == 1 and dec_batch + 1 <= 8

    cond8 = jnp.zeros((8, D_MODEL), F32).at[0].set(c_ctx).at[1:1 + dec_batch].set(c)
    mod = _adaln(cond8, ada_w[0], ada_b[0])
    tabs = _decay_tables(ret_decay_fwd[0], ret_decay_bwd[0])

    w_in_b = w_in[0].astype(BF16)
    cw = jnp.broadcast_to(conv_dw_w[0][:, None, :], (CONV_WIDTH, 8, D_MODEL))
    cb = jnp.broadcast_to(conv_dw_b, (8, D_MODEL))
    fw = jnp.zeros((8, 2 * D_FF), F32).at[:FFN_CONV_WIDTH].set(ffn_dw_w[0])
    fb = ffn_dw_b
    conv = (cw, cb, conv_ln_w, conv_ln_b)

    xp = x_prompt.reshape(batch * seq, D_MODEL)
    later = (ret_w_out[0], conv_w_out[0], w_mix_out[0], ffn_w_up[0], ffn_w_down[0])
    ro, cu, gr, gc, sf, sb, wro, wco, wmx, wup, wdn = _proj(
        xp, mod, norm1_pre, w_in_b, *conv, latent=False, seq=seq, tm=seq, gn_w=ret_gn_w, tabs=tabs, later=later)

    def tail(x, ro, cu, gr, gc, *, latent, seq_len):
        x1, h2 = _mixer(x, ro, cu, gr, gc, mod, wro, wco, wmx, norm1_post, norm2_pre,
                        seq=seq_len, tm=512, per_seq_mod=latent)
        return _ffn(x1, h2, mod, wup, fw, fb, wdn, norm2_post, seq=seq_len, tm=256, per_seq_mod=latent)

    yp = tail(xp, ro, cu, gr, gc, latent=False, seq_len=seq)

    xs = x_sample.reshape(dec_batch * dec_seq, D_MODEL)
    q, k, v, sg, cu, gr, gc = _proj(xs, mod, norm1_pre, w_in_b, *conv, latent=True, seq=dec_seq, tm=256)
    ro = _ret_lat(q, k, v, sg, ret_gn_w, state_ret_fwd, state_ret_bwd, tabs, dec_batch, dec_seq)
    ys = tail(xs, ro, cu, gr, gc, latent=True, seq_len=dec_seq)
    return (yp.reshape(batch, seq, D_MODEL), ys.reshape(dec_batch, dec_seq, D_MODEL), sf, sb)
```

```python
import functools

import numpy as np
import jax
import jax.numpy as jnp
from jax import lax
from jax.experimental import pallas as pl
from jax.experimental.pallas import tpu as pltpu

F32 = jnp.float32
BF16 = jnp.bfloat16

D_MODEL = 1024
N_HEADS = 4
D_HEAD = 256
D_FF = 2816
CONV_WIDTH = 31
FFN_CONV_WIDTH = 3
GRID_W = 64
ROPE_BASE = 10000.0
EPS = 1e-6
GN_EPS = 1e-5
RET_CHUNK = 256
ADALN_CHUNKS = 3
LAT_HEADS = 2
LAT_UNROLL = 4
HALO = 16
CONV_COLS = 256
PROJ_CONV_ROWS = 64
MIXER_PARTS = 4
FFN_COLS = 256
FFN_AHEAD = 3
VMEM_LIMIT = 56 * 1024 * 1024


def _sigmoid(x):
    return 1.0 / (1.0 + jnp.exp(-x))


def _rms(x, w):
    ms = jnp.mean(x * x, axis=-1, keepdims=True)
    return (x * lax.rsqrt(ms + EPS)) * w


def _params(sem):
    return pltpu.CompilerParams(dimension_semantics=sem, vmem_limit_bytes=VMEM_LIMIT)


def _resident(shape):
    nd = len(shape)
    return pl.BlockSpec(shape, lambda *_: (0,) * nd, pipeline_mode=pl.Buffered(1))


def _adaln_kernel(cond_ref, w_ref, b_ref, o_ref):
    c = cond_ref[...]
    s = (c * _sigmoid(c)).astype(BF16)
    for j in range(ADALN_CHUNKS):
        w = w_ref[:, j * D_MODEL:(j + 1) * D_MODEL].astype(BF16)
        o_ref[j] = jnp.dot(s, w, preferred_element_type=F32) + b_ref[j]


def _adaln(cond8, ada_w, ada_b):
    n = ADALN_CHUNKS
    return pl.pallas_call(
        _adaln_kernel,
        grid=(6 // n,),
        in_specs=[pl.BlockSpec((8, D_MODEL), lambda j: (0, 0)),
                  pl.BlockSpec((D_MODEL, n * D_MODEL), lambda j: (0, j)),
                  pl.BlockSpec((n, 1, D_MODEL), lambda j: (j, 0, 0))],
        out_specs=pl.BlockSpec((n, 8, D_MODEL), lambda j: (j, 0, 0)),
        out_shape=jax.ShapeDtypeStruct((6, 8, D_MODEL), F32),
        compiler_params=_params(("arbitrary",)),
        name="adaln",
    )(cond8, ada_w, ada_b.reshape(6, 1, D_MODEL))


def _decay_kernel(df_ref, db_ref, d_ref, kf_ref, kb_ref, qf_ref, qb_ref, cf_ref, cb_ref):
    def log_sigmoid(x):
        return -(jnp.maximum(-x, 0.0) + jnp.log1p(jnp.exp(-jnp.abs(x))))

    lgf = log_sigmoid(df_ref[...])
    lgb = log_sigmoid(db_ref[...])
    c = RET_CHUNK
    row = lax.broadcasted_iota(jnp.int32, (c, 128), 0).astype(F32)
    col = lax.broadcasted_iota(jnp.int32, (c, 128), 1).astype(F32)
    for h in range(N_HEADS):
        f = lgf[h:h + 1, :]
        b = lgb[h:h + 1, :]
        for half in range(2):
            sl = slice(half * 128, (half + 1) * 128)
            rel = row - (col + 128.0 * half)
            fwd = jnp.exp(jnp.maximum(rel, 0.0) * f)
            bwd = jnp.exp(jnp.maximum(-rel, 0.0) * b)
            d_ref[h, :, sl] = jnp.where(rel > 0, fwd, jnp.where(rel < 0, bwd, 2.0))
            kf_ref[h, :, sl] = jnp.exp((c - 1.0 - row) * f)
            kb_ref[h, :, sl] = jnp.exp(row * b)
            qf_ref[h, :, sl] = jnp.exp((row + 1.0) * f)
            qb_ref[h, :, sl] = jnp.exp((c - row) * b)
            cf_ref[h, :, sl] = jnp.exp(jnp.broadcast_to(c * f, (8, 128)))
            cb_ref[h, :, sl] = jnp.exp(jnp.broadcast_to(c * b, (8, 128)))


def _decay_tables(dec_f, dec_b):
    def lanes(d):
        return jnp.zeros((8, 128), F32).at[:N_HEADS].set(jnp.broadcast_to(d[:, None], (N_HEADS, 128)))

    big = jax.ShapeDtypeStruct((N_HEADS, RET_CHUNK, D_HEAD), F32)
    small = jax.ShapeDtypeStruct((N_HEADS, 8, D_HEAD), F32)
    return pl.pallas_call(
        _decay_kernel,
        out_shape=(big, big, big, big, big, small, small),
        name="decay_tables",
    )(lanes(dec_f), lanes(dec_b))


def _rope_tables(seq):
    rows = seq // GRID_W
    nf = D_HEAD // 4
    r = np.repeat(np.arange(rows), GRID_W).astype(np.float64)
    c = np.tile(np.arange(GRID_W), rows).astype(np.float64)
    inv = ROPE_BASE ** (-np.arange(nf, dtype=np.float64) / nf)
    ar, ac = r[:, None] * inv[None, :], c[:, None] * inv[None, :]
    cos = np.concatenate([np.cos(ar), np.cos(ar), np.cos(ac), np.cos(ac)], axis=1)
    sin = np.concatenate([-np.sin(ar), np.sin(ar), -np.sin(ac), np.sin(ac)], axis=1)
    return jnp.asarray(cos, F32), jnp.asarray(sin, F32)


def _proj_kernel(*refs, latent, tm, tiles_per_seq, n_later=0):
    it = iter(refs)
    x_ref = next(it)
    xp_ref, xn_ref = (next(it), next(it)) if latent else (None, None)
    mod_ref, n1_ref, w_ref, cw_ref, cb_ref, lnw_ref, lnb_ref = (next(it) for _ in range(7))
    if latent:
        cos_ref, sin_ref = next(it), next(it)
        q_ref, k_ref, v_ref, sg_ref, cu_ref, gr_ref, gc_ref = (next(it) for _ in range(7))
    else:
        gnw_ref, d_ref, kf_ref, kb_ref = (next(it) for _ in range(4))
        later_f32 = [next(it) for _ in range(n_later)]
        ro_ref, cu_ref, gr_ref, gc_ref, sf_ref, sb_ref = (next(it) for _ in range(6))
        for src, dst in zip(later_f32, [next(it) for _ in range(n_later)]):
            dst[...] = src[...].astype(BF16)
    pad_ref, uc_ref = it

    i = pl.program_id(0)
    r = (1 + i // tiles_per_seq) if latent else 0
    sh1 = mod_ref[0, pl.ds(r, 1), :]
    sc1 = mod_ref[1, pl.ds(r, 1), :]

    def norm(x):
        return (_rms(x, n1_ref[...]) * (1.0 + sc1) + sh1).astype(BF16)

    hb = norm(x_ref[...])
    n_pad = tm + 2 * HALO
    if latent:
        he = jnp.concatenate([norm(xp_ref[...]), hb, norm(xn_ref[...])], axis=0)
        row = lax.broadcasted_iota(jnp.int32, (n_pad, 1), 0)
        first = (i % tiles_per_seq) == 0
        last = (i % tiles_per_seq) == tiles_per_seq - 1
        outside = (first & (row < HALO)) | (last & (row >= HALO + tm))
    else:
        he = hb

    def proj(lhs, col, width=D_MODEL):
        return jnp.dot(lhs, w_ref[:, col:col + width], preferred_element_type=F32)

    off = HALO - CONV_WIDTH // 2
    n_sub = PROJ_CONV_ROWS // 8

    def conv_chunk(c):
        cs = slice(c * CONV_COLS, (c + 1) * CONV_COLS)
        u = (proj(he, 4 * D_MODEL + c * CONV_COLS, CONV_COLS)
             * _sigmoid(proj(he, 5 * D_MODEL + c * CONV_COLS, CONV_COLS)))
        if latent:
            pad_ref[c, 0] = jnp.where(outside, 0.0, u)
        else:
            pad_ref[c, 0, 0:HALO, :] = jnp.zeros((HALO, CONV_COLS), F32)
            pad_ref[c, 0, HALO:HALO + tm, :] = u
            pad_ref[c, 0, HALO + tm:, :] = jnp.zeros((HALO, CONV_COLS), F32)
        padded = pad_ref[c, 0]
        for s in range(1, 8):
            pad_ref[c, s] = pltpu.roll(padded, n_pad - s, axis=0)
        for rb in range(tm // PROJ_CONV_ROWS):
            base = rb * PROJ_CONV_ROWS
            acc = [cb_ref[:, cs]] * n_sub
            for j in range(CONV_WIDTH):
                s = j + off
                w = cw_ref[j, :, cs]
                for a in range(n_sub):
                    lo = base + (s // 8 + a) * 8
                    acc[a] = acc[a] + pad_ref[c, s % 8, lo:lo + 8, :] * w
            for a in range(n_sub):
                uc_ref[base + a * 8:base + (a + 1) * 8, cs] = acc[a]

    def conv_norm():
        uc = uc_ref[...]
        mu = jnp.mean(uc, axis=-1, keepdims=True)
        d = uc - mu
        var = jnp.mean(d * d, axis=-1, keepdims=True)
        ln = (d * lax.rsqrt(var + GN_EPS)) * lnw_ref[...] + lnb_ref[...]
        cu_ref[...] = (ln * _sigmoid(ln)).astype(BF16)

    def rotate(a):
        slabs = []
        for j in range(D_MODEL // 128):
            slab = a[:, j * 128:(j + 1) * 128]
            tsl = slice((j % 2) * 128, (j % 2 + 1) * 128)
            slabs.append(slab * cos_ref[:, tsl] + pltpu.roll(slab, 64, axis=1) * sin_ref[:, tsl])
        return jnp.concatenate(slabs, axis=1)

    def head(h):
        sl = slice(h * D_HEAD, (h + 1) * D_HEAD)
        qh, kh, vh = q[:, sl], k[:, sl], v[:, sl]
        p = (_dot_nt(qh, kh) * d_ref[h]).astype(BF16)
        o = jnp.dot(p, vh, preferred_element_type=F32)
        ro_ref[:, sl] = _gn_gate(o, gnw_ref[:, sl], sg[:, sl])
        kf = kh.astype(F32)
        sf_ref[0, 0, h] = _dot_tn((kf * kf_ref[h]).astype(BF16), vh)
        sb_ref[0, 0, h] = _dot_tn((kf * kb_ref[h]).astype(BF16), vh)

    conv_chunk(0)
    conv_chunk(1)
    a_q = proj(hb, 0)
    q = ((rotate(a_q) if latent else a_q) * D_HEAD ** -0.5).astype(BF16)
    a_k = proj(hb, D_MODEL)
    k = (rotate(a_k) if latent else a_k).astype(BF16)
    a_v = proj(hb, 2 * D_MODEL)
    v = a_v.astype(BF16)
    a_g = proj(hb, 3 * D_MODEL)
    sg = a_g * _sigmoid(a_g)
    if latent:
        q_ref[...] = q
        k_ref[...] = k
        v_ref[...] = v
        sg_ref[...] = sg.astype(BF16)
    else:
        for h in range(N_HEADS):
            head(h)
    conv_chunk(2)
    a_gr = proj(hb, 6 * D_MODEL)
    gr_ref[...] = _sigmoid(a_gr).astype(BF16)
    conv_chunk(3)
    a_gc = proj(hb, 7 * D_MODEL)
    gc_ref[...] = _sigmoid(a_gc).astype(BF16)
    conv_norm()


def _proj(x, mod, n1_pre, w_in, cw, cb, lnw, lnb, *, latent, seq, tm, gn_w=None, tabs=None, later=()):
    t = x.shape[0]
    tiles_per_seq = seq // tm
    hb = tm // HALO
    n_hb = t // HALO
    tile = pl.BlockSpec((tm, D_MODEL), lambda i: (i, 0))
    vec = _resident((1, D_MODEL))
    out = jax.ShapeDtypeStruct((t, D_MODEL), BF16)
    in_specs, args = [tile], [x]
    if latent:
        in_specs += [pl.BlockSpec((HALO, D_MODEL), lambda i: (jnp.maximum(i * hb - 1, 0), 0)),
                     pl.BlockSpec((HALO, D_MODEL), lambda i: (jnp.minimum((i + 1) * hb, n_hb - 1), 0))]
        args += [x, x]
    in_specs += [_resident(mod.shape), vec, _resident(w_in.shape), _resident(cw.shape), _resident(cb.shape), vec, vec]
    args += [mod, n1_pre, w_in, cw, cb, lnw, lnb]
    if latent:
        cos, sin = _rope_tables(seq)
        tab = pl.BlockSpec((tm, D_HEAD), lambda i: (i % tiles_per_seq, 0))
        in_specs += [tab, tab]
        args += [cos, sin]
        out_specs, out_shape = [tile] * 7, [out] * 7
    else:
        assert tiles_per_seq == 1 and tm == RET_CHUNK
        d, kf, kb = tabs[0], tabs[1], tabs[2]
        in_specs += [vec, _resident(d.shape), _resident(kf.shape), _resident(kb.shape)]
        args += [gn_w, d, kf, kb]
        st = pl.BlockSpec((1, 1, N_HEADS, D_HEAD, D_HEAD), lambda i: (i, 0, 0, 0, 0))
        st_shape = jax.ShapeDtypeStruct((t // tm, 1, N_HEADS, D_HEAD, D_HEAD), F32)
        out_specs, out_shape = [tile] * 4 + [st, st], [out] * 4 + [st_shape, st_shape]
        steps = t // tm
        for w in later:
            rows, cols = w.shape
            n = max(n for n in range(1, steps + 1) if rows % n == 0 and (rows // n) % HALO == 0)
            blk = pl.BlockSpec((rows // n, cols), lambda i, n=n: (jnp.minimum(i, n - 1), 0))
            in_specs.append(blk)
            args.append(w)
            out_specs.append(blk)
            out_shape.append(jax.ShapeDtypeStruct(w.shape, BF16))
    return pl.pallas_call(
        functools.partial(_proj_kernel, latent=latent, tm=tm, tiles_per_seq=tiles_per_seq, n_later=len(later)),
        grid=(t // tm,),
        in_specs=in_specs,
        out_specs=out_specs,
        out_shape=out_shape,
        scratch_shapes=[pltpu.VMEM((D_MODEL // CONV_COLS, 8, tm + 2 * HALO, CONV_COLS), F32),
                        pltpu.VMEM((tm, D_MODEL), F32)],
        compiler_params=_params(("arbitrary",)),
        name="proj_lat" if latent else "proj_ctx",
    )(*args)


def _dot_nt(a, b):
    return lax.dot_general(a, b, (((1,), (1,)), ((), ())), preferred_element_type=F32)


def _dot_tn(a, b):
    return lax.dot_general(a, b, (((0,), (0,)), ((), ())), preferred_element_type=F32)


def _gn_gate(o, gnw, sg):
    mu = jnp.mean(o, axis=-1, keepdims=True)
    d = o - mu
    var = jnp.mean(d * d, axis=-1, keepdims=True)
    y = (d * lax.rsqrt(var + GN_EPS)) * gnw
    return (sg.astype(F32) * y).astype(BF16)


def _ret_lat_kernel(q_ref, k_ref, v_ref, sg_ref, gnw_ref, sf0_ref, sb0_ref,
                    d_ref, kf_ref, kb_ref, qf_ref, qb_ref, cf_ref, cb_ref,
                    ro_ref, sbs_ref, *, n_chunks):
    c = RET_CHUNK

    def rows(j):
        return pl.ds(pl.multiple_of(j * c, c), c)

    def cols(h):
        return slice(h * D_HEAD, (h + 1) * D_HEAD)

    def bwd(jj, states):
        j = n_chunks - 1 - jj
        new = []
        for h, s in enumerate(states):
            sbs_ref[h, j] = s
            kb = (k_ref[rows(j), cols(h)].astype(F32) * kb_ref[h]).astype(BF16)
            new.append(s * cb_ref[h, 0:1, :] + _dot_tn(kb, v_ref[rows(j), cols(h)]))
        return tuple(new)

    lax.fori_loop(0, n_chunks, bwd, tuple(sb0_ref[0, 0, h] for h in range(LAT_HEADS)), unroll=LAT_UNROLL)

    def fwd(j, states):
        new = []
        for h, s in enumerate(states):
            q, k, v = q_ref[rows(j), cols(h)], k_ref[rows(j), cols(h)], v_ref[rows(j), cols(h)]
            qf32, kf32 = q.astype(F32), k.astype(F32)
            p = (_dot_nt(q, k) * d_ref[h]).astype(BF16)
            o = jnp.dot(p, v, preferred_element_type=F32)
            o = o + jnp.dot((qf32 * qf_ref[h]).astype(BF16), s.astype(BF16), preferred_element_type=F32)
            o = o + jnp.dot((qf32 * qb_ref[h]).astype(BF16), sbs_ref[h, j].astype(BF16),
                            preferred_element_type=F32)
            ro_ref[rows(j), cols(h)] = _gn_gate(o, gnw_ref[:, cols(h)], sg_ref[rows(j), cols(h)])
            new.append(s * cf_ref[h, 0:1, :] + _dot_tn((kf32 * kf_ref[h]).astype(BF16), v))
        return tuple(new)

    lax.fori_loop(0, n_chunks, fwd, tuple(sf0_ref[0, 0, h] for h in range(LAT_HEADS)), unroll=LAT_UNROLL)


def _ret_lat(q, k, v, sg, gn_w, sf0, sb0, tabs, batch, seq):
    n_chunks = seq // RET_CHUNK
    w = LAT_HEADS * D_HEAD
    tile = pl.BlockSpec((seq, w), lambda s, h: (s, h))
    st = pl.BlockSpec((1, 1, LAT_HEADS, D_HEAD, D_HEAD), lambda s, h: (s, 0, h, 0, 0))
    big = pl.BlockSpec((LAT_HEADS, RET_CHUNK, D_HEAD), lambda s, h: (h, 0, 0))
    small = pl.BlockSpec((LAT_HEADS, 8, D_HEAD), lambda s, h: (h, 0, 0))
    return pl.pallas_call(
        functools.partial(_ret_lat_kernel, n_chunks=n_chunks),
        grid=(batch, N_HEADS // LAT_HEADS),
        in_specs=[tile, tile, tile, tile, pl.BlockSpec((1, w), lambda s, h: (0, h)),
                  st, st, big, big, big, big, big, small, small],
        out_specs=tile,
        out_shape=jax.ShapeDtypeStruct((batch * seq, D_MODEL), BF16),
        scratch_shapes=[pltpu.VMEM((LAT_HEADS, n_chunks, D_HEAD, D_HEAD), F32)],
        compiler_params=_params(("arbitrary", "arbitrary")),
        name="ret_lat",
    )(q, k, v, sg, gn_w, sf0, sb0, *tabs)


def _mixer_kernel(x_ref, ro_ref, cu_ref, gr_ref, gc_ref, mod_ref, wro_ref, wco_ref, wmx_ref, n1p_ref, n2_ref,
                  x1_ref, h2_ref, *, tiles_per_seq, per_seq_mod):
    r = (1 + pl.program_id(0) // tiles_per_seq) if per_seq_mod else 0
    g1 = mod_ref[2, pl.ds(r, 1), :]
    sh2 = mod_ref[3, pl.ds(r, 1), :]
    sc2 = mod_ref[4, pl.ds(r, 1), :]
    part = x_ref.shape[0] // MIXER_PARTS
    rows = [slice(p * part, (p + 1) * part) for p in range(MIXER_PARTS)]
    outs = [(jnp.dot(cu_ref[rs, :], wco_ref[...], preferred_element_type=F32),
             jnp.dot(ro_ref[rs, :], wro_ref[...], preferred_element_type=F32)) for rs in rows]
    ms = []
    for rs, (conv_out, ret_out) in zip(rows, outs):
        mixed = gr_ref[rs, :].astype(F32) * ret_out + gc_ref[rs, :].astype(F32) * conv_out
        ms.append(jnp.dot(mixed.astype(BF16), wmx_ref[...], preferred_element_type=F32))
    for rs, m in zip(rows, ms):
        x1 = x_ref[rs, :] + g1 * _rms(m, n1p_ref[...])
        x1_ref[rs, :] = x1
        h2_ref[rs, :] = (_rms(x1, n2_ref[...]) * (1.0 + sc2) + sh2).astype(BF16)


def _mixer(x, ro, cu, gr, gc, mod, wro, wco, wmx, n1p, n2, *, seq, tm, per_seq_mod):
    t = x.shape[0]
    tile = pl.BlockSpec((tm, D_MODEL), lambda i: (i, 0))
    vec = _resident((1, D_MODEL))
    mat = _resident((D_MODEL, D_MODEL))
    return pl.pallas_call(
        functools.partial(_mixer_kernel, tiles_per_seq=max(seq // tm, 1), per_seq_mod=per_seq_mod),
        grid=(t // tm,),
        in_specs=[tile, tile, tile, tile, tile, _resident(mod.shape), mat, mat, mat, vec, vec],
        out_specs=[tile, tile],
        out_shape=[jax.ShapeDtypeStruct((t, D_MODEL), F32), jax.ShapeDtypeStruct((t, D_MODEL), BF16)],
        compiler_params=_params(("arbitrary",)),
        name="mixer_lat" if per_seq_mod else "mixer_ctx",
    )(x, ro, cu, gr, gc, mod, wro, wco, wmx, n1p, n2)


def _ffn_kernel(x1_ref, h2_ref, hp_ref, hn_ref, mod_ref, wup_ref, fw_ref, fb_ref, wdn_ref, n2p_ref,
                y_ref, *, tm, tiles_per_seq, per_seq_mod):
    i = pl.program_id(0)
    r = (1 + i // tiles_per_seq) if per_seq_mod else 0
    g2 = mod_ref[5, pl.ds(r, 1), :]
    first = (i % tiles_per_seq) == 0
    last = (i % tiles_per_seq) == tiles_per_seq - 1

    whole_seq = tiles_per_seq == 1
    if whole_seq:
        n_ext = tm
        he = h2_ref[...]
        sub = lax.broadcasted_iota(jnp.int32, (8, 1), 0)
    else:
        n_ext = tm + HALO
        row = lax.broadcasted_iota(jnp.int32, (HALO, 1), 0)
        zero = jnp.zeros((), BF16)
        halo = jnp.where(row < HALO // 2, jnp.where(last, zero, hn_ref[...]), jnp.where(first, zero, hp_ref[...]))
        he = jnp.concatenate([h2_ref[...], halo], axis=0)

    inner = slice(0, tm)
    n_chunks = D_FF // FFN_COLS

    def cols(cc, half):
        return slice(half * D_FF + cc * FFN_COLS, half * D_FF + (cc + 1) * FFN_COLS)

    def up_proj(cc):
        return [jnp.dot(he, wup_ref[:, cols(cc, half)], preferred_element_type=F32) for half in range(2)]

    def conv3(up, cs):
        before = pltpu.roll(up, 1, axis=0)[inner]
        after = pltpu.roll(up, n_ext - 1, axis=0)[inner]
        if whole_seq:
            before = jnp.concatenate([jnp.where(sub == 0, 0.0, before[0:8]), before[8:]], axis=0)
            after = jnp.concatenate([after[:tm - 8], jnp.where(sub == 7, 0.0, after[tm - 8:])], axis=0)
        return (fb_ref[:, cs] + before * fw_ref[0:1, cs] + up[inner] * fw_ref[1:2, cs]
                + after * fw_ref[2:3, cs])

    f = None
    ups = [up_proj(cc) for cc in range(FFN_AHEAD)]
    for cc in range(n_chunks):
        up = ups[cc]
        if cc + FFN_AHEAD < n_chunks:
            ups.append(up_proj(cc + FFN_AHEAD))
        a, gl = conv3(up[0], cols(cc, 0)), conv3(up[1], cols(cc, 1))
        act = ((a * _sigmoid(a)) * gl).astype(BF16)
        part = jnp.dot(act, wdn_ref[cc * FFN_COLS:(cc + 1) * FFN_COLS, :], preferred_element_type=F32)
        f = part if f is None else f + part
    y_ref[...] = x1_ref[...] + g2 * _rms(f, n2p_ref[...])


def _ffn(x1, h2, mod, wup, fw, fb, wdn, n2p, *, seq, tm, per_seq_mod):
    t = x1.shape[0]
    tiles_per_seq = seq // tm
    hb = tm // HALO
    n_hb = t // HALO
    tile = pl.BlockSpec((tm, D_MODEL), lambda i: (i, 0))
    prev = pl.BlockSpec((HALO, D_MODEL), lambda i: (jnp.maximum(i * hb - 1, 0), 0))
    nxt = pl.BlockSpec((HALO, D_MODEL), lambda i: (jnp.minimum((i + 1) * hb, n_hb - 1), 0))
    return pl.pallas_call(
        functools.partial(_ffn_kernel, tm=tm, tiles_per_seq=tiles_per_seq, per_seq_mod=per_seq_mod),
        grid=(t // tm,),
        in_specs=[tile, tile, prev, nxt, _resident(mod.shape), _resident(wup.shape),
                  _resident(fw.shape), _resident(fb.shape), _resident(wdn.shape), _resident((1, D_MODEL))],
        out_specs=tile,
        out_shape=jax.ShapeDtypeStruct((t, D_MODEL), F32),
        compiler_params=_params(("arbitrary",)),
        name="ffn_lat" if per_seq_mod else "ffn_ctx",
    )(x1, h2, h2, h2, mod, wup, fw, fb, wdn, n2p)


def kernel(x_prompt, x_sample, state_ret_fwd, state_ret_bwd, c, c_ctx, norm1_pre, norm1_post, norm2_pre, norm2_post, ada_w, ada_b, w_in, ret_decay_fwd, ret_decay_bwd, ret_gn_w, ret_w_out, conv_dw_w, conv_dw_b, conv_ln_w, conv_ln_b, conv_w_out, w_mix_out, ffn_w_up, ffn_dw_w, ffn_dw_b, ffn_w_down):
    batch, seq, _ = x_prompt.shape
    dec_batch, dec_seq, _ = x_sample.shape
    assert norm1_pre.shape[0] == 1 and dec_batch + 1 <= 8

    cond8 = jnp.zeros((8, D_MODEL), F32).at[0].set(c_ctx).at[1:1 + dec_batch].set(c)
    mod = _adaln(cond8, ada_w[0], ada_b[0])
    tabs = _decay_tables(ret_decay_fwd[0], ret_decay_bwd[0])

    w_in_b = w_in[0].astype(BF16)
    cw = jnp.broadcast_to(conv_dw_w[0][:, None, :], (CONV_WIDTH, 8, D_MODEL))
    cb = jnp.broadcast_to(conv_dw_b, (8, D_MODEL))
    fw = jnp.zeros((8, 2 * D_FF), F32).at[:FFN_CONV_WIDTH].set(ffn_dw_w[0])
    fb = ffn_dw_b
    conv = (cw, cb, conv_ln_w, conv_ln_b)

    xp = x_prompt.reshape(batch * seq, D_MODEL)
    later = (ret_w_out[0], conv_w_out[0], w_mix_out[0], ffn_w_up[0], ffn_w_down[0])
    ro, cu, gr, gc, sf, sb, wro, wco, wmx, wup, wdn = _proj(
        xp, mod, norm1_pre, w_in_b, *conv, latent=False, seq=seq, tm=seq, gn_w=ret_gn_w, tabs=tabs, later=later)

    def tail(x, ro, cu, gr, gc, *, latent, seq_len):
        x1, h2 = _mixer(x, ro, cu, gr, gc, mod, wro, wco, wmx, norm1_post, norm2_pre,
                        seq=seq_len, tm=512, per_seq_mod=latent)
        return _ffn(x1, h2, mod, wup, fw, fb, wdn, norm2_post, seq=seq_len, tm=256, per_seq_mod=latent)

    yp = tail(xp, ro, cu, gr, gc, latent=False, seq_len=seq)

    xs = x_sample.reshape(dec_batch * dec_seq, D_MODEL)
    q, k, v, sg, cu, gr, gc = _proj(xs, mod, norm1_pre, w_in_b, *conv, latent=True, seq=dec_seq, tm=256)
    ro = _ret_lat(q, k, v, sg, ret_gn_w, state_ret_fwd, state_ret_bwd, tabs, dec_batch, dec_seq)
    ys = tail(xs, ro, cu, gr, gc, latent=True, seq_len=dec_seq)
    return (yp.reshape(batch, seq, D_MODEL), ys.reshape(dec_batch, dec_seq, D_MODEL), sf, sb)
```

```python
import functools

import numpy as np
import jax
import jax.numpy as jnp
from jax import lax
from jax.experimental import pallas as pl
from jax.experimental.pallas import tpu as pltpu

F32 = jnp.float32
BF16 = jnp.bfloat16

D_MODEL = 1024
N_HEADS = 4
D_HEAD = 256
D_FF = 2816
CONV_WIDTH = 31
FFN_CONV_WIDTH = 3
GRID_W = 64
ROPE_BASE = 10000.0
EPS = 1e-6
GN_EPS = 1e-5
RET_CHUNK = 256
ADALN_CHUNKS = 3
LAT_HEADS = 2
LAT_UNROLL = 4
HALO = 16
CONV_COLS = 256
PROJ_CONV_ROWS = 64
MIXER_PARTS = 4
FFN_COLS = 256
FFN_AHEAD = 3
VMEM_LIMIT = 56 * 1024 * 1024


def _sigmoid(x):
    return 1.0 / (1.0 + jnp.exp(-x))


def _rms(x, w):
    ms = jnp.mean(x * x, axis=-1, keepdims=True)
    return (x * lax.rsqrt(ms + EPS)) * w


def _params(sem):
    return pltpu.CompilerParams(dimension_semantics=sem, vmem_limit_bytes=VMEM_LIMIT)


def _resident(shape):
    nd = len(shape)
    return pl.BlockSpec(shape, lambda *_: (0,) * nd, pipeline_mode=pl.Buffered(1))


def _adaln_kernel(cond_ref, w_ref, b_ref, o_ref):
    c = cond_ref[...]
    s = (c * _sigmoid(c)).astype(BF16)
    for j in range(ADALN_CHUNKS):
        w = w_ref[:, j * D_MODEL:(j + 1) * D_MODEL].astype(BF16)
        o_ref[j] = jnp.dot(s, w, preferred_element_type=F32) + b_ref[j]


def _adaln(cond8, ada_w, ada_b):
    n = ADALN_CHUNKS
    return pl.pallas_call(
        _adaln_kernel,
        grid=(6 // n,),
        in_specs=[pl.BlockSpec((8, D_MODEL), lambda j: (0, 0)),
                  pl.BlockSpec((D_MODEL, n * D_MODEL), lambda j: (0, j)),
                  pl.BlockSpec((n, 1, D_MODEL), lambda j: (j, 0, 0))],
        out_specs=pl.BlockSpec((n, 8, D_MODEL), lambda j: (j, 0, 0)),
        out_shape=jax.ShapeDtypeStruct((6, 8, D_MODEL), F32),
        compiler_params=_params(("arbitrary",)),
        name="adaln",
    )(cond8, ada_w, ada_b.reshape(6, 1, D_MODEL))


def _decay_kernel(df_ref, db_ref, d_ref, kf_ref, kb_ref, qf_ref, qb_ref, cf_ref, cb_ref):
    def log_sigmoid(x):
        return -(jnp.maximum(-x, 0.0) + jnp.log1p(jnp.exp(-jnp.abs(x))))

    lgf = log_sigmoid(df_ref[...])
    lgb = log_sigmoid(db_ref[...])
    c = RET_CHUNK
    row = lax.broadcasted_iota(jnp.int32, (c, 128), 0).astype(F32)
    col = lax.broadcasted_iota(jnp.int32, (c, 128), 1).astype(F32)
    for h in range(N_HEADS):
        f = lgf[h:h + 1, :]
        b = lgb[h:h + 1, :]
        for half in range(2):
            sl = slice(half * 128, (half + 1) * 128)
            rel = row - (col + 128.0 * half)
            fwd = jnp.exp(jnp.maximum(rel, 0.0) * f)
            bwd = jnp.exp(jnp.maximum(-rel, 0.0) * b)
            d_ref[h, :, sl] = jnp.where(rel > 0, fwd, jnp.where(rel < 0, bwd, 2.0))
            kf_ref[h, :, sl] = jnp.exp((c - 1.0 - row) * f)
            kb_ref[h, :, sl] = jnp.exp(row * b)
            qf_ref[h, :, sl] = jnp.exp((row + 1.0) * f)
            qb_ref[h, :, sl] = jnp.exp((c - row) * b)
            cf_ref[h, :, sl] = jnp.exp(jnp.broadcast_to(c * f, (8, 128)))
            cb_ref[h, :, sl] = jnp.exp(jnp.broadcast_to(c * b, (8, 128)))


def _decay_tables(dec_f, dec_b):
    def lanes(d):
        return jnp.zeros((8, 128), F32).at[:N_HEADS].set(jnp.broadcast_to(d[:, None], (N_HEADS, 128)))

    big = jax.ShapeDtypeStruct((N_HEADS, RET_CHUNK, D_HEAD), F32)
    small = jax.ShapeDtypeStruct((N_HEADS, 8, D_HEAD), F32)
    return pl.pallas_call(
        _decay_kernel,
        out_shape=(big, big, big, big, big, small, small),
        name="decay_tables",
    )(lanes(dec_f), lanes(dec_b))


def _rope_tables(seq):
    rows = seq // GRID_W
    nf = D_HEAD // 4
    r = np.repeat(np.arange(rows), GRID_W).astype(np.float64)
    c = np.tile(np.arange(GRID_W), rows).astype(np.float64)
    inv = ROPE_BASE ** (-np.arange(nf, dtype=np.float64) / nf)
    ar, ac = r[:, None] * inv[None, :], c[:, None] * inv[None, :]
    cos = np.concatenate([np.cos(ar), np.cos(ar), np.cos(ac), np.cos(ac)], axis=1)
    sin = np.concatenate([-np.sin(ar), np.sin(ar), -np.sin(ac), np.sin(ac)], axis=1)
    return jnp.asarray(cos, F32), jnp.asarray(sin, F32)


def _proj_kernel(*refs, latent, tm, tiles_per_seq, n_later=0):
    it = iter(refs)
    x_ref = next(it)
    xp_ref, xn_ref = (next(it), next(it)) if latent else (None, None)
    mod_ref, n1_ref, w_ref, cw_ref, cb_ref, lnw_ref, lnb_ref = (next(it) for _ in range(7))
    if latent:
        cos_ref, sin_ref = next(it), next(it)
        q_ref, k_ref, v_ref, sg_ref, cu_ref, gr_ref, gc_ref = (next(it) for _ in range(7))
    else:
        gnw_ref, d_ref, kf_ref, kb_ref = (next(it) for _ in range(4))
        later_f32 = [next(it) for _ in range(n_later)]
        ro_ref, cu_ref, gr_ref, gc_ref, sf_ref, sb_ref = (next(it) for _ in range(6))
        for src, dst in zip(later_f32, [next(it) for _ in range(n_later)]):
            dst[...] = src[...].astype(BF16)
    pad_ref, uc_ref = it

    i = pl.program_id(0)
    r = (1 + i // tiles_per_seq) if latent else 0
    sh1 = mod_ref[0, pl.ds(r, 1), :]
    sc1 = mod_ref[1, pl.ds(r, 1), :]

    def norm(x):
        return (_rms(x, n1_ref[...]) * (1.0 + sc1) + sh1).astype(BF16)

    hb = norm(x_ref[...])
    n_pad = tm + 2 * HALO
    if latent:
        he = jnp.concatenate([norm(xp_ref[...]), hb, norm(xn_ref[...])], axis=0)
        row = lax.broadcasted_iota(jnp.int32, (n_pad, 1), 0)
        first = (i % tiles_per_seq) == 0
        last = (i % tiles_per_seq) == tiles_per_seq - 1
        outside = (first & (row < HALO)) | (last & (row >= HALO + tm))
    else:
        he = hb

    def proj(lhs, col, width=D_MODEL):
        return jnp.dot(lhs, w_ref[:, col:col + width], preferred_element_type=F32)

    off = HALO - CONV_WIDTH // 2
    n_sub = PROJ_CONV_ROWS // 8

    def conv_chunk(c):
        cs = slice(c * CONV_COLS, (c + 1) * CONV_COLS)
        u = (proj(he, 4 * D_MODEL + c * CONV_COLS, CONV_COLS)
             * _sigmoid(proj(he, 5 * D_MODEL + c * CONV_COLS, CONV_COLS)))
        if latent:
            pad_ref[c, 0] = jnp.where(outside, 0.0, u)
        else:
            pad_ref[c, 0, 0:HALO, :] = jnp.zeros((HALO, CONV_COLS), F32)
            pad_ref[c, 0, HALO:HALO + tm, :] = u
            pad_ref[c, 0, HALO + tm:, :] = jnp.zeros((HALO, CONV_COLS), F32)
        padded = pad_ref[c, 0]
        for s in range(1, 8):
            pad_ref[c, s] = pltpu.roll(padded, n_pad - s, axis=0)
        for rb in range(tm // PROJ_CONV_ROWS):
            base = rb * PROJ_CONV_ROWS
            acc = [cb_ref[:, cs]] * n_sub
            for j in range(CONV_WIDTH):
                s = j + off
                w = cw_ref[j, :, cs]
                for a in range(n_sub):
                    lo = base + (s // 8 + a) * 8
                    acc[a] = acc[a] + pad_ref[c, s % 8, lo:lo + 8, :] * w
            for a in range(n_sub):
                uc_ref[base + a * 8:base + (a + 1) * 8, cs] = acc[a]

    def conv_norm():
        uc = uc_ref[...]
        mu = jnp.mean(uc, axis=-1, keepdims=True)
        d = uc - mu
        var = jnp.mean(d * d, axis=-1, keepdims=True)
        ln = (d * lax.rsqrt(var + GN_EPS)) * lnw_ref[...] + lnb_ref[...]
        cu_ref[...] = (ln * _sigmoid(ln)).astype(BF16)

    def rotate(a):
        slabs = []
        for j in range(D_MODEL // 128):
            slab = a[:, j * 128:(j + 1) * 128]
            tsl = slice((j % 2) * 128, (j % 2 + 1) * 128)
            slabs.append(slab * cos_ref[:, tsl] + pltpu.roll(slab, 64, axis=1) * sin_ref[:, tsl])
        return jnp.concatenate(slabs, axis=1)

    def head(h):
        sl = slice(h * D_HEAD, (h + 1) * D_HEAD)
        qh, kh, vh = q[:, sl], k[:, sl], v[:, sl]
        p = (_dot_nt(qh, kh) * d_ref[h]).astype(BF16)
        o = jnp.dot(p, vh, preferred_element_type=F32)
        ro_ref[:, sl] = _gn_gate(o, gnw_ref[:, sl], sg[:, sl])
        kf = kh.astype(F32)
        sf_ref[0, 0, h] = _dot_tn((kf * kf_ref[h]).astype(BF16), vh)
        sb_ref[0, 0, h] = _dot_tn((kf * kb_ref[h]).astype(BF16), vh)

    def gate_c():
        a_gc = proj(hb, 7 * D_MODEL)
        gc_ref[...] = _sigmoid(a_gc).astype(BF16)

    conv_chunk(0)
    conv_chunk(1)
    a_q = proj(hb, 0)
    q = ((rotate(a_q) if latent else a_q) * D_HEAD ** -0.5).astype(BF16)
    conv_chunk(2)
    a_k = proj(hb, D_MODEL)
    k = (rotate(a_k) if latent else a_k).astype(BF16)
    a_v = proj(hb, 2 * D_MODEL)
    v = a_v.astype(BF16)
    a_g = proj(hb, 3 * D_MODEL)
    sg = a_g * _sigmoid(a_g)
    if latent:
        q_ref[...] = q
        k_ref[...] = k
        v_ref[...] = v
        sg_ref[...] = sg.astype(BF16)
    else:
        for h in range(N_HEADS):
            head(h)
    a_gr = proj(hb, 6 * D_MODEL)
    gr_ref[...] = _sigmoid(a_gr).astype(BF16)
    if latent:
        gate_c()
        conv_chunk(3)
    else:
        conv_chunk(3)
        gate_c()
    conv_norm()


def _proj(x, mod, n1_pre, w_in, cw, cb, lnw, lnb, *, latent, seq, tm, gn_w=None, tabs=None, later=()):
    t = x.shape[0]
    tiles_per_seq = seq // tm
    hb = tm // HALO
    n_hb = t // HALO
    tile = pl.BlockSpec((tm, D_MODEL), lambda i: (i, 0))
    vec = _resident((1, D_MODEL))
    out = jax.ShapeDtypeStruct((t, D_MODEL), BF16)
    in_specs, args = [tile], [x]
    if latent:
        in_specs += [pl.BlockSpec((HALO, D_MODEL), lambda i: (jnp.maximum(i * hb - 1, 0), 0)),
                     pl.BlockSpec((HALO, D_MODEL), lambda i: (jnp.minimum((i + 1) * hb, n_hb - 1), 0))]
        args += [x, x]
    in_specs += [_resident(mod.shape), vec, _resident(w_in.shape), _resident(cw.shape), _resident(cb.shape), vec, vec]
    args += [mod, n1_pre, w_in, cw, cb, lnw, lnb]
    if latent:
        cos, sin = _rope_tables(seq)
        tab = pl.BlockSpec((tm, D_HEAD), lambda i: (i % tiles_per_seq, 0))
        in_specs += [tab, tab]
        args += [cos, sin]
        out_specs, out_shape = [tile] * 7, [out] * 7
    else:
        assert tiles_per_seq == 1 and tm == RET_CHUNK
        d, kf, kb = tabs[0], tabs[1], tabs[2]
        in_specs += [vec, _resident(d.shape), _resident(kf.shape), _resident(kb.shape)]
        args += [gn_w, d, kf, kb]
        st = pl.BlockSpec((1, 1, N_HEADS, D_HEAD, D_HEAD), lambda i: (i, 0, 0, 0, 0))
        st_shape = jax.ShapeDtypeStruct((t // tm, 1, N_HEADS, D_HEAD, D_HEAD), F32)
        out_specs, out_shape = [tile] * 4 + [st, st], [out] * 4 + [st_shape, st_shape]
        steps = t // tm
        for w in later:
            rows, cols = w.shape
            n = max(n for n in range(1, steps + 1) if rows % n == 0 and (rows // n) % HALO == 0)
            blk = pl.BlockSpec((rows // n, cols), lambda i, n=n: (jnp.minimum(i, n - 1), 0))
            in_specs.append(blk)
            args.append(w)
            out_specs.append(blk)
            out_shape.append(jax.ShapeDtypeStruct(w.shape, BF16))
    return pl.pallas_call(
        functools.partial(_proj_kernel, latent=latent, tm=tm, tiles_per_seq=tiles_per_seq, n_later=len(later)),
        grid=(t // tm,),
        in_specs=in_specs,
        out_specs=out_specs,
        out_shape=out_shape,
        scratch_shapes=[pltpu.VMEM((D_MODEL // CONV_COLS, 8, tm + 2 * HALO, CONV_COLS), F32),
                        pltpu.VMEM((tm, D_MODEL), F32)],
        compiler_params=_params(("arbitrary",)),
        name="proj_lat" if latent else "proj_ctx",
    )(*args)


def _dot_nt(a, b):
    return lax.dot_general(a, b, (((1,), (1,)), ((), ())), preferred_element_type=F32)


def _dot_tn(a, b):
    return lax.dot_general(a, b, (((0,), (0,)), ((), ())), preferred_element_type=F32)


def _gn_gate(o, gnw, sg):
    mu = jnp.mean(o, axis=-1, keepdims=True)
    d = o - mu
    var = jnp.mean(d * d, axis=-1, keepdims=True)
    y = (d * lax.rsqrt(var + GN_EPS)) * gnw
    return (sg.astype(F32) * y).astype(BF16)


def _ret_lat_kernel(q_ref, k_ref, v_ref, sg_ref, gnw_ref, sf0_ref, sb0_ref,
                    d_ref, kf_ref, kb_ref, qf_ref, qb_ref, cf_ref, cb_ref,
                    ro_ref, sbs_ref, *, n_chunks):
    c = RET_CHUNK

    def rows(j):
        return pl.ds(pl.multiple_of(j * c, c), c)

    def cols(h):
        return slice(h * D_HEAD, (h + 1) * D_HEAD)

    def bwd(jj, states):
        j = n_chunks - 1 - jj
        new = []
        for h, s in enumerate(states):
            sbs_ref[h, j] = s
            kb = (k_ref[rows(j), cols(h)].astype(F32) * kb_ref[h]).astype(BF16)
            new.append(s * cb_ref[h, 0:1, :] + _dot_tn(kb, v_ref[rows(j), cols(h)]))
        return tuple(new)

    lax.fori_loop(0, n_chunks, bwd, tuple(sb0_ref[0, 0, h] for h in range(LAT_HEADS)), unroll=LAT_UNROLL)

    def fwd(j, states):
        new = []
        for h, s in enumerate(states):
            q, k, v = q_ref[rows(j), cols(h)], k_ref[rows(j), cols(h)], v_ref[rows(j), cols(h)]
            qf32, kf32 = q.astype(F32), k.astype(F32)
            p = (_dot_nt(q, k) * d_ref[h]).astype(BF16)
            o = jnp.dot(p, v, preferred_element_type=F32)
            o = o + jnp.dot((qf32 * qf_ref[h]).astype(BF16), s.astype(BF16), preferred_element_type=F32)
            o = o + jnp.dot((qf32 * qb_ref[h]).astype(BF16), sbs_ref[h, j].astype(BF16),
                            preferred_element_type=F32)
            ro_ref[rows(j), cols(h)] = _gn_gate(o, gnw_ref[:, cols(h)], sg_ref[rows(j), cols(h)])
            new.append(s * cf_ref[h, 0:1, :] + _dot_tn((kf32 * kf_ref[h]).astype(BF16), v))
        return tuple(new)

    lax.fori_loop(0, n_chunks, fwd, tuple(sf0_ref[0, 0, h] for h in range(LAT_HEADS)), unroll=LAT_UNROLL)


def _ret_lat(q, k, v, sg, gn_w, sf0, sb0, tabs, batch, seq):
    n_chunks = seq // RET_CHUNK
    w = LAT_HEADS * D_HEAD
    tile = pl.BlockSpec((seq, w), lambda s, h: (s, h))
    st = pl.BlockSpec((1, 1, LAT_HEADS, D_HEAD, D_HEAD), lambda s, h: (s, 0, h, 0, 0))
    big = pl.BlockSpec((LAT_HEADS, RET_CHUNK, D_HEAD), lambda s, h: (h, 0, 0))
    small = pl.BlockSpec((LAT_HEADS, 8, D_HEAD), lambda s, h: (h, 0, 0))
    return pl.pallas_call(
        functools.partial(_ret_lat_kernel, n_chunks=n_chunks),
        grid=(batch, N_HEADS // LAT_HEADS),
        in_specs=[tile, tile, tile, tile, pl.BlockSpec((1, w), lambda s, h: (0, h)),
                  st, st, big, big, big, big, big, small, small],
        out_specs=tile,
        out_shape=jax.ShapeDtypeStruct((batch * seq, D_MODEL), BF16),
        scratch_shapes=[pltpu.VMEM((LAT_HEADS, n_chunks, D_HEAD, D_HEAD), F32)],
        compiler_params=_params(("arbitrary", "arbitrary")),
        name="ret_lat",
    )(q, k, v, sg, gn_w, sf0, sb0, *tabs)


def _mixer_kernel(x_ref, ro_ref, cu_ref, gr_ref, gc_ref, mod_ref, wro_ref, wco_ref, wmx_ref, n1p_ref, n2_ref,
                  x1_ref, h2_ref, *, tiles_per_seq, per_seq_mod):
    r = (1 + pl.program_id(0) // tiles_per_seq) if per_seq_mod else 0
    g1 = mod_ref[2, pl.ds(r, 1), :]
    sh2 = mod_ref[3, pl.ds(r, 1), :]
    sc2 = mod_ref[4, pl.ds(r, 1), :]
    part = x_ref.shape[0] // MIXER_PARTS
    rows = [slice(p * part, (p + 1) * part) for p in range(MIXER_PARTS)]
    outs = [(jnp.dot(cu_ref[rs, :], wco_ref[...], preferred_element_type=F32),
             jnp.dot(ro_ref[rs, :], wro_ref[...], preferred_element_type=F32)) for rs in rows]
    ms = []
    for rs, (conv_out, ret_out) in zip(rows, outs):
        mixed = gr_ref[rs, :].astype(F32) * ret_out + gc_ref[rs, :].astype(F32) * conv_out
        ms.append(jnp.dot(mixed.astype(BF16), wmx_ref[...], preferred_element_type=F32))
    for rs, m in zip(rows, ms):
        x1 = x_ref[rs, :] + g1 * _rms(m, n1p_ref[...])
        x1_ref[rs, :] = x1
        h2_ref[rs, :] = (_rms(x1, n2_ref[...]) * (1.0 + sc2) + sh2).astype(BF16)


def _mixer(x, ro, cu, gr, gc, mod, wro, wco, wmx, n1p, n2, *, seq, tm, per_seq_mod):
    t = x.shape[0]
    tile = pl.BlockSpec((tm, D_MODEL), lambda i: (i, 0))
    vec = _resident((1, D_MODEL))
    mat = _resident((D_MODEL, D_MODEL))
    return pl.pallas_call(
        functools.partial(_mixer_kernel, tiles_per_seq=max(seq // tm, 1), per_seq_mod=per_seq_mod),
        grid=(t // tm,),
        in_specs=[tile, tile, tile, tile, tile, _resident(mod.shape), mat, mat, mat, vec, vec],
        out_specs=[tile, tile],
        out_shape=[jax.ShapeDtypeStruct((t, D_MODEL), F32), jax.ShapeDtypeStruct((t, D_MODEL), BF16)],
        compiler_params=_params(("arbitrary",)),
        name="mixer_lat" if per_seq_mod else "mixer_ctx",
    )(x, ro, cu, gr, gc, mod, wro, wco, wmx, n1p, n2)


def _ffn_kernel(x1_ref, h2_ref, hp_ref, hn_ref, mod_ref, wup_ref, fw_ref, fb_ref, wdn_ref, n2p_ref,
                y_ref, *, tm, tiles_per_seq, per_seq_mod):
    i = pl.program_id(0)
    r = (1 + i // tiles_per_seq) if per_seq_mod else 0
    g2 = mod_ref[5, pl.ds(r, 1), :]
    first = (i % tiles_per_seq) == 0
    last = (i % tiles_per_seq) == tiles_per_seq - 1

    whole_seq = tiles_per_seq == 1
    if whole_seq:
        n_ext = tm
        he = h2_ref[...]
        sub = lax.broadcasted_iota(jnp.int32, (8, 1), 0)
    else:
        n_ext = tm + HALO
        row = lax.broadcasted_iota(jnp.int32, (HALO, 1), 0)
        zero = jnp.zeros((), BF16)
        halo = jnp.where(row < HALO // 2, jnp.where(last, zero, hn_ref[...]), jnp.where(first, zero, hp_ref[...]))
        he = jnp.concatenate([h2_ref[...], halo], axis=0)

    inner = slice(0, tm)
    n_chunks = D_FF // FFN_COLS

    def cols(cc, half):
        return slice(half * D_FF + cc * FFN_COLS, half * D_FF + (cc + 1) * FFN_COLS)

    def up_proj(cc):
        return [jnp.dot(he, wup_ref[:, cols(cc, half)], preferred_element_type=F32) for half in range(2)]

    def conv3(up, cs):
        before = pltpu.roll(up, 1, axis=0)[inner]
        after = pltpu.roll(up, n_ext - 1, axis=0)[inner]
        if whole_seq:
            before = jnp.concatenate([jnp.where(sub == 0, 0.0, before[0:8]), before[8:]], axis=0)
            after = jnp.concatenate([after[:tm - 8], jnp.where(sub == 7, 0.0, after[tm - 8:])], axis=0)
        return (fb_ref[:, cs] + before * fw_ref[0:1, cs] + up[inner] * fw_ref[1:2, cs]
                + after * fw_ref[2:3, cs])

    f = None
    ups = [up_proj(cc) for cc in range(FFN_AHEAD)]
    for cc in range(n_chunks):
        up = ups[cc]
        if cc + FFN_AHEAD < n_chunks:
            ups.append(up_proj(cc + FFN_AHEAD))
        a, gl = conv3(up[0], cols(cc, 0)), conv3(up[1], cols(cc, 1))
        act = ((a * _sigmoid(a)) * gl).astype(BF16)
        part = jnp.dot(act, wdn_ref[cc * FFN_COLS:(cc + 1) * FFN_COLS, :], preferred_element_type=F32)
        f = part if f is None else f + part
    y_ref[...] = x1_ref[...] + g2 * _rms(f, n2p_ref[...])


def _ffn(x1, h2, mod, wup, fw, fb, wdn, n2p, *, seq, tm, per_seq_mod):
    t = x1.shape[0]
    tiles_per_seq = seq // tm
    hb = tm // HALO
    n_hb = t // HALO
    tile = pl.BlockSpec((tm, D_MODEL), lambda i: (i, 0))
    prev = pl.BlockSpec((HALO, D_MODEL), lambda i: (jnp.maximum(i * hb - 1, 0), 0))
    nxt = pl.BlockSpec((HALO, D_MODEL), lambda i: (jnp.minimum((i + 1) * hb, n_hb - 1), 0))
    return pl.pallas_call(
        functools.partial(_ffn_kernel, tm=tm, tiles_per_seq=tiles_per_seq, per_seq_mod=per_seq_mod),
        grid=(t // tm,),
        in_specs=[tile, tile, prev, nxt, _resident(mod.shape), _resident(wup.shape),
                  _resident(fw.shape), _resident(fb.shape), _resident(wdn.shape), _resident((1, D_MODEL))],
        out_specs=tile,
        out_shape=jax.ShapeDtypeStruct((t, D_MODEL), F32),
        compiler_params=_params(("arbitrary",)),
        name="ffn_lat" if per_seq_mod else "ffn_ctx",
    )(x1, h2, h2, h2, mod, wup, fw, fb, wdn, n2p)


def kernel(x_prompt, x_sample, state_ret_fwd, state_ret_bwd, c, c_ctx, norm1_pre, norm1_post, norm2_pre, norm2_post, ada_w, ada_b, w_in, ret_decay_fwd, ret_decay_bwd, ret_gn_w, ret_w_out, conv_dw_w, conv_dw_b, conv_ln_w, conv_ln_b, conv_w_out, w_mix_out, ffn_w_up, ffn_dw_w, ffn_dw_b, ffn_w_down):
    batch, seq, _ = x_prompt.shape
    dec_batch, dec_seq, _ = x_sample.shape
    assert norm1_pre.shape[0] == 1 and dec_batch + 1 <= 8

    cond8 = jnp.zeros((8, D_MODEL), F32).at[0].set(c_ctx).at[1:1 + dec_batch].set(c)
    mod = _adaln(cond8, ada_w[0], ada_b[0])
    tabs = _decay_tables(ret_decay_fwd[0], ret_decay_bwd[0])

    w_in_b = w_in[0].astype(BF16)
    cw = jnp.broadcast_to(conv_dw_w[0][:, None, :], (CONV_WIDTH, 8, D_MODEL))
    cb = jnp.broadcast_to(conv_dw_b, (8, D_MODEL))
    fw = jnp.zeros((8, 2 * D_FF), F32).at[:FFN_CONV_WIDTH].set(ffn_dw_w[0])
    fb = ffn_dw_b
    conv = (cw, cb, conv_ln_w, conv_ln_b)

    xp = x_prompt.reshape(batch * seq, D_MODEL)
    later = (ret_w_out[0], conv_w_out[0], w_mix_out[0], ffn_w_up[0], ffn_w_down[0])
    ro, cu, gr, gc, sf, sb, wro, wco, wmx, wup, wdn = _proj(
        xp, mod, norm1_pre, w_in_b, *conv, latent=False, seq=seq, tm=seq, gn_w=ret_gn_w, tabs=tabs, later=later)

    def tail(x, ro, cu, gr, gc, *, latent, seq_len):
        x1, h2 = _mixer(x, ro, cu, gr, gc, mod, wro, wco, wmx, norm1_post, norm2_pre,
                        seq=seq_len, tm=512, per_seq_mod=latent)
        return _ffn(x1, h2, mod, wup, fw, fb, wdn, norm2_post, seq=seq_len, tm=256, per_seq_mod=latent)

    yp = tail(xp, ro, cu, gr, gc, latent=False, seq_len=seq)

    xs = x_sample.reshape(dec_batch * dec_seq, D_MODEL)
    q, k, v, sg, cu, gr, gc = _proj(xs, mod, norm1_pre, w_in_b, *conv, latent=True, seq=dec_seq, tm=256)
    ro = _ret_lat(q, k, v, sg, ret_gn_w, state_ret_fwd, state_ret_bwd, tabs, dec_batch, dec_seq)
    ys = tail(xs, ro, cu, gr, gc, latent=True, seq_len=dec_seq)
    return (yp.reshape(batch, seq, D_MODEL), ys.reshape(dec_batch, dec_seq, D_MODEL), sf, sb)
```

```python
import functools

import numpy as np
import jax
import jax.numpy as jnp
from jax import lax
from jax.experimental import pallas as pl
from jax.experimental.pallas import tpu as pltpu

F32 = jnp.float32
BF16 = jnp.bfloat16

D_MODEL = 1024
N_HEADS = 4
D_HEAD = 256
D_FF = 2816
CONV_WIDTH = 31
FFN_CONV_WIDTH = 3
GRID_W = 64
ROPE_BASE = 10000.0
EPS = 1e-6
GN_EPS = 1e-5
RET_CHUNK = 256
ADALN_CHUNKS = 3
LAT_HEADS = 2
LAT_UNROLL = 4
HALO = 16
CONV_COLS = 256
PROJ_CONV_ROWS = 64
MIXER_PARTS = 4
FFN_COLS = 256
FFN_AHEAD = 3
V7X_VMEM_BYTES = 64 * 1024 * 1024
VMEM_LIMIT = V7X_VMEM_BYTES - 8 * 1024 * 1024


def _sigmoid(x):
    return 1.0 / (1.0 + jnp.exp(-x))


def _rms(x, w):
    ms = jnp.mean(x * x, axis=-1, keepdims=True)
    return (x * lax.rsqrt(ms + EPS)) * w


def _params(sem):
    return pltpu.CompilerParams(dimension_semantics=sem, vmem_limit_bytes=VMEM_LIMIT)


def _resident(shape):
    nd = len(shape)
    return pl.BlockSpec(shape, lambda *_: (0,) * nd, pipeline_mode=pl.Buffered(1))


def _adaln_kernel(cond_ref, w_ref, b_ref, o_ref):
    c = cond_ref[...]
    s = (c * _sigmoid(c)).astype(BF16)
    for j in range(ADALN_CHUNKS):
        w = w_ref[:, j * D_MODEL:(j + 1) * D_MODEL].astype(BF16)
        o_ref[j] = jnp.dot(s, w, preferred_element_type=F32) + b_ref[:, j * D_MODEL:(j + 1) * D_MODEL]


def _adaln(cond8, ada_w, ada_b):
    n = ADALN_CHUNKS
    return pl.pallas_call(
        _adaln_kernel,
        grid=(6 // n,),
        in_specs=[pl.BlockSpec((8, D_MODEL), lambda j: (0, 0)),
                  pl.BlockSpec((D_MODEL, n * D_MODEL), lambda j: (0, j)),
                  pl.BlockSpec((1, n * D_MODEL), lambda j: (0, j))],
        out_specs=pl.BlockSpec((n, 8, D_MODEL), lambda j: (j, 0, 0)),
        out_shape=jax.ShapeDtypeStruct((6, 8, D_MODEL), F32),
        compiler_params=_params(("arbitrary",)),
        name="adaln",
    )(cond8, ada_w, ada_b.reshape(1, 6 * D_MODEL))


def _decay_kernel(df_ref, db_ref, d_ref, kf_ref, kb_ref, qf_ref, qb_ref, cf_ref, cb_ref):
    def log_sigmoid(x):
        return -(jnp.maximum(-x, 0.0) + jnp.log1p(jnp.exp(-jnp.abs(x))))

    lgf = log_sigmoid(df_ref[...])
    lgb = log_sigmoid(db_ref[...])
    c = RET_CHUNK
    row = lax.broadcasted_iota(jnp.int32, (c, 128), 0).astype(F32)
    col = lax.broadcasted_iota(jnp.int32, (c, 128), 1).astype(F32)
    for h in range(N_HEADS):
        f = lgf[h:h + 1, :]
        b = lgb[h:h + 1, :]
        for half in range(2):
            sl = slice(half * 128, (half + 1) * 128)
            rel = row - (col + 128.0 * half)
            fwd = jnp.exp(jnp.maximum(rel, 0.0) * f)
            bwd = jnp.exp(jnp.maximum(-rel, 0.0) * b)
            d_ref[h, :, sl] = jnp.where(rel > 0, fwd, jnp.where(rel < 0, bwd, 2.0))
            kf_ref[h, :, sl] = jnp.exp((c - 1.0 - row) * f)
            kb_ref[h, :, sl] = jnp.exp(row * b)
            qf_ref[h, :, sl] = jnp.exp((row + 1.0) * f)
            qb_ref[h, :, sl] = jnp.exp((c - row) * b)
            cf_ref[h, :, sl] = jnp.exp(jnp.broadcast_to(c * f, (8, 128)))
            cb_ref[h, :, sl] = jnp.exp(jnp.broadcast_to(c * b, (8, 128)))


def _decay_tables(dec_f, dec_b):
    def lanes(d):
        return jnp.broadcast_to(d[:, None], (N_HEADS, 128))

    big = jax.ShapeDtypeStruct((N_HEADS, RET_CHUNK, D_HEAD), F32)
    small = jax.ShapeDtypeStruct((N_HEADS, 8, D_HEAD), F32)
    return pl.pallas_call(
        _decay_kernel,
        out_shape=(big, big, big, big, big, small, small),
        name="decay_tables",
    )(lanes(dec_f), lanes(dec_b))


def _rope_tables(seq):
    rows = seq // GRID_W
    nf = D_HEAD // 4
    r = np.repeat(np.arange(rows), GRID_W).astype(np.float64)
    c = np.tile(np.arange(GRID_W), rows).astype(np.float64)
    inv = ROPE_BASE ** (-np.arange(nf, dtype=np.float64) / nf)
    ar, ac = r[:, None] * inv[None, :], c[:, None] * inv[None, :]
    cos = np.concatenate([np.cos(ar), np.cos(ar), np.cos(ac), np.cos(ac)], axis=1)
    sin = np.concatenate([-np.sin(ar), np.sin(ar), -np.sin(ac), np.sin(ac)], axis=1)
    return jnp.asarray(cos, F32), jnp.asarray(sin, F32)


def _proj_kernel(*refs, latent, tm, tiles_per_seq, n_later=0):
    it = iter(refs)
    x_ref = next(it)
    xp_ref, xn_ref = (next(it), next(it)) if latent else (None, None)
    mod_ref, n1_ref, w_ref, cw_ref, cb_ref, lnw_ref, lnb_ref = (next(it) for _ in range(7))
    if latent:
        cos_ref, sin_ref = next(it), next(it)
        q_ref, k_ref, v_ref, sg_ref, cu_ref, gr_ref, gc_ref = (next(it) for _ in range(7))
    else:
        gnw_ref, d_ref, kf_ref, kb_ref = (next(it) for _ in range(4))
        later_f32 = [next(it) for _ in range(n_later)]
        ro_ref, cu_ref, gr_ref, gc_ref, sf_ref, sb_ref = (next(it) for _ in range(6))
        for src, dst in zip(later_f32, [next(it) for _ in range(n_later)]):
            dst[...] = src[...].astype(BF16)
    pad_ref, uc_ref = it

    i = pl.program_id(0)
    r = (1 + i // tiles_per_seq) if latent else 0
    sh1 = mod_ref[0, pl.ds(r, 1), :]
    sc1 = mod_ref[1, pl.ds(r, 1), :]

    def norm(x):
        return (_rms(x, n1_ref[...]) * (1.0 + sc1) + sh1).astype(BF16)

    hb = norm(x_ref[...])
    n_pad = tm + 2 * HALO
    if latent:
        he = jnp.concatenate([norm(xp_ref[...]), hb, norm(xn_ref[...])], axis=0)
        row = lax.broadcasted_iota(jnp.int32, (n_pad, 1), 0)
        first = (i % tiles_per_seq) == 0
        last = (i % tiles_per_seq) == tiles_per_seq - 1
        outside = (first & (row < HALO)) | (last & (row >= HALO + tm))
    else:
        he = hb

    def proj(lhs, col, width=D_MODEL):
        return jnp.dot(lhs, w_ref[:, col:col + width], preferred_element_type=F32)

    off = HALO - CONV_WIDTH // 2
    n_sub = PROJ_CONV_ROWS // 8

    def conv_chunk(c):
        cs = slice(c * CONV_COLS, (c + 1) * CONV_COLS)
        u = (proj(he, 4 * D_MODEL + c * CONV_COLS, CONV_COLS)
             * _sigmoid(proj(he, 5 * D_MODEL + c * CONV_COLS, CONV_COLS)))
        if latent:
            pad_ref[c, 0] = jnp.where(outside, 0.0, u)
        else:
            pad_ref[c, 0, 0:HALO, :] = jnp.zeros((HALO, CONV_COLS), F32)
            pad_ref[c, 0, HALO:HALO + tm, :] = u
            pad_ref[c, 0, HALO + tm:, :] = jnp.zeros((HALO, CONV_COLS), F32)
        padded = pad_ref[c, 0]
        for s in range(1, 8):
            pad_ref[c, s] = pltpu.roll(padded, n_pad - s, axis=0)
        for rb in range(tm // PROJ_CONV_ROWS):
            base = rb * PROJ_CONV_ROWS
            acc = [cb_ref[:, cs]] * n_sub
            for j in range(CONV_WIDTH):
                s = j + off
                w = cw_ref[j, :, cs]
                for a in range(n_sub):
                    lo = base + (s // 8 + a) * 8
                    acc[a] = acc[a] + pad_ref[c, s % 8, lo:lo + 8, :] * w
            for a in range(n_sub):
                uc_ref[base + a * 8:base + (a + 1) * 8, cs] = acc[a]

    def conv_norm():
        uc = uc_ref[...]
        mu = jnp.mean(uc, axis=-1, keepdims=True)
        d = uc - mu
        var = jnp.mean(d * d, axis=-1, keepdims=True)
        ln = (d * lax.rsqrt(var + GN_EPS)) * lnw_ref[...] + lnb_ref[...]
        cu_ref[...] = (ln * _sigmoid(ln)).astype(BF16)

    def rotate(a):
        slabs = []
        for j in range(D_MODEL // 128):
            slab = a[:, j * 128:(j + 1) * 128]
            tsl = slice((j % 2) * 128, (j % 2 + 1) * 128)
            slabs.append(slab * cos_ref[:, tsl] + pltpu.roll(slab, 64, axis=1) * sin_ref[:, tsl])
        return jnp.concatenate(slabs, axis=1)

    def head(h):
        sl = slice(h * D_HEAD, (h + 1) * D_HEAD)
        qh, kh, vh = q[:, sl], k[:, sl], v[:, sl]
        p = (_dot_nt(qh, kh) * d_ref[h]).astype(BF16)
        o = jnp.dot(p, vh, preferred_element_type=F32)
        ro_ref[:, sl] = _gn_gate(o, gnw_ref[:, sl], sg[:, sl])
        kf = kh.astype(F32)
        sf_ref[0, 0, h] = _dot_tn((kf * kf_ref[h]).astype(BF16), vh)
        sb_ref[0, 0, h] = _dot_tn((kf * kb_ref[h]).astype(BF16), vh)

    def gate_c():
        a_gc = proj(hb, 7 * D_MODEL)
        gc_ref[...] = _sigmoid(a_gc).astype(BF16)

    conv_chunk(0)
    conv_chunk(1)
    a_q = proj(hb, 0)
    q = ((rotate(a_q) if latent else a_q) * D_HEAD ** -0.5).astype(BF16)
    conv_chunk(2)
    a_k = proj(hb, D_MODEL)
    k = (rotate(a_k) if latent else a_k).astype(BF16)
    a_v = proj(hb, 2 * D_MODEL)
    v = a_v.astype(BF16)
    a_g = proj(hb, 3 * D_MODEL)
    sg = a_g * _sigmoid(a_g)
    if latent:
        q_ref[...] = q
        k_ref[...] = k
        v_ref[...] = v
        sg_ref[...] = sg.astype(BF16)
    else:
        for h in range(N_HEADS):
            head(h)
    a_gr = proj(hb, 6 * D_MODEL)
    gr_ref[...] = _sigmoid(a_gr).astype(BF16)
    if latent:
        gate_c()
        conv_chunk(3)
    else:
        conv_chunk(3)
        gate_c()
    conv_norm()


def _proj(x, mod, n1_pre, w_in, cw, cb, lnw, lnb, *, latent, seq, tm, gn_w=None, tabs=None, later=()):
    t = x.shape[0]
    tiles_per_seq = seq // tm
    hb = tm // HALO
    n_hb = t // HALO
    tile = pl.BlockSpec((tm, D_MODEL), lambda i: (i, 0))
    vec = _resident((1, D_MODEL))
    out = jax.ShapeDtypeStruct((t, D_MODEL), BF16)
    in_specs, args = [tile], [x]
    if latent:
        in_specs += [pl.BlockSpec((HALO, D_MODEL), lambda i: (jnp.maximum(i * hb - 1, 0), 0)),
                     pl.BlockSpec((HALO, D_MODEL), lambda i: (jnp.minimum((i + 1) * hb, n_hb - 1), 0))]
        args += [x, x]
    in_specs += [_resident(mod.shape), vec, _resident(w_in.shape), _resident(cw.shape), _resident(cb.shape), vec, vec]
    args += [mod, n1_pre, w_in, cw, cb, lnw, lnb]
    if latent:
        cos, sin = _rope_tables(seq)
        tab = pl.BlockSpec((tm, D_HEAD), lambda i: (i % tiles_per_seq, 0))
        in_specs += [tab, tab]
        args += [cos, sin]
        out_specs, out_shape = [tile] * 7, [out] * 7
    else:
        assert tiles_per_seq == 1 and tm == RET_CHUNK
        d, kf, kb = tabs[0], tabs[1], tabs[2]
        in_specs += [vec, _resident(d.shape), _resident(kf.shape), _resident(kb.shape)]
        args += [gn_w, d, kf, kb]
        st = pl.BlockSpec((1, 1, N_HEADS, D_HEAD, D_HEAD), lambda i: (i, 0, 0, 0, 0))
        st_shape = jax.ShapeDtypeStruct((t // tm, 1, N_HEADS, D_HEAD, D_HEAD), F32)
        out_specs, out_shape = [tile] * 4 + [st, st], [out] * 4 + [st_shape, st_shape]
        steps = t // tm
        for w in later:
            rows, cols = w.shape
            n = max(n for n in range(1, steps + 1) if rows % n == 0 and (rows // n) % HALO == 0)
            blk = pl.BlockSpec((rows // n, cols), lambda i, n=n: (jnp.minimum(i, n - 1), 0))
            in_specs.append(blk)
            args.append(w)
            out_specs.append(blk)
            out_shape.append(jax.ShapeDtypeStruct(w.shape, BF16))
    return pl.pallas_call(
        functools.partial(_proj_kernel, latent=latent, tm=tm, tiles_per_seq=tiles_per_seq, n_later=len(later)),
        grid=(t // tm,),
        in_specs=in_specs,
        out_specs=out_specs,
        out_shape=out_shape,
        scratch_shapes=[pltpu.VMEM((D_MODEL // CONV_COLS, 8, tm + 2 * HALO, CONV_COLS), F32),
                        pltpu.VMEM((tm, D_MODEL), F32)],
        compiler_params=_params(("arbitrary",)),
        name="proj_lat" if latent else "proj_ctx",
    )(*args)


def _dot_nt(a, b):
    return lax.dot_general(a, b, (((1,), (1,)), ((), ())), preferred_element_type=F32)


def _dot_tn(a, b):
    return lax.dot_general(a, b, (((0,), (0,)), ((), ())), preferred_element_type=F32)


def _gn_gate(o, gnw, sg):
    mu = jnp.mean(o, axis=-1, keepdims=True)
    d = o - mu
    var = jnp.mean(d * d, axis=-1, keepdims=True)
    y = (d * lax.rsqrt(var + GN_EPS)) * gnw
    return (sg.astype(F32) * y).astype(BF16)


def _ret_lat_kernel(q_ref, k_ref, v_ref, sg_ref, gnw_ref, sf0_ref, sb0_ref,
                    d_ref, kf_ref, kb_ref, qf_ref, qb_ref, cf_ref, cb_ref,
                    ro_ref, sbs_ref, *, n_chunks):
    c = RET_CHUNK

    def rows(j):
        return pl.ds(pl.multiple_of(j * c, c), c)

    def cols(h):
        return slice(h * D_HEAD, (h + 1) * D_HEAD)

    def bwd(jj, states):
        j = n_chunks - 1 - jj
        new = []
        for h, s in enumerate(states):
            sbs_ref[h, j] = s
            kb = (k_ref[rows(j), cols(h)].astype(F32) * kb_ref[h]).astype(BF16)
            new.append(s * cb_ref[h, 0:1, :] + _dot_tn(kb, v_ref[rows(j), cols(h)]))
        return tuple(new)

    lax.fori_loop(0, n_chunks, bwd, tuple(sb0_ref[0, 0, h] for h in range(LAT_HEADS)), unroll=LAT_UNROLL)

    def fwd(j, states):
        new = []
        for h, s in enumerate(states):
            q, k, v = q_ref[rows(j), cols(h)], k_ref[rows(j), cols(h)], v_ref[rows(j), cols(h)]
            qf32, kf32 = q.astype(F32), k.astype(F32)
            p = (_dot_nt(q, k) * d_ref[h]).astype(BF16)
            o = jnp.dot(p, v, preferred_element_type=F32)
            o = o + jnp.dot((qf32 * qf_ref[h]).astype(BF16), s.astype(BF16), preferred_element_type=F32)
            o = o + jnp.dot((qf32 * qb_ref[h]).astype(BF16), sbs_ref[h, j].astype(BF16),
                            preferred_element_type=F32)
            ro_ref[rows(j), cols(h)] = _gn_gate(o, gnw_ref[:, cols(h)], sg_ref[rows(j), cols(h)])
            new.append(s * cf_ref[h, 0:1, :] + _dot_tn((kf32 * kf_ref[h]).astype(BF16), v))
        return tuple(new)

    lax.fori_loop(0, n_chunks, fwd, tuple(sf0_ref[0, 0, h] for h in range(LAT_HEADS)), unroll=LAT_UNROLL)


def _ret_lat(q, k, v, sg, gn_w, sf0, sb0, tabs, batch, seq):
    n_chunks = seq // RET_CHUNK
    w = LAT_HEADS * D_HEAD
    tile = pl.BlockSpec((seq, w), lambda s, h: (s, h))
    st = pl.BlockSpec((1, 1, LAT_HEADS, D_HEAD, D_HEAD), lambda s, h: (s, 0, h, 0, 0))
    big = pl.BlockSpec((LAT_HEADS, RET_CHUNK, D_HEAD), lambda s, h: (h, 0, 0))
    small = pl.BlockSpec((LAT_HEADS, 8, D_HEAD), lambda s, h: (h, 0, 0))
    return pl.pallas_call(
        functools.partial(_ret_lat_kernel, n_chunks=n_chunks),
        grid=(batch, N_HEADS // LAT_HEADS),
        in_specs=[tile, tile, tile, tile, pl.BlockSpec((1, w), lambda s, h: (0, h)),
                  st, st, big, big, big, big, big, small, small],
        out_specs=tile,
        out_shape=jax.ShapeDtypeStruct((batch * seq, D_MODEL), BF16),
        scratch_shapes=[pltpu.VMEM((LAT_HEADS, n_chunks, D_HEAD, D_HEAD), F32)],
        compiler_params=_params(("arbitrary", "arbitrary")),
        name="ret_lat",
    )(q, k, v, sg, gn_w, sf0, sb0, *tabs)


def _mixer_kernel(x_ref, ro_ref, cu_ref, gr_ref, gc_ref, mod_ref, wro_ref, wco_ref, wmx_ref, n1p_ref, n2_ref,
                  x1_ref, h2_ref, *, tiles_per_seq, per_seq_mod):
    r = (1 + pl.program_id(0) // tiles_per_seq) if per_seq_mod else 0
    g1 = mod_ref[2, pl.ds(r, 1), :]
    sh2 = mod_ref[3, pl.ds(r, 1), :]
    sc2 = mod_ref[4, pl.ds(r, 1), :]
    part = x_ref.shape[0] // MIXER_PARTS
    rows = [slice(p * part, (p + 1) * part) for p in range(MIXER_PARTS)]
    outs = [(jnp.dot(cu_ref[rs, :], wco_ref[...], preferred_element_type=F32),
             jnp.dot(ro_ref[rs, :], wro_ref[...], preferred_element_type=F32)) for rs in rows]
    ms = []
    for rs, (conv_out, ret_out) in zip(rows, outs):
        mixed = gr_ref[rs, :].astype(F32) * ret_out + gc_ref[rs, :].astype(F32) * conv_out
        ms.append(jnp.dot(mixed.astype(BF16), wmx_ref[...], preferred_element_type=F32))
    for rs, m in zip(rows, ms):
        x1 = x_ref[rs, :] + g1 * _rms(m, n1p_ref[...])
        x1_ref[rs, :] = x1
        h2_ref[rs, :] = (_rms(x1, n2_ref[...]) * (1.0 + sc2) + sh2).astype(BF16)


def _mixer(x, ro, cu, gr, gc, mod, wro, wco, wmx, n1p, n2, *, seq, tm, per_seq_mod):
    t = x.shape[0]
    tile = pl.BlockSpec((tm, D_MODEL), lambda i: (i, 0))
    vec = _resident((1, D_MODEL))
    mat = _resident((D_MODEL, D_MODEL))
    return pl.pallas_call(
        functools.partial(_mixer_kernel, tiles_per_seq=max(seq // tm, 1), per_seq_mod=per_seq_mod),
        grid=(t // tm,),
        in_specs=[tile, tile, tile, tile, tile, _resident(mod.shape), mat, mat, mat, vec, vec],
        out_specs=[tile, tile],
        out_shape=[jax.ShapeDtypeStruct((t, D_MODEL), F32), jax.ShapeDtypeStruct((t, D_MODEL), BF16)],
        compiler_params=_params(("arbitrary",)),
        name="mixer_lat" if per_seq_mod else "mixer_ctx",
    )(x, ro, cu, gr, gc, mod, wro, wco, wmx, n1p, n2)


def _ffn_kernel(x1_ref, h2_ref, hp_ref, hn_ref, mod_ref, wup_ref, fw_ref, fb_ref, wdn_ref, n2p_ref,
                y_ref, *, tm, tiles_per_seq, per_seq_mod):
    i = pl.program_id(0)
    r = (1 + i // tiles_per_seq) if per_seq_mod else 0
    g2 = mod_ref[5, pl.ds(r, 1), :]
    first = (i % tiles_per_seq) == 0
    last = (i % tiles_per_seq) == tiles_per_seq - 1

    whole_seq = tiles_per_seq == 1
    if whole_seq:
        n_ext = tm
        he = h2_ref[...]
        sub = lax.broadcasted_iota(jnp.int32, (8, 1), 0)
    else:
        n_ext = tm + HALO
        row = lax.broadcasted_iota(jnp.int32, (HALO, 1), 0)
        zero = jnp.zeros((), BF16)
        halo = jnp.where(row < HALO // 2, jnp.where(last, zero, hn_ref[...]), jnp.where(first, zero, hp_ref[...]))
        he = jnp.concatenate([h2_ref[...], halo], axis=0)

    inner = slice(0, tm)
    n_chunks = D_FF // FFN_COLS

    def cols(cc, half):
        return slice(half * D_FF + cc * FFN_COLS, half * D_FF + (cc + 1) * FFN_COLS)

    def up_proj(cc):
        return [jnp.dot(he, wup_ref[:, cols(cc, half)], preferred_element_type=F32) for half in range(2)]

    def conv3(up, cs):
        before = pltpu.roll(up, 1, axis=0)[inner]
        after = pltpu.roll(up, n_ext - 1, axis=0)[inner]
        if whole_seq:
            before = jnp.concatenate([jnp.where(sub == 0, 0.0, before[0:8]), before[8:]], axis=0)
            after = jnp.concatenate([after[:tm - 8], jnp.where(sub == 7, 0.0, after[tm - 8:])], axis=0)
        return (fb_ref[:, cs] + before * fw_ref[0:1, cs] + up[inner] * fw_ref[1:2, cs]
                + after * fw_ref[2:3, cs])

    f = None
    ups = [up_proj(cc) for cc in range(FFN_AHEAD)]
    for cc in range(n_chunks):
        up = ups[cc]
        if cc + FFN_AHEAD < n_chunks:
            ups.append(up_proj(cc + FFN_AHEAD))
        a, gl = conv3(up[0], cols(cc, 0)), conv3(up[1], cols(cc, 1))
        act = ((a * _sigmoid(a)) * gl).astype(BF16)
        part = jnp.dot(act, wdn_ref[cc * FFN_COLS:(cc + 1) * FFN_COLS, :], preferred_element_type=F32)
        f = part if f is None else f + part
    y_ref[...] = x1_ref[...] + g2 * _rms(f, n2p_ref[...])


def _ffn(x1, h2, mod, wup, fw, fb, wdn, n2p, *, seq, tm, per_seq_mod):
    t = x1.shape[0]
    tiles_per_seq = seq // tm
    hb = tm // HALO
    n_hb = t // HALO
    tile = pl.BlockSpec((tm, D_MODEL), lambda i: (i, 0))
    prev = pl.BlockSpec((HALO, D_MODEL), lambda i: (jnp.maximum(i * hb - 1, 0), 0))
    nxt = pl.BlockSpec((HALO, D_MODEL), lambda i: (jnp.minimum((i + 1) * hb, n_hb - 1), 0))
    return pl.pallas_call(
        functools.partial(_ffn_kernel, tm=tm, tiles_per_seq=tiles_per_seq, per_seq_mod=per_seq_mod),
        grid=(t // tm,),
        in_specs=[tile, tile, prev, nxt, _resident(mod.shape), _resident(wup.shape),
                  _resident(fw.shape), _resident(fb.shape), _resident(wdn.shape), _resident((1, D_MODEL))],
        out_specs=tile,
        out_shape=jax.ShapeDtypeStruct((t, D_MODEL), F32),
        compiler_params=_params(("arbitrary",)),
        name="ffn_lat" if per_seq_mod else "ffn_ctx",
    )(x1, h2, h2, h2, mod, wup, fw, fb, wdn, n2p)


def kernel(x_prompt, x_sample, state_ret_fwd, state_ret_bwd, c, c_ctx, norm1_pre, norm1_post, norm2_pre, norm2_post, ada_w, ada_b, w_in, ret_decay_fwd, ret_decay_bwd, ret_gn_w, ret_w_out, conv_dw_w, conv_dw_b, conv_ln_w, conv_ln_b, conv_w_out, w_mix_out, ffn_w_up, ffn_dw_w, ffn_dw_b, ffn_w_down):
    batch, seq, _ = x_prompt.shape
    dec_batch, dec_seq, _ = x_sample.shape
    assert norm1_pre.shape[0] == 1 and dec_batch + 1 <= 8

    cond8 = jnp.concatenate([c_ctx[None], c, jnp.zeros((7 - dec_batch, D_MODEL), F32)], axis=0)
    mod = _adaln(cond8, ada_w[0], ada_b[0])
    tabs = _decay_tables(ret_decay_fwd[0], ret_decay_bwd[0])

    w_in_b = w_in[0].astype(BF16)
    cw = jnp.broadcast_to(conv_dw_w[0][:, None, :], (CONV_WIDTH, 8, D_MODEL))
    cb = jnp.broadcast_to(conv_dw_b, (8, D_MODEL))
    fw, fb = ffn_dw_w[0], ffn_dw_b
    conv = (cw, cb, conv_ln_w, conv_ln_b)

    xp = x_prompt.reshape(batch * seq, D_MODEL)
    later = (ret_w_out[0], conv_w_out[0], w_mix_out[0], ffn_w_up[0], ffn_w_down[0])
    ro, cu, gr, gc, sf, sb, wro, wco, wmx, wup, wdn = _proj(
        xp, mod, norm1_pre, w_in_b, *conv, latent=False, seq=seq, tm=seq, gn_w=ret_gn_w, tabs=tabs, later=later)

    def tail(x, ro, cu, gr, gc, *, latent, seq_len):
        x1, h2 = _mixer(x, ro, cu, gr, gc, mod, wro, wco, wmx, norm1_post, norm2_pre,
                        seq=seq_len, tm=512, per_seq_mod=latent)
        return _ffn(x1, h2, mod, wup, fw, fb, wdn, norm2_post, seq=seq_len, tm=256, per_seq_mod=latent)

    yp = tail(xp, ro, cu, gr, gc, latent=False, seq_len=seq)

    xs = x_sample.reshape(dec_batch * dec_seq, D_MODEL)
    q, k, v, sg, cu, gr, gc = _proj(xs, mod, norm1_pre, w_in_b, *conv, latent=True, seq=dec_seq, tm=256)
    ro = _ret_lat(q, k, v, sg, ret_gn_w, state_ret_fwd, state_ret_bwd, tabs, dec_batch, dec_seq)
    ys = tail(xs, ro, cu, gr, gc, latent=True, seq_len=dec_seq)
    return (yp.reshape(batch, seq, D_MODEL), ys.reshape(dec_batch, dec_seq, D_MODEL), sf, sb)
```

```python
import functools

import numpy as np
import jax
import jax.numpy as jnp
from jax import lax
from jax.experimental import pallas as pl
from jax.experimental.pallas import tpu as pltpu

F32 = jnp.float32
BF16 = jnp.bfloat16

D_MODEL = 1024
N_HEADS = 4
D_HEAD = 256
D_FF = 2816
CONV_WIDTH = 31
FFN_CONV_WIDTH = 3
GRID_W = 64
ROPE_BASE = 10000.0
EPS = 1e-6
GN_EPS = 1e-5
RET_CHUNK = 256
ADALN_CHUNKS = 3
LAT_HEADS = 2
LAT_UNROLL = 8
HALO = 16
CONV_COLS = 256
PROJ_CONV_ROWS = 64
MIXER_PARTS = 4
FFN_COLS = 256
FFN_AHEAD = 4
V7X_VMEM_BYTES = 64 * 1024 * 1024
VMEM_LIMIT = V7X_VMEM_BYTES - 8 * 1024 * 1024


def _sigmoid(x):
    return 1.0 / (1.0 + jnp.exp(-x))


def _rms(x, w):
    ms = jnp.mean(x * x, axis=-1, keepdims=True)
    return (x * lax.rsqrt(ms + EPS)) * w


def _params(sem):
    return pltpu.CompilerParams(dimension_semantics=sem, vmem_limit_bytes=VMEM_LIMIT)


def _resident(shape):
    nd = len(shape)
    return pl.BlockSpec(shape, lambda *_: (0,) * nd, pipeline_mode=pl.Buffered(1))


def _adaln_kernel(cond_ref, w_ref, b_ref, o_ref):
    c = cond_ref[...]
    s = (c * _sigmoid(c)).astype(BF16)
    for j in range(ADALN_CHUNKS):
        w = w_ref[:, j * D_MODEL:(j + 1) * D_MODEL].astype(BF16)
        o_ref[j] = jnp.dot(s, w, preferred_element_type=F32) + b_ref[:, j * D_MODEL:(j + 1) * D_MODEL]


def _adaln(cond8, ada_w, ada_b):
    n = ADALN_CHUNKS
    return pl.pallas_call(
        _adaln_kernel,
        grid=(6 // n,),
        in_specs=[pl.BlockSpec((8, D_MODEL), lambda j: (0, 0)),
                  pl.BlockSpec((D_MODEL, n * D_MODEL), lambda j: (0, j)),
                  pl.BlockSpec((1, n * D_MODEL), lambda j: (0, j))],
        out_specs=pl.BlockSpec((n, 8, D_MODEL), lambda j: (j, 0, 0)),
        out_shape=jax.ShapeDtypeStruct((6, 8, D_MODEL), F32),
        compiler_params=_params(("arbitrary",)),
        name="adaln",
    )(cond8, ada_w, ada_b.reshape(1, 6 * D_MODEL))


def _decay_kernel(df_ref, db_ref, d_ref, kf_ref, kb_ref, qf_ref, qb_ref, cf_ref, cb_ref):
    def log_sigmoid(x):
        return -(jnp.maximum(-x, 0.0) + jnp.log1p(jnp.exp(-jnp.abs(x))))

    lgf = log_sigmoid(df_ref[...])
    lgb = log_sigmoid(db_ref[...])
    c = RET_CHUNK
    row = lax.broadcasted_iota(jnp.int32, (c, 128), 0).astype(F32)
    col = lax.broadcasted_iota(jnp.int32, (c, 128), 1).astype(F32)
    for h in range(N_HEADS):
        f = lgf[h:h + 1, :]
        b = lgb[h:h + 1, :]
        for half in range(2):
            sl = slice(half * 128, (half + 1) * 128)
            rel = row - (col + 128.0 * half)
            fwd = jnp.exp(jnp.maximum(rel, 0.0) * f)
            bwd = jnp.exp(jnp.maximum(-rel, 0.0) * b)
            d_ref[h, :, sl] = jnp.where(rel > 0, fwd, jnp.where(rel < 0, bwd, 2.0))
            kf_ref[h, :, sl] = jnp.exp((c - 1.0 - row) * f)
            kb_ref[h, :, sl] = jnp.exp(row * b)
            qf_ref[h, :, sl] = jnp.exp((row + 1.0) * f)
            qb_ref[h, :, sl] = jnp.exp((c - row) * b)
            cf_ref[h, :, sl] = jnp.exp(jnp.broadcast_to(c * f, (8, 128)))
            cb_ref[h, :, sl] = jnp.exp(jnp.broadcast_to(c * b, (8, 128)))


def _decay_tables(dec_f, dec_b):
    def lanes(d):
        return jnp.broadcast_to(d[:, None], (N_HEADS, 128))

    big = jax.ShapeDtypeStruct((N_HEADS, RET_CHUNK, D_HEAD), F32)
    small = jax.ShapeDtypeStruct((N_HEADS, 8, D_HEAD), F32)
    return pl.pallas_call(
        _decay_kernel,
        out_shape=(big, big, big, big, big, small, small),
        name="decay_tables",
    )(lanes(dec_f), lanes(dec_b))


def _rope_tables(seq):
    rows = seq // GRID_W
    nf = D_HEAD // 4
    r = np.repeat(np.arange(rows), GRID_W).astype(np.float64)
    c = np.tile(np.arange(GRID_W), rows).astype(np.float64)
    inv = ROPE_BASE ** (-np.arange(nf, dtype=np.float64) / nf)
    ar, ac = r[:, None] * inv[None, :], c[:, None] * inv[None, :]
    cos = np.concatenate([np.cos(ar), np.cos(ar), np.cos(ac), np.cos(ac)], axis=1)
    sin = np.concatenate([-np.sin(ar), np.sin(ar), -np.sin(ac), np.sin(ac)], axis=1)
    return jnp.asarray(cos, F32), jnp.asarray(sin, F32)


def _proj_kernel(*refs, latent, tm, tiles_per_seq, n_later=0):
    it = iter(refs)
    x_ref = next(it)
    xp_ref, xn_ref = (next(it), next(it)) if latent else (None, None)
    mod_ref, n1_ref, w_ref, cw_ref, cb_ref, lnw_ref, lnb_ref = (next(it) for _ in range(7))
    if latent:
        cos_ref, sin_ref = next(it), next(it)
        q_ref, k_ref, v_ref, sg_ref, cu_ref, gr_ref, gc_ref = (next(it) for _ in range(7))
    else:
        gnw_ref, d_ref, kf_ref, kb_ref = (next(it) for _ in range(4))
        later_f32 = [next(it) for _ in range(n_later)]
        ro_ref, cu_ref, gr_ref, gc_ref, sf_ref, sb_ref = (next(it) for _ in range(6))
        for src, dst in zip(later_f32, [next(it) for _ in range(n_later)]):
            dst[...] = src[...].astype(BF16)
    pad_ref, uc_ref = it

    i = pl.program_id(0)
    r = (1 + i // tiles_per_seq) if latent else 0
    sh1 = mod_ref[0, pl.ds(r, 1), :]
    sc1 = mod_ref[1, pl.ds(r, 1), :]

    def norm(x):
        return (_rms(x, n1_ref[...]) * (1.0 + sc1) + sh1).astype(BF16)

    hb = norm(x_ref[...])
    n_pad = tm + 2 * HALO
    if latent:
        he = jnp.concatenate([norm(xp_ref[...]), hb, norm(xn_ref[...])], axis=0)
        row = lax.broadcasted_iota(jnp.int32, (n_pad, 1), 0)
        first = (i % tiles_per_seq) == 0
        last = (i % tiles_per_seq) == tiles_per_seq - 1
        outside = (first & (row < HALO)) | (last & (row >= HALO + tm))
    else:
        he = hb

    def proj(lhs, col, width=D_MODEL):
        return jnp.dot(lhs, w_ref[:, col:col + width], preferred_element_type=F32)

    off = HALO - CONV_WIDTH // 2
    n_sub = PROJ_CONV_ROWS // 8

    def conv_chunk(c):
        cs = slice(c * CONV_COLS, (c + 1) * CONV_COLS)
        u = (proj(he, 4 * D_MODEL + c * CONV_COLS, CONV_COLS)
             * _sigmoid(proj(he, 5 * D_MODEL + c * CONV_COLS, CONV_COLS)))
        if latent:
            pad_ref[c, 0] = jnp.where(outside, 0.0, u)
        else:
            pad_ref[c, 0, 0:HALO, :] = jnp.zeros((HALO, CONV_COLS), F32)
            pad_ref[c, 0, HALO:HALO + tm, :] = u
            pad_ref[c, 0, HALO + tm:, :] = jnp.zeros((HALO, CONV_COLS), F32)
        padded = pad_ref[c, 0]
        for s in range(1, 8):
            pad_ref[c, s] = pltpu.roll(padded, n_pad - s, axis=0)
        for rb in range(tm // PROJ_CONV_ROWS):
            base = rb * PROJ_CONV_ROWS
            acc = [cb_ref[:, cs]] * n_sub
            for j in range(CONV_WIDTH):
                s = j + off
                w = cw_ref[j, :, cs]
                for a in range(n_sub):
                    lo = base + (s // 8 + a) * 8
                    acc[a] = acc[a] + pad_ref[c, s % 8, lo:lo + 8, :] * w
            for a in range(n_sub):
                uc_ref[base + a * 8:base + (a + 1) * 8, cs] = acc[a]

    def conv_norm():
        uc = uc_ref[...]
        mu = jnp.mean(uc, axis=-1, keepdims=True)
        d = uc - mu
        var = jnp.mean(d * d, axis=-1, keepdims=True)
        ln = (d * lax.rsqrt(var + GN_EPS)) * lnw_ref[...] + lnb_ref[...]
        cu_ref[...] = (ln * _sigmoid(ln)).astype(BF16)

    def rotate(a):
        slabs = []
        for j in range(D_MODEL // 128):
            slab = a[:, j * 128:(j + 1) * 128]
            tsl = slice((j % 2) * 128, (j % 2 + 1) * 128)
            slabs.append(slab * cos_ref[:, tsl] + pltpu.roll(slab, 64, axis=1) * sin_ref[:, tsl])
        return jnp.concatenate(slabs, axis=1)

    def head(h):
        sl = slice(h * D_HEAD, (h + 1) * D_HEAD)
        qh, kh, vh = q[:, sl], k[:, sl], v[:, sl]
        p = (_dot_nt(qh, kh) * d_ref[h]).astype(BF16)
        o = jnp.dot(p, vh, preferred_element_type=F32)
        ro_ref[:, sl] = _gn_gate(o, gnw_ref[:, sl], sg[:, sl])
        kf = kh.astype(F32)
        sf_ref[0, 0, h] = _dot_tn((kf * kf_ref[h]).astype(BF16), vh)
        sb_ref[0, 0, h] = _dot_tn((kf * kb_ref[h]).astype(BF16), vh)

    def gate_c():
        a_gc = proj(hb, 7 * D_MODEL)
        gc_ref[...] = _sigmoid(a_gc).astype(BF16)

    conv_chunk(0)
    conv_chunk(1)
    a_q = proj(hb, 0)
    q = ((rotate(a_q) if latent else a_q) * D_HEAD ** -0.5).astype(BF16)
    conv_chunk(2)
    a_k = proj(hb, D_MODEL)
    k = (rotate(a_k) if latent else a_k).astype(BF16)
    a_v = proj(hb, 2 * D_MODEL)
    v = a_v.astype(BF16)
    a_g = proj(hb, 3 * D_MODEL)
    sg = a_g * _sigmoid(a_g)
    if latent:
        q_ref[...] = q
        k_ref[...] = k
        v_ref[...] = v
        sg_ref[...] = sg.astype(BF16)
    else:
        for h in range(N_HEADS):
            head(h)
    a_gr = proj(hb, 6 * D_MODEL)
    gr_ref[...] = _sigmoid(a_gr).astype(BF16)
    if latent:
        gate_c()
        conv_chunk(3)
    else:
        conv_chunk(3)
        gate_c()
    conv_norm()


def _proj(x, mod, n1_pre, w_in, cw, cb, lnw, lnb, *, latent, seq, tm, gn_w=None, tabs=None, later=()):
    t = x.shape[0]
    tiles_per_seq = seq // tm
    hb = tm // HALO
    n_hb = t // HALO
    tile = pl.BlockSpec((tm, D_MODEL), lambda i: (i, 0))
    vec = _resident((1, D_MODEL))
    out = jax.ShapeDtypeStruct((t, D_MODEL), BF16)
    in_specs, args = [tile], [x]
    if latent:
        in_specs += [pl.BlockSpec((HALO, D_MODEL), lambda i: (jnp.maximum(i * hb - 1, 0), 0)),
                     pl.BlockSpec((HALO, D_MODEL), lambda i: (jnp.minimum((i + 1) * hb, n_hb - 1), 0))]
        args += [x, x]
    in_specs += [_resident(mod.shape), vec, _resident(w_in.shape), _resident(cw.shape), _resident(cb.shape), vec, vec]
    args += [mod, n1_pre, w_in, cw, cb, lnw, lnb]
    if latent:
        cos, sin = _rope_tables(seq)
        tab = pl.BlockSpec((tm, D_HEAD), lambda i: (i % tiles_per_seq, 0))
        in_specs += [tab, tab]
        args += [cos, sin]
        out_specs, out_shape = [tile] * 7, [out] * 7
    else:
        assert tiles_per_seq == 1 and tm == RET_CHUNK
        d, kf, kb = tabs[0], tabs[1], tabs[2]
        in_specs += [vec, _resident(d.shape), _resident(kf.shape), _resident(kb.shape)]
        args += [gn_w, d, kf, kb]
        st = pl.BlockSpec((1, 1, N_HEADS, D_HEAD, D_HEAD), lambda i: (i, 0, 0, 0, 0))
        st_shape = jax.ShapeDtypeStruct((t // tm, 1, N_HEADS, D_HEAD, D_HEAD), F32)
        out_specs, out_shape = [tile] * 4 + [st, st], [out] * 4 + [st_shape, st_shape]
        steps = t // tm
        for w in later:
            rows, cols = w.shape
            n = max(n for n in range(1, steps + 1) if rows % n == 0 and (rows // n) % HALO == 0)
            blk = pl.BlockSpec((rows // n, cols), lambda i, n=n: (jnp.minimum(i, n - 1), 0))
            in_specs.append(blk)
            args.append(w)
            out_specs.append(blk)
            out_shape.append(jax.ShapeDtypeStruct(w.shape, BF16))
    return pl.pallas_call(
        functools.partial(_proj_kernel, latent=latent, tm=tm, tiles_per_seq=tiles_per_seq, n_later=len(later)),
        grid=(t // tm,),
        in_specs=in_specs,
        out_specs=out_specs,
        out_shape=out_shape,
        scratch_shapes=[pltpu.VMEM((D_MODEL // CONV_COLS, 8, tm + 2 * HALO, CONV_COLS), F32),
                        pltpu.VMEM((tm, D_MODEL), F32)],
        compiler_params=_params(("arbitrary",)),
        name="proj_lat" if latent else "proj_ctx",
    )(*args)


def _dot_nt(a, b):
    return lax.dot_general(a, b, (((1,), (1,)), ((), ())), preferred_element_type=F32)


def _dot_tn(a, b):
    return lax.dot_general(a, b, (((0,), (0,)), ((), ())), preferred_element_type=F32)


def _gn_gate(o, gnw, sg):
    mu = jnp.mean(o, axis=-1, keepdims=True)
    d = o - mu
    var = jnp.mean(d * d, axis=-1, keepdims=True)
    y = (d * lax.rsqrt(var + GN_EPS)) * gnw
    return (sg.astype(F32) * y).astype(BF16)


def _ret_lat_kernel(q_ref, k_ref, v_ref, sg_ref, gnw_ref, sf0_ref, sb0_ref,
                    d_ref, kf_ref, kb_ref, qf_ref, qb_ref, cf_ref, cb_ref,
                    ro_ref, sbs_ref, *, n_chunks):
    c = RET_CHUNK

    def rows(j):
        return pl.ds(pl.multiple_of(j * c, c), c)

    def cols(h):
        return slice(h * D_HEAD, (h + 1) * D_HEAD)

    def bwd(jj, states):
        j = n_chunks - 1 - jj
        new = []
        for h, s in enumerate(states):
            sbs_ref[h, j] = s
            kb = (k_ref[rows(j), cols(h)].astype(F32) * kb_ref[h]).astype(BF16)
            new.append(s * cb_ref[h, 0:1, :] + _dot_tn(kb, v_ref[rows(j), cols(h)]))
        return tuple(new)

    lax.fori_loop(0, n_chunks, bwd, tuple(sb0_ref[0, 0, h] for h in range(LAT_HEADS)), unroll=LAT_UNROLL)

    def fwd(j, states):
        new = []
        for h, s in enumerate(states):
            q, k, v = q_ref[rows(j), cols(h)], k_ref[rows(j), cols(h)], v_ref[rows(j), cols(h)]
            qf32, kf32 = q.astype(F32), k.astype(F32)
            p = (_dot_nt(q, k) * d_ref[h]).astype(BF16)
            o = jnp.dot(p, v, preferred_element_type=F32)
            o = o + jnp.dot((qf32 * qf_ref[h]).astype(BF16), s.astype(BF16), preferred_element_type=F32)
            o = o + jnp.dot((qf32 * qb_ref[h]).astype(BF16), sbs_ref[h, j].astype(BF16),
                            preferred_element_type=F32)
            ro_ref[rows(j), cols(h)] = _gn_gate(o, gnw_ref[:, cols(h)], sg_ref[rows(j), cols(h)])
            new.append(s * cf_ref[h, 0:1, :] + _dot_tn((kf32 * kf_ref[h]).astype(BF16), v))
        return tuple(new)

    lax.fori_loop(0, n_chunks, fwd, tuple(sf0_ref[0, 0, h] for h in range(LAT_HEADS)), unroll=LAT_UNROLL)


def _ret_lat(q, k, v, sg, gn_w, sf0, sb0, tabs, batch, seq):
    n_chunks = seq // RET_CHUNK
    w = LAT_HEADS * D_HEAD
    tile = pl.BlockSpec((seq, w), lambda s, h: (s, h))
    st = pl.BlockSpec((1, 1, LAT_HEADS, D_HEAD, D_HEAD), lambda s, h: (s, 0, h, 0, 0))
    big = pl.BlockSpec((LAT_HEADS, RET_CHUNK, D_HEAD), lambda s, h: (h, 0, 0))
    small = pl.BlockSpec((LAT_HEADS, 8, D_HEAD), lambda s, h: (h, 0, 0))
    return pl.pallas_call(
        functools.partial(_ret_lat_kernel, n_chunks=n_chunks),
        grid=(batch, N_HEADS // LAT_HEADS),
        in_specs=[tile, tile, tile, tile, pl.BlockSpec((1, w), lambda s, h: (0, h)),
                  st, st, big, big, big, big, big, small, small],
        out_specs=tile,
        out_shape=jax.ShapeDtypeStruct((batch * seq, D_MODEL), BF16),
        scratch_shapes=[pltpu.VMEM((LAT_HEADS, n_chunks, D_HEAD, D_HEAD), F32)],
        compiler_params=_params(("arbitrary", "arbitrary")),
        name="ret_lat",
    )(q, k, v, sg, gn_w, sf0, sb0, *tabs)


def _mixer_kernel(x_ref, ro_ref, cu_ref, gr_ref, gc_ref, mod_ref, wro_ref, wco_ref, wmx_ref, n1p_ref, n2_ref,
                  x1_ref, h2_ref, *, tiles_per_seq, per_seq_mod):
    r = (1 + pl.program_id(0) // tiles_per_seq) if per_seq_mod else 0
    g1 = mod_ref[2, pl.ds(r, 1), :]
    sh2 = mod_ref[3, pl.ds(r, 1), :]
    sc2 = mod_ref[4, pl.ds(r, 1), :]
    part = x_ref.shape[0] // MIXER_PARTS
    rows = [slice(p * part, (p + 1) * part) for p in range(MIXER_PARTS)]
    outs = [(jnp.dot(cu_ref[rs, :], wco_ref[...], preferred_element_type=F32),
             jnp.dot(ro_ref[rs, :], wro_ref[...], preferred_element_type=F32)) for rs in rows]
    ms = []
    for rs, (conv_out, ret_out) in zip(rows, outs):
        mixed = gr_ref[rs, :].astype(F32) * ret_out + gc_ref[rs, :].astype(F32) * conv_out
        ms.append(jnp.dot(mixed.astype(BF16), wmx_ref[...], preferred_element_type=F32))
    for rs, m in zip(rows, ms):
        x1 = x_ref[rs, :] + g1 * _rms(m, n1p_ref[...])
        x1_ref[rs, :] = x1
        h2_ref[rs, :] = (_rms(x1, n2_ref[...]) * (1.0 + sc2) + sh2).astype(BF16)


def _mixer(x, ro, cu, gr, gc, mod, wro, wco, wmx, n1p, n2, *, seq, tm, per_seq_mod):
    t = x.shape[0]
    tile = pl.BlockSpec((tm, D_MODEL), lambda i: (i, 0))
    vec = _resident((1, D_MODEL))
    mat = _resident((D_MODEL, D_MODEL))
    return pl.pallas_call(
        functools.partial(_mixer_kernel, tiles_per_seq=max(seq // tm, 1), per_seq_mod=per_seq_mod),
        grid=(t // tm,),
        in_specs=[tile, tile, tile, tile, tile, _resident(mod.shape), mat, mat, mat, vec, vec],
        out_specs=[tile, tile],
        out_shape=[jax.ShapeDtypeStruct((t, D_MODEL), F32), jax.ShapeDtypeStruct((t, D_MODEL), BF16)],
        compiler_params=_params(("arbitrary",)),
        name="mixer_lat" if per_seq_mod else "mixer_ctx",
    )(x, ro, cu, gr, gc, mod, wro, wco, wmx, n1p, n2)


def _ffn_kernel(x1_ref, h2_ref, hp_ref, hn_ref, mod_ref, wup_ref, fw_ref, fb_ref, wdn_ref, n2p_ref,
                y_ref, *, tm, tiles_per_seq, per_seq_mod):
    i = pl.program_id(0)
    r = (1 + i // tiles_per_seq) if per_seq_mod else 0
    g2 = mod_ref[5, pl.ds(r, 1), :]
    first = (i % tiles_per_seq) == 0
    last = (i % tiles_per_seq) == tiles_per_seq - 1

    whole_seq = tiles_per_seq == 1
    if whole_seq:
        n_ext = tm
        he = h2_ref[...]
        sub = lax.broadcasted_iota(jnp.int32, (8, 1), 0)
    else:
        n_ext = tm + HALO
        row = lax.broadcasted_iota(jnp.int32, (HALO, 1), 0)
        zero = jnp.zeros((), BF16)
        halo = jnp.where(row < HALO // 2, jnp.where(last, zero, hn_ref[...]), jnp.where(first, zero, hp_ref[...]))
        he = jnp.concatenate([h2_ref[...], halo], axis=0)

    inner = slice(0, tm)
    n_chunks = D_FF // FFN_COLS

    def cols(cc, half):
        return slice(half * D_FF + cc * FFN_COLS, half * D_FF + (cc + 1) * FFN_COLS)

    def up_proj(cc):
        return [jnp.dot(he, wup_ref[:, cols(cc, half)], preferred_element_type=F32) for half in range(2)]

    def conv3(up, cs):
        before = pltpu.roll(up, 1, axis=0)[inner]
        after = pltpu.roll(up, n_ext - 1, axis=0)[inner]
        if whole_seq:
            before = jnp.concatenate([jnp.where(sub == 0, 0.0, before[0:8]), before[8:]], axis=0)
            after = jnp.concatenate([after[:tm - 8], jnp.where(sub == 7, 0.0, after[tm - 8:])], axis=0)
        return (fb_ref[:, cs] + before * fw_ref[0:1, cs] + up[inner] * fw_ref[1:2, cs]
                + after * fw_ref[2:3, cs])

    f = None
    ups = [up_proj(cc) for cc in range(FFN_AHEAD)]
    for cc in range(n_chunks):
        up = ups[cc]
        if cc + FFN_AHEAD < n_chunks:
            ups.append(up_proj(cc + FFN_AHEAD))
        a, gl = conv3(up[0], cols(cc, 0)), conv3(up[1], cols(cc, 1))
        act = ((a * _sigmoid(a)) * gl).astype(BF16)
        part = jnp.dot(act, wdn_ref[cc * FFN_COLS:(cc + 1) * FFN_COLS, :], preferred_element_type=F32)
        f = part if f is None else f + part
    y_ref[...] = x1_ref[...] + g2 * _rms(f, n2p_ref[...])


def _ffn(x1, h2, mod, wup, fw, fb, wdn, n2p, *, seq, tm, per_seq_mod):
    t = x1.shape[0]
    tiles_per_seq = seq // tm
    hb = tm // HALO
    n_hb = t // HALO
    tile = pl.BlockSpec((tm, D_MODEL), lambda i: (i, 0))
    prev = pl.BlockSpec((HALO, D_MODEL), lambda i: (jnp.maximum(i * hb - 1, 0), 0))
    nxt = pl.BlockSpec((HALO, D_MODEL), lambda i: (jnp.minimum((i + 1) * hb, n_hb - 1), 0))
    return pl.pallas_call(
        functools.partial(_ffn_kernel, tm=tm, tiles_per_seq=tiles_per_seq, per_seq_mod=per_seq_mod),
        grid=(t // tm,),
        in_specs=[tile, tile, prev, nxt, _resident(mod.shape), _resident(wup.shape),
                  _resident(fw.shape), _resident(fb.shape), _resident(wdn.shape), _resident((1, D_MODEL))],
        out_specs=tile,
        out_shape=jax.ShapeDtypeStruct((t, D_MODEL), F32),
        compiler_params=_params(("arbitrary",)),
        name="ffn_lat" if per_seq_mod else "ffn_ctx",
    )(x1, h2, h2, h2, mod, wup, fw, fb, wdn, n2p)


def kernel(x_prompt, x_sample, state_ret_fwd, state_ret_bwd, c, c_ctx, norm1_pre, norm1_post, norm2_pre, norm2_post, ada_w, ada_b, w_in, ret_decay_fwd, ret_decay_bwd, ret_gn_w, ret_w_out, conv_dw_w, conv_dw_b, conv_ln_w, conv_ln_b, conv_w_out, w_mix_out, ffn_w_up, ffn_dw_w, ffn_dw_b, ffn_w_down):
    batch, seq, _ = x_prompt.shape
    dec_batch, dec_seq, _ = x_sample.shape
    assert norm1_pre.shape[0] == 1 and dec_batch + 1 <= 8

    cond8 = jnp.concatenate([c_ctx[None], c, jnp.zeros((7 - dec_batch, D_MODEL), F32)], axis=0)
    mod = _adaln(cond8, ada_w[0], ada_b[0])
    tabs = _decay_tables(ret_decay_fwd[0], ret_decay_bwd[0])

    w_in_b = w_in[0].astype(BF16)
    cw = jnp.broadcast_to(conv_dw_w[0][:, None, :], (CONV_WIDTH, 8, D_MODEL))
    cb = jnp.broadcast_to(conv_dw_b, (8, D_MODEL))
    fw, fb = ffn_dw_w[0], ffn_dw_b
    conv = (cw, cb, conv_ln_w, conv_ln_b)

    xp = x_prompt.reshape(batch * seq, D_MODEL)
    later = (ret_w_out[0], conv_w_out[0], w_mix_out[0], ffn_w_up[0], ffn_w_down[0])
    ro, cu, gr, gc, sf, sb, wro, wco, wmx, wup, wdn = _proj(
        xp, mod, norm1_pre, w_in_b, *conv, latent=False, seq=seq, tm=seq, gn_w=ret_gn_w, tabs=tabs, later=later)

    def tail(x, ro, cu, gr, gc, *, latent, seq_len):
        x1, h2 = _mixer(x, ro, cu, gr, gc, mod, wro, wco, wmx, norm1_post, norm2_pre,
                        seq=seq_len, tm=512, per_seq_mod=latent)
        return _ffn(x1, h2, mod, wup, fw, fb, wdn, norm2_post, seq=seq_len, tm=256, per_seq_mod=latent)

    yp = tail(xp, ro, cu, gr, gc, latent=False, seq_len=seq)

    xs = x_sample.reshape(dec_batch * dec_seq, D_MODEL)
    q, k, v, sg, cu, gr, gc = _proj(xs, mod, norm1_pre, w_in_b, *conv, latent=True, seq=dec_seq, tm=256)
    ro = _ret_lat(q, k, v, sg, ret_gn_w, state_ret_fwd, state_ret_bwd, tabs, dec_batch, dec_seq)
    ys = tail(xs, ro, cu, gr, gc, latent=True, seq_len=dec_seq)
    return (yp.reshape(batch, seq, D_MODEL), ys.reshape(dec_batch, dec_seq, D_MODEL), sf, sb)
```

```python
import functools

import numpy as np
import jax
import jax.numpy as jnp
from jax import lax
from jax.experimental import pallas as pl
from jax.experimental.pallas import tpu as pltpu

F32 = jnp.float32
BF16 = jnp.bfloat16

D_MODEL = 1024
N_HEADS = 4
D_HEAD = 256
D_FF = 2816
CONV_WIDTH = 31
FFN_CONV_WIDTH = 3
GRID_W = 64
ROPE_BASE = 10000.0
EPS = 1e-6
GN_EPS = 1e-5
RET_CHUNK = 256
LAT_HEADS = 2
LAT_UNROLL = 4
HALO = 16
CONV_COLS = 256
PROJ_CONV_ROWS = 64
MIXER_PARTS = 4
FFN_COLS = 256
FFN_AHEAD = 3
VMEM_LIMIT = 56 * 1024 * 1024


def _sigmoid(x):
    return 1.0 / (1.0 + jnp.exp(-x))


def _rms(x, w):
    ms = jnp.mean(x * x, axis=-1, keepdims=True)
    return (x * lax.rsqrt(ms + EPS)) * w


def _params(sem):
    return pltpu.CompilerParams(dimension_semantics=sem, vmem_limit_bytes=VMEM_LIMIT)


def _resident(shape):
    nd = len(shape)
    return pl.BlockSpec(shape, lambda *_: (0,) * nd, pipeline_mode=pl.Buffered(1))


def _adaln_kernel(cond_ref, w_ref, b_ref, win_ref, o_ref, winb_ref):
    c = cond_ref[...]
    s = (c * _sigmoid(c)).astype(BF16)
    o_ref[0] = jnp.dot(s, w_ref[...].astype(BF16), preferred_element_type=F32) + b_ref[0]
    winb_ref[...] = win_ref[...].astype(BF16)


def _adaln(cond8, ada_w, ada_b, w_in):
    n_mod = ada_w.shape[1] // D_MODEL
    steps = w_in.shape[1] // D_MODEL
    assert steps >= n_mod
    last = n_mod - 1
    return pl.pallas_call(
        _adaln_kernel,
        grid=(steps,),
        in_specs=[pl.BlockSpec((8, D_MODEL), lambda j: (0, 0)),
                  pl.BlockSpec((D_MODEL, D_MODEL), lambda j: (0, jnp.minimum(j, last))),
                  pl.BlockSpec((1, 1, D_MODEL), lambda j: (jnp.minimum(j, last), 0, 0)),
                  pl.BlockSpec((D_MODEL, D_MODEL), lambda j: (0, j))],
        out_specs=[pl.BlockSpec((1, 8, D_MODEL), lambda j: (jnp.minimum(j, last), 0, 0)),
                   pl.BlockSpec((D_MODEL, D_MODEL), lambda j: (0, j))],
        out_shape=[jax.ShapeDtypeStruct((n_mod, 8, D_MODEL), F32), jax.ShapeDtypeStruct(w_in.shape, BF16)],
        compiler_params=_params(("arbitrary",)),
        name="adaln",
    )(cond8, ada_w, ada_b.reshape(n_mod, 1, D_MODEL), w_in)


def _decay_kernel(df_ref, db_ref, d_ref, kf_ref, kb_ref, qf_ref, qb_ref, cf_ref, cb_ref):
    def log_sigmoid(x):
        return -(jnp.maximum(-x, 0.0) + jnp.log1p(jnp.exp(-jnp.abs(x))))

    lgf = log_sigmoid(df_ref[...])
    lgb = log_sigmoid(db_ref[...])
    c = RET_CHUNK
    row = lax.broadcasted_iota(jnp.int32, (c, 128), 0).astype(F32)
    col = lax.broadcasted_iota(jnp.int32, (c, 128), 1).astype(F32)
    for h in range(N_HEADS):
        f = lgf[h:h + 1, :]
        b = lgb[h:h + 1, :]
        for half in range(2):
            sl = slice(half * 128, (half + 1) * 128)
            rel = row - (col + 128.0 * half)
            fwd = jnp.exp(jnp.maximum(rel, 0.0) * f)
            bwd = jnp.exp(jnp.maximum(-rel, 0.0) * b)
            d_ref[h, :, sl] = jnp.where(rel > 0, fwd, jnp.where(rel < 0, bwd, 2.0))
            kf_ref[h, :, sl] = jnp.exp((c - 1.0 - row) * f)
            kb_ref[h, :, sl] = jnp.exp(row * b)
            qf_ref[h, :, sl] = jnp.exp((row + 1.0) * f)
            qb_ref[h, :, sl] = jnp.exp((c - row) * b)
            cf_ref[h, :, sl] = jnp.exp(jnp.broadcast_to(c * f, (8, 128)))
            cb_ref[h, :, sl] = jnp.exp(jnp.broadcast_to(c * b, (8, 128)))


def _decay_tables(dec_f, dec_b):
    def lanes(d):
        return jnp.zeros((8, 128), F32).at[:N_HEADS].set(jnp.broadcast_to(d[:, None], (N_HEADS, 128)))

    big = jax.ShapeDtypeStruct((N_HEADS, RET_CHUNK, D_HEAD), F32)
    small = jax.ShapeDtypeStruct((N_HEADS, 8, D_HEAD), F32)
    return pl.pallas_call(
        _decay_kernel,
        out_shape=(big, big, big, big, big, small, small),
        name="decay_tables",
    )(lanes(dec_f), lanes(dec_b))


def _rope_tables(seq):
    rows = seq // GRID_W
    nf = D_HEAD // 4
    r = np.repeat(np.arange(rows), GRID_W).astype(np.float64)
    c = np.tile(np.arange(GRID_W), rows).astype(np.float64)
    inv = ROPE_BASE ** (-np.arange(nf, dtype=np.float64) / nf)
    ar, ac = r[:, None] * inv[None, :], c[:, None] * inv[None, :]
    cos = np.concatenate([np.cos(ar), np.cos(ar), np.cos(ac), np.cos(ac)], axis=1)
    sin = np.concatenate([-np.sin(ar), np.sin(ar), -np.sin(ac), np.sin(ac)], axis=1)
    return jnp.asarray(cos, F32), jnp.asarray(sin, F32)


def _proj_kernel(*refs, latent, tm, tiles_per_seq, n_later=0):
    it = iter(refs)
    x_ref = next(it)
    xp_ref, xn_ref = (next(it), next(it)) if latent else (None, None)
    mod_ref, n1_ref, w_ref, cw_ref, cb_ref, lnw_ref, lnb_ref = (next(it) for _ in range(7))
    if latent:
        cos_ref, sin_ref = next(it), next(it)
        q_ref, k_ref, v_ref, sg_ref, cu_ref, gr_ref, gc_ref = (next(it) for _ in range(7))
    else:
        gnw_ref, d_ref, kf_ref, kb_ref = (next(it) for _ in range(4))
        later_f32 = [next(it) for _ in range(n_later)]
        ro_ref, cu_ref, gr_ref, gc_ref, sf_ref, sb_ref = (next(it) for _ in range(6))
        for src, dst in zip(later_f32, [next(it) for _ in range(n_later)]):
            dst[...] = src[...].astype(BF16)
    pad_ref, uc_ref = it

    i = pl.program_id(0)
    r = (1 + i // tiles_per_seq) if latent else 0
    sh1 = mod_ref[0, pl.ds(r, 1), :]
    sc1 = mod_ref[1, pl.ds(r, 1), :]

    def norm(x):
        return (_rms(x, n1_ref[...]) * (1.0 + sc1) + sh1).astype(BF16)

    hb = norm(x_ref[...])
    n_pad = tm + 2 * HALO
    if latent:
        he = jnp.concatenate([norm(xp_ref[...]), hb, norm(xn_ref[...])], axis=0)
        row = lax.broadcasted_iota(jnp.int32, (n_pad, 1), 0)
        first = (i % tiles_per_seq) == 0
        last = (i % tiles_per_seq) == tiles_per_seq - 1
        outside = (first & (row < HALO)) | (last & (row >= HALO + tm))
    else:
        he = hb

    def proj(lhs, col, width=D_MODEL):
        return jnp.dot(lhs, w_ref[:, col:col + width], preferred_element_type=F32)

    off = HALO - CONV_WIDTH // 2
    n_sub = PROJ_CONV_ROWS // 8

    def conv_chunk(c):
        cs = slice(c * CONV_COLS, (c + 1) * CONV_COLS)
        u = (proj(he, 4 * D_MODEL + c * CONV_COLS, CONV_COLS)
             * _sigmoid(proj(he, 5 * D_MODEL + c * CONV_COLS, CONV_COLS)))
        if latent:
            pad_ref[c, 0] = jnp.where(outside, 0.0, u)
        else:
            pad_ref[c, 0, 0:HALO, :] = jnp.zeros((HALO, CONV_COLS), F32)
            pad_ref[c, 0, HALO:HALO + tm, :] = u
            pad_ref[c, 0, HALO + tm:, :] = jnp.zeros((HALO, CONV_COLS), F32)
        padded = pad_ref[c, 0]
        for s in range(1, 8):
            pad_ref[c, s] = pltpu.roll(padded, n_pad - s, axis=0)
        for rb in range(tm // PROJ_CONV_ROWS):
            base = rb * PROJ_CONV_ROWS
            acc = [cb_ref[:, cs]] * n_sub
            for j in range(CONV_WIDTH):
                s = j + off
                w = cw_ref[j, :, cs]
                for a in range(n_sub):
                    lo = base + (s // 8 + a) * 8
                    acc[a] = acc[a] + pad_ref[c, s % 8, lo:lo + 8, :] * w
            for a in range(n_sub):
                uc_ref[base + a * 8:base + (a + 1) * 8, cs] = acc[a]

    def conv_norm():
        uc = uc_ref[...]
        mu = jnp.mean(uc, axis=-1, keepdims=True)
        d = uc - mu
        var = jnp.mean(d * d, axis=-1, keepdims=True)
        ln = (d * lax.rsqrt(var + GN_EPS)) * lnw_ref[...] + lnb_ref[...]
        cu_ref[...] = (ln * _sigmoid(ln)).astype(BF16)

    def rotate(a):
        slabs = []
        for j in range(D_MODEL // 128):
            slab = a[:, j * 128:(j + 1) * 128]
            tsl = slice((j % 2) * 128, (j % 2 + 1) * 128)
            slabs.append(slab * cos_ref[:, tsl] + pltpu.roll(slab, 64, axis=1) * sin_ref[:, tsl])
        return jnp.concatenate(slabs, axis=1)

    def head(h):
        sl = slice(h * D_HEAD, (h + 1) * D_HEAD)
        qh, kh, vh = q[:, sl], k[:, sl], v[:, sl]
        p = (_dot_nt(qh, kh) * d_ref[h]).astype(BF16)
        o = jnp.dot(p, vh, preferred_element_type=F32)
        ro_ref[:, sl] = _gn_gate(o, gnw_ref[:, sl], sg[:, sl])
        kf = kh.astype(F32)
        sf_ref[0, 0, h] = _dot_tn((kf * kf_ref[h]).astype(BF16), vh)
        sb_ref[0, 0, h] = _dot_tn((kf * kb_ref[h]).astype(BF16), vh)

    def gate_c():
        a_gc = proj(hb, 7 * D_MODEL)
        gc_ref[...] = _sigmoid(a_gc).astype(BF16)

    conv_chunk(0)
    conv_chunk(1)
    a_q = proj(hb, 0)
    q = ((rotate(a_q) if latent else a_q) * D_HEAD ** -0.5).astype(BF16)
    conv_chunk(2)
    a_k = proj(hb, D_MODEL)
    k = (rotate(a_k) if latent else a_k).astype(BF16)
    a_v = proj(hb, 2 * D_MODEL)
    v = a_v.astype(BF16)
    a_g = proj(hb, 3 * D_MODEL)
    sg = a_g * _sigmoid(a_g)
    if latent:
        q_ref[...] = q
        k_ref[...] = k
        v_ref[...] = v
        sg_ref[...] = sg.astype(BF16)
    else:
        for h in range(N_HEADS):
            head(h)
    a_gr = proj(hb, 6 * D_MODEL)
    gr_ref[...] = _sigmoid(a_gr).astype(BF16)
    if latent:
        gate_c()
        conv_chunk(3)
    else:
        conv_chunk(3)
        gate_c()
    conv_norm()


def _proj(x, mod, n1_pre, w_in, cw, cb, lnw, lnb, *, latent, seq, tm, gn_w=None, tabs=None, later=()):
    t = x.shape[0]
    tiles_per_seq = seq // tm
    hb = tm // HALO
    n_hb = t // HALO
    tile = pl.BlockSpec((tm, D_MODEL), lambda i: (i, 0))
    vec = _resident((1, D_MODEL))
    out = jax.ShapeDtypeStruct((t, D_MODEL), BF16)
    in_specs, args = [tile], [x]
    if latent:
        in_specs += [pl.BlockSpec((HALO, D_MODEL), lambda i: (jnp.maximum(i * hb - 1, 0), 0)),
                     pl.BlockSpec((HALO, D_MODEL), lambda i: (jnp.minimum((i + 1) * hb, n_hb - 1), 0))]
        args += [x, x]
    in_specs += [_resident(mod.shape), vec, _resident(w_in.shape), _resident(cw.shape), _resident(cb.shape), vec, vec]
    args += [mod, n1_pre, w_in, cw, cb, lnw, lnb]
    if latent:
        cos, sin = _rope_tables(seq)
        tab = pl.BlockSpec((tm, D_HEAD), lambda i: (i % tiles_per_seq, 0))
        in_specs += [tab, tab]
        args += [cos, sin]
        out_specs, out_shape = [tile] * 7, [out] * 7
    else:
        assert tiles_per_seq == 1 and tm == RET_CHUNK
        d, kf, kb = tabs[0], tabs[1], tabs[2]
        in_specs += [vec, _resident(d.shape), _resident(kf.shape), _resident(kb.shape)]
        args += [gn_w, d, kf, kb]
        st = pl.BlockSpec((1, 1, N_HEADS, D_HEAD, D_HEAD), lambda i: (i, 0, 0, 0, 0))
        st_shape = jax.ShapeDtypeStruct((t // tm, 1, N_HEADS, D_HEAD, D_HEAD), F32)
        out_specs, out_shape = [tile] * 4 + [st, st], [out] * 4 + [st_shape, st_shape]
        steps = t // tm
        for w in later:
            rows, cols = w.shape
            n = max(n for n in range(1, steps + 1) if rows % n == 0 and (rows // n) % HALO == 0)
            blk = pl.BlockSpec((rows // n, cols), lambda i, n=n: (jnp.minimum(i, n - 1), 0))
            in_specs.append(blk)
            args.append(w)
            out_specs.append(blk)
            out_shape.append(jax.ShapeDtypeStruct(w.shape, BF16))
    return pl.pallas_call(
        functools.partial(_proj_kernel, latent=latent, tm=tm, tiles_per_seq=tiles_per_seq, n_later=len(later)),
        grid=(t // tm,),
        in_specs=in_specs,
        out_specs=out_specs,
        out_shape=out_shape,
        scratch_shapes=[pltpu.VMEM((D_MODEL // CONV_COLS, 8, tm + 2 * HALO, CONV_COLS), F32),
                        pltpu.VMEM((tm, D_MODEL), F32)],
        compiler_params=_params(("arbitrary",)),
        name="proj_lat" if latent else "proj_ctx",
    )(*args)


def _dot_nt(a, b):
    return lax.dot_general(a, b, (((1,), (1,)), ((), ())), preferred_element_type=F32)


def _dot_tn(a, b):
    return lax.dot_general(a, b, (((0,), (0,)), ((), ())), preferred_element_type=F32)


def _gn_gate(o, gnw, sg):
    mu = jnp.mean(o, axis=-1, keepdims=True)
    d = o - mu
    var = jnp.mean(d * d, axis=-1, keepdims=True)
    y = (d * lax.rsqrt(var + GN_EPS)) * gnw
    return (sg.astype(F32) * y).astype(BF16)


def _ret_lat_kernel(q_ref, k_ref, v_ref, sg_ref, gnw_ref, sf0_ref, sb0_ref,
                    d_ref, kf_ref, kb_ref, qf_ref, qb_ref, cf_ref, cb_ref,
                    ro_ref, sbs_ref, *, n_chunks):
    c = RET_CHUNK

    def rows(j):
        return pl.ds(pl.multiple_of(j * c, c), c)

    def cols(h):
        return slice(h * D_HEAD, (h + 1) * D_HEAD)

    def bwd(jj, states):
        j = n_chunks - 1 - jj
        new = []
        for h, s in enumerate(states):
            sbs_ref[h, j] = s
            kb = (k_ref[rows(j), cols(h)].astype(F32) * kb_ref[h]).astype(BF16)
            new.append(s * cb_ref[h, 0:1, :] + _dot_tn(kb, v_ref[rows(j), cols(h)]))
        return tuple(new)

    lax.fori_loop(0, n_chunks, bwd, tuple(sb0_ref[0, 0, h] for h in range(LAT_HEADS)), unroll=LAT_UNROLL)

    def fwd(j, states):
        new = []
        for h, s in enumerate(states):
            q, k, v = q_ref[rows(j), cols(h)], k_ref[rows(j), cols(h)], v_ref[rows(j), cols(h)]
            qf32, kf32 = q.astype(F32), k.astype(F32)
            p = (_dot_nt(q, k) * d_ref[h]).astype(BF16)
            o = jnp.dot(p, v, preferred_element_type=F32)
            o = o + jnp.dot((qf32 * qf_ref[h]).astype(BF16), s.astype(BF16), preferred_element_type=F32)
            o = o + jnp.dot((qf32 * qb_ref[h]).astype(BF16), sbs_ref[h, j].astype(BF16),
                            preferred_element_type=F32)
            ro_ref[rows(j), cols(h)] = _gn_gate(o, gnw_ref[:, cols(h)], sg_ref[rows(j), cols(h)])
            new.append(s * cf_ref[h, 0:1, :] + _dot_tn((kf32 * kf_ref[h]).astype(BF16), v))
        return tuple(new)

    lax.fori_loop(0, n_chunks, fwd, tuple(sf0_ref[0, 0, h] for h in range(LAT_HEADS)), unroll=LAT_UNROLL)


def _ret_lat(q, k, v, sg, gn_w, sf0, sb0, tabs, batch, seq):
    n_chunks = seq // RET_CHUNK
    w = LAT_HEADS * D_HEAD
    tile = pl.BlockSpec((seq, w), lambda s, h: (s, h))
    st = pl.BlockSpec((1, 1, LAT_HEADS, D_HEAD, D_HEAD), lambda s, h: (s, 0, h, 0, 0))
    big = pl.BlockSpec((LAT_HEADS, RET_CHUNK, D_HEAD), lambda s, h: (h, 0, 0))
    small = pl.BlockSpec((LAT_HEADS, 8, D_HEAD), lambda s, h: (h, 0, 0))
    return pl.pallas_call(
        functools.partial(_ret_lat_kernel, n_chunks=n_chunks),
        grid=(batch, N_HEADS // LAT_HEADS),
        in_specs=[tile, tile, tile, tile, pl.BlockSpec((1, w), lambda s, h: (0, h)),
                  st, st, big, big, big, big, big, small, small],
        out_specs=tile,
        out_shape=jax.ShapeDtypeStruct((batch * seq, D_MODEL), BF16),
        scratch_shapes=[pltpu.VMEM((LAT_HEADS, n_chunks, D_HEAD, D_HEAD), F32)],
        compiler_params=_params(("arbitrary", "arbitrary")),
        name="ret_lat",
    )(q, k, v, sg, gn_w, sf0, sb0, *tabs)


def _mixer_kernel(x_ref, ro_ref, cu_ref, gr_ref, gc_ref, mod_ref, wro_ref, wco_ref, wmx_ref, n1p_ref, n2_ref,
                  x1_ref, h2_ref, *, tiles_per_seq, per_seq_mod):
    r = (1 + pl.program_id(0) // tiles_per_seq) if per_seq_mod else 0
    g1 = mod_ref[2, pl.ds(r, 1), :]
    sh2 = mod_ref[3, pl.ds(r, 1), :]
    sc2 = mod_ref[4, pl.ds(r, 1), :]
    part = x_ref.shape[0] // MIXER_PARTS
    rows = [slice(p * part, (p + 1) * part) for p in range(MIXER_PARTS)]
    outs = [(jnp.dot(cu_ref[rs, :], wco_ref[...], preferred_element_type=F32),
             jnp.dot(ro_ref[rs, :], wro_ref[...], preferred_element_type=F32)) for rs in rows]
    ms = []
    for rs, (conv_out, ret_out) in zip(rows, outs):
        mixed = gr_ref[rs, :].astype(F32) * ret_out + gc_ref[rs, :].astype(F32) * conv_out
        ms.append(jnp.dot(mixed.astype(BF16), wmx_ref[...], preferred_element_type=F32))
    for rs, m in zip(rows, ms):
        x1 = x_ref[rs, :] + g1 * _rms(m, n1p_ref[...])
        x1_ref[rs, :] = x1
        h2_ref[rs, :] = (_rms(x1, n2_ref[...]) * (1.0 + sc2) + sh2).astype(BF16)


def _mixer(x, ro, cu, gr, gc, mod, wro, wco, wmx, n1p, n2, *, seq, tm, per_seq_mod):
    t = x.shape[0]
    tile = pl.BlockSpec((tm, D_MODEL), lambda i: (i, 0))
    vec = _resident((1, D_MODEL))
    mat = _resident((D_MODEL, D_MODEL))
    return pl.pallas_call(
        functools.partial(_mixer_kernel, tiles_per_seq=max(seq // tm, 1), per_seq_mod=per_seq_mod),
        grid=(t // tm,),
        in_specs=[tile, tile, tile, tile, tile, _resident(mod.shape), mat, mat, mat, vec, vec],
        out_specs=[tile, tile],
        out_shape=[jax.ShapeDtypeStruct((t, D_MODEL), F32), jax.ShapeDtypeStruct((t, D_MODEL), BF16)],
        compiler_params=_params(("arbitrary",)),
        name="mixer_lat" if per_seq_mod else "mixer_ctx",
    )(x, ro, cu, gr, gc, mod, wro, wco, wmx, n1p, n2)


def _ffn_kernel(x1_ref, h2_ref, hp_ref, hn_ref, mod_ref, wup_ref, fw_ref, fb_ref, wdn_ref, n2p_ref,
                y_ref, *, tm, tiles_per_seq, per_seq_mod):
    i = pl.program_id(0)
    r = (1 + i // tiles_per_seq) if per_seq_mod else 0
    g2 = mod_ref[5, pl.ds(r, 1), :]
    first = (i % tiles_per_seq) == 0
    last = (i % tiles_per_seq) == tiles_per_seq - 1

    whole_seq = tiles_per_seq == 1
    if whole_seq:
        n_ext = tm
        he = h2_ref[...]
        sub = lax.broadcasted_iota(jnp.int32, (8, 1), 0)
    else:
        n_ext = tm + HALO
        row = lax.broadcasted_iota(jnp.int32, (HALO, 1), 0)
        zero = jnp.zeros((), BF16)
        halo = jnp.where(row < HALO // 2, jnp.where(last, zero, hn_ref[...]), jnp.where(first, zero, hp_ref[...]))
        he = jnp.concatenate([h2_ref[...], halo], axis=0)

    inner = slice(0, tm)
    n_chunks = D_FF // FFN_COLS

    def cols(cc, half):
        return slice(half * D_FF + cc * FFN_COLS, half * D_FF + (cc + 1) * FFN_COLS)

    def up_proj(cc):
        return [jnp.dot(he, wup_ref[:, cols(cc, half)], preferred_element_type=F32) for half in range(2)]

    def conv3(up, cs):
        before = pltpu.roll(up, 1, axis=0)[inner]
        after = pltpu.roll(up, n_ext - 1, axis=0)[inner]
        if whole_seq:
            before = jnp.concatenate([jnp.where(sub == 0, 0.0, before[0:8]), before[8:]], axis=0)
            after = jnp.concatenate([after[:tm - 8], jnp.where(sub == 7, 0.0, after[tm - 8:])], axis=0)
        return (fb_ref[:, cs] + before * fw_ref[0:1, cs] + up[inner] * fw_ref[1:2, cs]
                + after * fw_ref[2:3, cs])

    f = None
    ups = [up_proj(cc) for cc in range(FFN_AHEAD)]
    for cc in range(n_chunks):
        up = ups[cc]
        if cc + FFN_AHEAD < n_chunks:
            ups.append(up_proj(cc + FFN_AHEAD))
        a, gl = conv3(up[0], cols(cc, 0)), conv3(up[1], cols(cc, 1))
        act = ((a * _sigmoid(a)) * gl).astype(BF16)
        part = jnp.dot(act, wdn_ref[cc * FFN_COLS:(cc + 1) * FFN_COLS, :], preferred_element_type=F32)
        f = part if f is None else f + part
    y_ref[...] = x1_ref[...] + g2 * _rms(f, n2p_ref[...])


def _ffn(x1, h2, mod, wup, fw, fb, wdn, n2p, *, seq, tm, per_seq_mod):
    t = x1.shape[0]
    tiles_per_seq = seq // tm
    hb = tm // HALO
    n_hb = t // HALO
    tile = pl.BlockSpec((tm, D_MODEL), lambda i: (i, 0))
    prev = pl.BlockSpec((HALO, D_MODEL), lambda i: (jnp.maximum(i * hb - 1, 0), 0))
    nxt = pl.BlockSpec((HALO, D_MODEL), lambda i: (jnp.minimum((i + 1) * hb, n_hb - 1), 0))
    return pl.pallas_call(
        functools.partial(_ffn_kernel, tm=tm, tiles_per_seq=tiles_per_seq, per_seq_mod=per_seq_mod),
        grid=(t // tm,),
        in_specs=[tile, tile, prev, nxt, _resident(mod.shape), _resident(wup.shape),
                  _resident(fw.shape), _resident(fb.shape), _resident(wdn.shape), _resident((1, D_MODEL))],
        out_specs=tile,
        out_shape=jax.ShapeDtypeStruct((t, D_MODEL), F32),
        compiler_params=_params(("arbitrary",)),
        name="ffn_lat" if per_seq_mod else "ffn_ctx",
    )(x1, h2, h2, h2, mod, wup, fw, fb, wdn, n2p)


def kernel(x_prompt, x_sample, state_ret_fwd, state_ret_bwd, c, c_ctx, norm1_pre, norm1_post, norm2_pre, norm2_post, ada_w, ada_b, w_in, ret_decay_fwd, ret_decay_bwd, ret_gn_w, ret_w_out, conv_dw_w, conv_dw_b, conv_ln_w, conv_ln_b, conv_w_out, w_mix_out, ffn_w_up, ffn_dw_w, ffn_dw_b, ffn_w_down):
    batch, seq, _ = x_prompt.shape
    dec_batch, dec_seq, _ = x_sample.shape
    assert norm1_pre.shape[0] == 1 and dec_batch + 1 <= 8

    cond8 = jnp.zeros((8, D_MODEL), F32).at[0].set(c_ctx).at[1:1 + dec_batch].set(c)
    mod, w_in_b = _adaln(cond8, ada_w[0], ada_b[0], w_in[0])
    tabs = _decay_tables(ret_decay_fwd[0], ret_decay_bwd[0])

    cw = jnp.broadcast_to(conv_dw_w[0][:, None, :], (CONV_WIDTH, 8, D_MODEL))
    cb = jnp.broadcast_to(conv_dw_b, (8, D_MODEL))
    fw = jnp.zeros((8, 2 * D_FF), F32).at[:FFN_CONV_WIDTH].set(ffn_dw_w[0])
    fb = ffn_dw_b
    conv = (cw, cb, conv_ln_w, conv_ln_b)

    xp = x_prompt.reshape(batch * seq, D_MODEL)
    later = (ret_w_out[0], conv_w_out[0], w_mix_out[0], ffn_w_up[0], ffn_w_down[0])
    ro, cu, gr, gc, sf, sb, wro, wco, wmx, wup, wdn = _proj(
        xp, mod, norm1_pre, w_in_b, *conv, latent=False, seq=seq, tm=seq, gn_w=ret_gn_w, tabs=tabs, later=later)

    def tail(x, ro, cu, gr, gc, *, latent, seq_len):
        x1, h2 = _mixer(x, ro, cu, gr, gc, mod, wro, wco, wmx, norm1_post, norm2_pre,
                        seq=seq_len, tm=512, per_seq_mod=latent)
        return _ffn(x1, h2, mod, wup, fw, fb, wdn, norm2_post, seq=seq_len, tm=256, per_seq_mod=latent)

    yp = tail(xp, ro, cu, gr, gc, latent=False, seq_len=seq)

    xs = x_sample.reshape(dec_batch * dec_seq, D_MODEL)
    q, k, v, sg, cu, gr, gc = _proj(xs, mod, norm1_pre, w_in_b, *conv, latent=True, seq=dec_seq, tm=256)
    ro = _ret_lat(q, k, v, sg, ret_gn_w, state_ret_fwd, state_ret_bwd, tabs, dec_batch, dec_seq)
    ys = tail(xs, ro, cu, gr, gc, latent=True, seq_len=dec_seq)
    return (yp.reshape(batch, seq, D_MODEL), ys.reshape(dec_batch, dec_seq, D_MODEL), sf, sb)
```

```python
import functools

import numpy as np
import jax
import jax.numpy as jnp
from jax import lax
from jax.experimental import pallas as pl
from jax.experimental.pallas import tpu as pltpu

F32 = jnp.float32
BF16 = jnp.bfloat16

D_MODEL = 1024
N_HEADS = 4
D_HEAD = 256
D_FF = 2816
CONV_WIDTH = 31
FFN_CONV_WIDTH = 3
GRID_W = 64
ROPE_BASE = 10000.0
EPS = 1e-6
GN_EPS = 1e-5
RET_CHUNK = 256
LAT_HEADS = 2
LAT_UNROLL = 4
HALO = 16
CONV_COLS = 256
PROJ_CONV_ROWS = 64
MIXER_PARTS = 4
FFN_COLS = 256
RING_SLOTS = 3
FFN_AHEAD = 3
VMEM_LIMIT = 56 * 1024 * 1024


def _sigmoid(x):
    return 1.0 / (1.0 + jnp.exp(-x))


def _rms(x, w):
    ms = jnp.mean(x * x, axis=-1, keepdims=True)
    return (x * lax.rsqrt(ms + EPS)) * w


def _params(sem):
    return pltpu.CompilerParams(dimension_semantics=sem, vmem_limit_bytes=VMEM_LIMIT)


def _resident(shape):
    nd = len(shape)
    return pl.BlockSpec(shape, lambda *_: (0,) * nd, pipeline_mode=pl.Buffered(1))


def _adaln_kernel(cond_ref, w_ref, b_ref, win_ref, o_ref, winb_ref):
    c = cond_ref[...]
    s = (c * _sigmoid(c)).astype(BF16)
    o_ref[0] = jnp.dot(s, w_ref[...].astype(BF16), preferred_element_type=F32) + b_ref[0]
    winb_ref[...] = win_ref[...].astype(BF16)


def _adaln(cond8, ada_w, ada_b, w_in):
    n_mod = ada_w.shape[1] // D_MODEL
    steps = w_in.shape[1] // D_MODEL
    assert steps >= n_mod
    last = n_mod - 1
    return pl.pallas_call(
        _adaln_kernel,
        grid=(steps,),
        in_specs=[pl.BlockSpec((8, D_MODEL), lambda j: (0, 0)),
                  pl.BlockSpec((D_MODEL, D_MODEL), lambda j: (0, jnp.minimum(j, last))),
                  pl.BlockSpec((1, 1, D_MODEL), lambda j: (jnp.minimum(j, last), 0, 0)),
                  pl.BlockSpec((D_MODEL, D_MODEL), lambda j: (0, j))],
        out_specs=[pl.BlockSpec((1, 8, D_MODEL), lambda j: (jnp.minimum(j, last), 0, 0)),
                   pl.BlockSpec((D_MODEL, D_MODEL), lambda j: (0, j))],
        out_shape=[jax.ShapeDtypeStruct((n_mod, 8, D_MODEL), F32), jax.ShapeDtypeStruct(w_in.shape, BF16)],
        compiler_params=_params(("arbitrary",)),
        name="adaln",
    )(cond8, ada_w, ada_b.reshape(n_mod, 1, D_MODEL), w_in)


def _decay_kernel(df_ref, db_ref, d_ref, kf_ref, kb_ref, qf_ref, qb_ref, cf_ref, cb_ref):
    def log_sigmoid(x):
        return -(jnp.maximum(-x, 0.0) + jnp.log1p(jnp.exp(-jnp.abs(x))))

    lgf = log_sigmoid(df_ref[...])
    lgb = log_sigmoid(db_ref[...])
    c = RET_CHUNK
    row = lax.broadcasted_iota(jnp.int32, (c, 128), 0).astype(F32)
    col = lax.broadcasted_iota(jnp.int32, (c, 128), 1).astype(F32)
    for h in range(N_HEADS):
        f = lgf[h:h + 1, :]
        b = lgb[h:h + 1, :]
        for half in range(2):
            sl = slice(half * 128, (half + 1) * 128)
            rel = row - (col + 128.0 * half)
            fwd = jnp.exp(jnp.maximum(rel, 0.0) * f)
            bwd = jnp.exp(jnp.maximum(-rel, 0.0) * b)
            d_ref[h, :, sl] = jnp.where(rel > 0, fwd, jnp.where(rel < 0, bwd, 2.0))
            kf_ref[h, :, sl] = jnp.exp((c - 1.0 - row) * f)
            kb_ref[h, :, sl] = jnp.exp(row * b)
            qf_ref[h, :, sl] = jnp.exp((row + 1.0) * f)
            qb_ref[h, :, sl] = jnp.exp((c - row) * b)
            cf_ref[h, :, sl] = jnp.exp(jnp.broadcast_to(c * f, (8, 128)))
            cb_ref[h, :, sl] = jnp.exp(jnp.broadcast_to(c * b, (8, 128)))


def _decay_tables(dec_f, dec_b):
    def lanes(d):
        return jnp.zeros((8, 128), F32).at[:N_HEADS].set(jnp.broadcast_to(d[:, None], (N_HEADS, 128)))

    big = jax.ShapeDtypeStruct((N_HEADS, RET_CHUNK, D_HEAD), F32)
    small = jax.ShapeDtypeStruct((N_HEADS, 8, D_HEAD), F32)
    return pl.pallas_call(
        _decay_kernel,
        out_shape=(big, big, big, big, big, small, small),
        name="decay_tables",
    )(lanes(dec_f), lanes(dec_b))


def _rope_tables(seq):
    rows = seq // GRID_W
    nf = D_HEAD // 4
    r = np.repeat(np.arange(rows), GRID_W).astype(np.float64)
    c = np.tile(np.arange(GRID_W), rows).astype(np.float64)
    inv = ROPE_BASE ** (-np.arange(nf, dtype=np.float64) / nf)
    ar, ac = r[:, None] * inv[None, :], c[:, None] * inv[None, :]
    cos = np.concatenate([np.cos(ar), np.cos(ar), np.cos(ac), np.cos(ac)], axis=1)
    sin = np.concatenate([-np.sin(ar), np.sin(ar), -np.sin(ac), np.sin(ac)], axis=1)
    return jnp.asarray(cos, F32), jnp.asarray(sin, F32)


def _proj_kernel(*refs, latent, tm, tiles_per_seq, n_later=0):
    it = iter(refs)
    x_ref = next(it)
    xp_ref, xn_ref = (next(it), next(it)) if latent else (None, None)
    mod_ref, n1_ref, w_ref, cw_ref, cb_ref, lnw_ref, lnb_ref = (next(it) for _ in range(7))
    if latent:
        cos_ref, sin_ref = next(it), next(it)
        q_ref, k_ref, v_ref, sg_ref, cu_ref, gr_ref, gc_ref = (next(it) for _ in range(7))
    else:
        gnw_ref, d_ref, kf_ref, kb_ref = (next(it) for _ in range(4))
        later_f32 = [next(it) for _ in range(n_later)]
        ro_ref, cu_ref, gr_ref, gc_ref, sf_ref, sb_ref = (next(it) for _ in range(6))
        for src, dst in zip(later_f32, [next(it) for _ in range(n_later)]):
            dst[...] = src[...].astype(BF16)
    pad_ref, uc_ref = it

    i = pl.program_id(0)
    r = (1 + i // tiles_per_seq) if latent else 0
    sh1 = mod_ref[0, pl.ds(r, 1), :]
    sc1 = mod_ref[1, pl.ds(r, 1), :]

    def norm(x):
        return (_rms(x, n1_ref[...]) * (1.0 + sc1) + sh1).astype(BF16)

    hb = norm(x_ref[...])
    n_pad = tm + 2 * HALO
    if latent:
        he = jnp.concatenate([norm(xp_ref[...]), hb, norm(xn_ref[...])], axis=0)
        row = lax.broadcasted_iota(jnp.int32, (n_pad, 1), 0)
        first = (i % tiles_per_seq) == 0
        last = (i % tiles_per_seq) == tiles_per_seq - 1
        outside = (first & (row < HALO)) | (last & (row >= HALO + tm))
    else:
        he = hb

    def proj(lhs, col, width=D_MODEL):
        return jnp.dot(lhs, w_ref[:, col:col + width], preferred_element_type=F32)

    off = HALO - CONV_WIDTH // 2
    n_sub = PROJ_CONV_ROWS // 8

    def conv_chunk(c):
        cs = slice(c * CONV_COLS, (c + 1) * CONV_COLS)
        u = (proj(he, 4 * D_MODEL + c * CONV_COLS, CONV_COLS)
             * _sigmoid(proj(he, 5 * D_MODEL + c * CONV_COLS, CONV_COLS)))
        if latent:
            pad_ref[c, 0] = jnp.where(outside, 0.0, u)
        else:
            pad_ref[c, 0, 0:HALO, :] = jnp.zeros((HALO, CONV_COLS), F32)
            pad_ref[c, 0, HALO:HALO + tm, :] = u
            pad_ref[c, 0, HALO + tm:, :] = jnp.zeros((HALO, CONV_COLS), F32)
        padded = pad_ref[c, 0]
        for s in range(1, 8):
            pad_ref[c, s] = pltpu.roll(padded, n_pad - s, axis=0)
        for rb in range(tm // PROJ_CONV_ROWS):
            base = rb * PROJ_CONV_ROWS
            acc = [cb_ref[:, cs]] * n_sub
            for j in range(CONV_WIDTH):
                s = j + off
                w = cw_ref[j, :, cs]
                for a in range(n_sub):
                    lo = base + (s // 8 + a) * 8
                    acc[a] = acc[a] + pad_ref[c, s % 8, lo:lo + 8, :] * w
            for a in range(n_sub):
                uc_ref[base + a * 8:base + (a + 1) * 8, cs] = acc[a]

    def conv_norm():
        uc = uc_ref[...]
        mu = jnp.mean(uc, axis=-1, keepdims=True)
        d = uc - mu
        var = jnp.mean(d * d, axis=-1, keepdims=True)
        ln = (d * lax.rsqrt(var + GN_EPS)) * lnw_ref[...] + lnb_ref[...]
        cu_ref[...] = (ln * _sigmoid(ln)).astype(BF16)

    def rotate(a):
        slabs = []
        for j in range(D_MODEL // 128):
            slab = a[:, j * 128:(j + 1) * 128]
            tsl = slice((j % 2) * 128, (j % 2 + 1) * 128)
            slabs.append(slab * cos_ref[:, tsl] + pltpu.roll(slab, 64, axis=1) * sin_ref[:, tsl])
        return jnp.concatenate(slabs, axis=1)

    def head(h):
        sl = slice(h * D_HEAD, (h + 1) * D_HEAD)
        qh, kh, vh = q[:, sl], k[:, sl], v[:, sl]
        p = (_dot_nt(qh, kh) * d_ref[h]).astype(BF16)
        o = jnp.dot(p, vh, preferred_element_type=F32)
        ro_ref[:, sl] = _gn_gate(o, gnw_ref[:, sl], sg[:, sl])
        kf = kh.astype(F32)
        sf_ref[0, 0, h] = _dot_tn((kf * kf_ref[h]).astype(BF16), vh)
        sb_ref[0, 0, h] = _dot_tn((kf * kb_ref[h]).astype(BF16), vh)

    def gate_c():
        a_gc = proj(hb, 7 * D_MODEL)
        gc_ref[...] = _sigmoid(a_gc).astype(BF16)

    conv_chunk(0)
    conv_chunk(1)
    a_q = proj(hb, 0)
    q = ((rotate(a_q) if latent else a_q) * D_HEAD ** -0.5).astype(BF16)
    conv_chunk(2)
    a_k = proj(hb, D_MODEL)
    k = (rotate(a_k) if latent else a_k).astype(BF16)
    a_v = proj(hb, 2 * D_MODEL)
    v = a_v.astype(BF16)
    a_g = proj(hb, 3 * D_MODEL)
    sg = a_g * _sigmoid(a_g)
    if latent:
        q_ref[...] = q
        k_ref[...] = k
        v_ref[...] = v
        sg_ref[...] = sg.astype(BF16)
    else:
        for h in range(N_HEADS):
            head(h)
    a_gr = proj(hb, 6 * D_MODEL)
    gr_ref[...] = _sigmoid(a_gr).astype(BF16)
    if latent:
        gate_c()
        conv_chunk(3)
    else:
        conv_chunk(3)
        gate_c()
    conv_norm()


def _proj(x, mod, n1_pre, w_in, cw, cb, lnw, lnb, *, latent, seq, tm, gn_w=None, tabs=None, later=()):
    t = x.shape[0]
    tiles_per_seq = seq // tm
    hb = tm // HALO
    n_hb = t // HALO
    tile = pl.BlockSpec((tm, D_MODEL), lambda i: (i, 0))
    vec = _resident((1, D_MODEL))
    out = jax.ShapeDtypeStruct((t, D_MODEL), BF16)
    in_specs, args = [tile], [x]
    if latent:
        in_specs += [pl.BlockSpec((HALO, D_MODEL), lambda i: (jnp.maximum(i * hb - 1, 0), 0)),
                     pl.BlockSpec((HALO, D_MODEL), lambda i: (jnp.minimum((i + 1) * hb, n_hb - 1), 0))]
        args += [x, x]
    in_specs += [_resident(mod.shape), vec, _resident(w_in.shape), _resident(cw.shape), _resident(cb.shape), vec, vec]
    args += [mod, n1_pre, w_in, cw, cb, lnw, lnb]
    if latent:
        cos, sin = _rope_tables(seq)
        tab = pl.BlockSpec((tm, D_HEAD), lambda i: (i % tiles_per_seq, 0))
        in_specs += [tab, tab]
        args += [cos, sin]
        out_specs, out_shape = [tile] * 7, [out] * 7
    else:
        assert tiles_per_seq == 1 and tm == RET_CHUNK
        d, kf, kb = tabs[0], tabs[1], tabs[2]
        in_specs += [vec, _resident(d.shape), _resident(kf.shape), _resident(kb.shape)]
        args += [gn_w, d, kf, kb]
        st = pl.BlockSpec((1, 1, N_HEADS, D_HEAD, D_HEAD), lambda i: (i, 0, 0, 0, 0))
        st_shape = jax.ShapeDtypeStruct((t // tm, 1, N_HEADS, D_HEAD, D_HEAD), F32)
        out_specs, out_shape = [tile] * 4 + [st, st], [out] * 4 + [st_shape, st_shape]
        steps = t // tm
        for w in later:
            rows, cols = w.shape
            n = max(n for n in range(1, steps + 1) if rows % n == 0 and (rows // n) % HALO == 0)
            blk = pl.BlockSpec((rows // n, cols), lambda i, n=n: (jnp.minimum(i, n - 1), 0))
            in_specs.append(blk)
            args.append(w)
            out_specs.append(blk)
            out_shape.append(jax.ShapeDtypeStruct(w.shape, BF16))
    return pl.pallas_call(
        functools.partial(_proj_kernel, latent=latent, tm=tm, tiles_per_seq=tiles_per_seq, n_later=len(later)),
        grid=(t // tm,),
        in_specs=in_specs,
        out_specs=out_specs,
        out_shape=out_shape,
        scratch_shapes=[pltpu.VMEM((D_MODEL // CONV_COLS, 8, tm + 2 * HALO, CONV_COLS), F32),
                        pltpu.VMEM((tm, D_MODEL), F32)],
        compiler_params=_params(("arbitrary",)),
        name="proj_lat" if latent else "proj_ctx",
    )(*args)


def _dot_nt(a, b):
    return lax.dot_general(a, b, (((1,), (1,)), ((), ())), preferred_element_type=F32)


def _dot_tn(a, b):
    return lax.dot_general(a, b, (((0,), (0,)), ((), ())), preferred_element_type=F32)


def _gn_gate(o, gnw, sg):
    mu = jnp.mean(o, axis=-1, keepdims=True)
    d = o - mu
    var = jnp.mean(d * d, axis=-1, keepdims=True)
    y = (d * lax.rsqrt(var + GN_EPS)) * gnw
    return (sg.astype(F32) * y).astype(BF16)


def _ret_lat_kernel(q_ref, k_ref, v_ref, sg_ref, gnw_ref, sf0_ref, sb0_ref,
                    d_ref, kf_ref, kb_ref, qf_ref, qb_ref, cf_ref, cb_ref,
                    ro_ref, sbs_ref, *, n_chunks):
    c = RET_CHUNK

    def rows(j):
        return pl.ds(pl.multiple_of(j * c, c), c)

    def cols(h):
        return slice(h * D_HEAD, (h + 1) * D_HEAD)

    def bwd(jj, states):
        j = n_chunks - 1 - jj
        new = []
        for h, s in enumerate(states):
            sbs_ref[h, j] = s
            kb = (k_ref[rows(j), cols(h)].astype(F32) * kb_ref[h]).astype(BF16)
            new.append(s * cb_ref[h, 0:1, :] + _dot_tn(kb, v_ref[rows(j), cols(h)]))
        return tuple(new)

    lax.fori_loop(0, n_chunks, bwd, tuple(sb0_ref[0, 0, h] for h in range(LAT_HEADS)), unroll=LAT_UNROLL)

    def fwd(j, states):
        new = []
        for h, s in enumerate(states):
            q, k, v = q_ref[rows(j), cols(h)], k_ref[rows(j), cols(h)], v_ref[rows(j), cols(h)]
            qf32, kf32 = q.astype(F32), k.astype(F32)
            p = (_dot_nt(q, k) * d_ref[h]).astype(BF16)
            o = jnp.dot(p, v, preferred_element_type=F32)
            o = o + jnp.dot((qf32 * qf_ref[h]).astype(BF16), s.astype(BF16), preferred_element_type=F32)
            o = o + jnp.dot((qf32 * qb_ref[h]).astype(BF16), sbs_ref[h, j].astype(BF16),
                            preferred_element_type=F32)
            ro_ref[rows(j), cols(h)] = _gn_gate(o, gnw_ref[:, cols(h)], sg_ref[rows(j), cols(h)])
            new.append(s * cf_ref[h, 0:1, :] + _dot_tn((kf32 * kf_ref[h]).astype(BF16), v))
        return tuple(new)

    lax.fori_loop(0, n_chunks, fwd, tuple(sf0_ref[0, 0, h] for h in range(LAT_HEADS)), unroll=LAT_UNROLL)


def _ret_lat(q, k, v, sg, gn_w, sf0, sb0, tabs, batch, seq):
    n_chunks = seq // RET_CHUNK
    w = LAT_HEADS * D_HEAD
    tile = pl.BlockSpec((seq, w), lambda s, h: (s, h))
    st = pl.BlockSpec((1, 1, LAT_HEADS, D_HEAD, D_HEAD), lambda s, h: (s, 0, h, 0, 0))
    big = pl.BlockSpec((LAT_HEADS, RET_CHUNK, D_HEAD), lambda s, h: (h, 0, 0))
    small = pl.BlockSpec((LAT_HEADS, 8, D_HEAD), lambda s, h: (h, 0, 0))
    return pl.pallas_call(
        functools.partial(_ret_lat_kernel, n_chunks=n_chunks),
        grid=(batch, N_HEADS // LAT_HEADS),
        in_specs=[tile, tile, tile, tile, pl.BlockSpec((1, w), lambda s, h: (0, h)),
                  st, st, big, big, big, big, big, small, small],
        out_specs=tile,
        out_shape=jax.ShapeDtypeStruct((batch * seq, D_MODEL), BF16),
        scratch_shapes=[pltpu.VMEM((LAT_HEADS, n_chunks, D_HEAD, D_HEAD), F32)],
        compiler_params=_params(("arbitrary", "arbitrary")),
        name="ret_lat",
    )(q, k, v, sg, gn_w, sf0, sb0, *tabs)


def _mixer_kernel(x_ref, ro_ref, cu_ref, gr_ref, gc_ref, mod_ref, wro_ref, wco_ref, wmx_ref, n1p_ref, n2_ref,
                  x1_ref, h2_ref, *, tiles_per_seq, per_seq_mod):
    r = (1 + pl.program_id(0) // tiles_per_seq) if per_seq_mod else 0
    g1 = mod_ref[2, pl.ds(r, 1), :]
    sh2 = mod_ref[3, pl.ds(r, 1), :]
    sc2 = mod_ref[4, pl.ds(r, 1), :]
    part = x_ref.shape[0] // MIXER_PARTS
    rows = [slice(p * part, (p + 1) * part) for p in range(MIXER_PARTS)]
    outs = [(jnp.dot(cu_ref[rs, :], wco_ref[...], preferred_element_type=F32),
             jnp.dot(ro_ref[rs, :], wro_ref[...], preferred_element_type=F32)) for rs in rows]
    ms = []
    for rs, (conv_out, ret_out) in zip(rows, outs):
        mixed = gr_ref[rs, :].astype(F32) * ret_out + gc_ref[rs, :].astype(F32) * conv_out
        ms.append(jnp.dot(mixed.astype(BF16), wmx_ref[...], preferred_element_type=F32))
    for rs, m in zip(rows, ms):
        x1 = x_ref[rs, :] + g1 * _rms(m, n1p_ref[...])
        x1_ref[rs, :] = x1
        h2_ref[rs, :] = (_rms(x1, n2_ref[...]) * (1.0 + sc2) + sh2).astype(BF16)


def _mixer(x, ro, cu, gr, gc, mod, wro, wco, wmx, n1p, n2, *, seq, tm, per_seq_mod):
    t = x.shape[0]
    tile = pl.BlockSpec((tm, D_MODEL), lambda i: (i, 0))
    vec = _resident((1, D_MODEL))
    mat = _resident((D_MODEL, D_MODEL))
    return pl.pallas_call(
        functools.partial(_mixer_kernel, tiles_per_seq=max(seq // tm, 1), per_seq_mod=per_seq_mod),
        grid=(t // tm,),
        in_specs=[tile, tile, tile, tile, tile, _resident(mod.shape), mat, mat, mat, vec, vec],
        out_specs=[tile, tile],
        out_shape=[jax.ShapeDtypeStruct((t, D_MODEL), F32), jax.ShapeDtypeStruct((t, D_MODEL), BF16)],
        compiler_params=_params(("arbitrary",)),
        name="mixer_lat" if per_seq_mod else "mixer_ctx",
    )(x, ro, cu, gr, gc, mod, wro, wco, wmx, n1p, n2)


def _ffn_kernel(x1_ref, h2_ref, hp_ref, hn_ref, mod_ref, wup_ref, fw_ref, fb_ref, wdn_ref, n2p_ref,
                y_ref, *ring, tm, tiles_per_seq, per_seq_mod):
    i = pl.program_id(0)
    if ring:
        x1_hbm, h2_hbm = x1_ref, h2_ref
        x1_buf, h2_buf, sem = ring
        n_steps = pl.num_programs(0)

        def copies(step, slot):
            rows = pl.ds(pl.multiple_of(step * tm, tm), tm)
            return (pltpu.make_async_copy(x1_hbm.at[rows], x1_buf.at[slot], sem.at[0, slot]),
                    pltpu.make_async_copy(h2_hbm.at[rows], h2_buf.at[slot], sem.at[1, slot]))

        @pl.when(i == 0)
        def _():
            for s in range(RING_SLOTS - 1):
                for cp in copies(s, s):
                    cp.start()

        @pl.when(i + (RING_SLOTS - 1) < n_steps)
        def _():
            nxt_step = i + (RING_SLOTS - 1)
            for cp in copies(nxt_step, nxt_step % RING_SLOTS):
                cp.start()

        slot = i % RING_SLOTS
        for cp in copies(i, slot):
            cp.wait()
        x1_ref, h2_ref = x1_buf.at[slot], h2_buf.at[slot]
    r = (1 + i // tiles_per_seq) if per_seq_mod else 0
    g2 = mod_ref[5, pl.ds(r, 1), :]
    first = (i % tiles_per_seq) == 0
    last = (i % tiles_per_seq) == tiles_per_seq - 1

    whole_seq = tiles_per_seq == 1
    if whole_seq:
        n_ext = tm
        he = h2_ref[...]
        sub = lax.broadcasted_iota(jnp.int32, (8, 1), 0)
    else:
        n_ext = tm + HALO
        row = lax.broadcasted_iota(jnp.int32, (HALO, 1), 0)
        zero = jnp.zeros((), BF16)
        halo = jnp.where(row < HALO // 2, jnp.where(last, zero, hn_ref[...]), jnp.where(first, zero, hp_ref[...]))
        he = jnp.concatenate([h2_ref[...], halo], axis=0)

    inner = slice(0, tm)
    n_chunks = D_FF // FFN_COLS

    def cols(cc, half):
        return slice(half * D_FF + cc * FFN_COLS, half * D_FF + (cc + 1) * FFN_COLS)

    def up_proj(cc):
        return [jnp.dot(he, wup_ref[:, cols(cc, half)], preferred_element_type=F32) for half in range(2)]

    def conv3(up, cs):
        before = pltpu.roll(up, 1, axis=0)[inner]
        after = pltpu.roll(up, n_ext - 1, axis=0)[inner]
        if whole_seq:
            before = jnp.concatenate([jnp.where(sub == 0, 0.0, before[0:8]), before[8:]], axis=0)
            after = jnp.concatenate([after[:tm - 8], jnp.where(sub == 7, 0.0, after[tm - 8:])], axis=0)
        return (fb_ref[:, cs] + before * fw_ref[0:1, cs] + up[inner] * fw_ref[1:2, cs]
                + after * fw_ref[2:3, cs])

    f = None
    ups = [up_proj(cc) for cc in range(FFN_AHEAD)]
    for cc in range(n_chunks):
        up = ups[cc]
        if cc + FFN_AHEAD < n_chunks:
            ups.append(up_proj(cc + FFN_AHEAD))
        a, gl = conv3(up[0], cols(cc, 0)), conv3(up[1], cols(cc, 1))
        act = ((a * _sigmoid(a)) * gl).astype(BF16)
        part = jnp.dot(act, wdn_ref[cc * FFN_COLS:(cc + 1) * FFN_COLS, :], preferred_element_type=F32)
        f = part if f is None else f + part
    y_ref[...] = x1_ref[...] + g2 * _rms(f, n2p_ref[...])


def _ffn(x1, h2, mod, wup, fw, fb, wdn, n2p, *, seq, tm, per_seq_mod):
    t = x1.shape[0]
    tiles_per_seq = seq // tm
    hb = tm // HALO
    n_hb = t // HALO
    tile = pl.BlockSpec((tm, D_MODEL), lambda i: (i, 0))
    prev = pl.BlockSpec((HALO, D_MODEL), lambda i: (jnp.maximum(i * hb - 1, 0), 0))
    nxt = pl.BlockSpec((HALO, D_MODEL), lambda i: (jnp.minimum((i + 1) * hb, n_hb - 1), 0))
    manual = tiles_per_seq == 1 and t // tm >= RING_SLOTS
    stream = pl.BlockSpec(memory_space=pl.ANY) if manual else tile
    ring = [pltpu.VMEM((RING_SLOTS, tm, D_MODEL), F32), pltpu.VMEM((RING_SLOTS, tm, D_MODEL), BF16),
            pltpu.SemaphoreType.DMA((2, RING_SLOTS))] if manual else []
    return pl.pallas_call(
        functools.partial(_ffn_kernel, tm=tm, tiles_per_seq=tiles_per_seq, per_seq_mod=per_seq_mod),
        grid=(t // tm,),
        in_specs=[stream, stream, prev, nxt, _resident(mod.shape), _resident(wup.shape),
                  _resident(fw.shape), _resident(fb.shape), _resident(wdn.shape), _resident((1, D_MODEL))],
        out_specs=tile,
        out_shape=jax.ShapeDtypeStruct((t, D_MODEL), F32),
        scratch_shapes=ring,
        compiler_params=_params(("arbitrary",)),
        name="ffn_lat" if per_seq_mod else "ffn_ctx",
    )(x1, h2, h2, h2, mod, wup, fw, fb, wdn, n2p)


def kernel(x_prompt, x_sample, state_ret_fwd, state_ret_bwd, c, c_ctx, norm1_pre, norm1_post, norm2_pre, norm2_post, ada_w, ada_b, w_in, ret_decay_fwd, ret_decay_bwd, ret_gn_w, ret_w_out, conv_dw_w, conv_dw_b, conv_ln_w, conv_ln_b, conv_w_out, w_mix_out, ffn_w_up, ffn_dw_w, ffn_dw_b, ffn_w_down):
    batch, seq, _ = x_prompt.shape
    dec_batch, dec_seq, _ = x_sample.shape
    assert norm1_pre.shape[0] == 1 and dec_batch + 1 <= 8

    cond8 = jnp.zeros((8, D_MODEL), F32).at[0].set(c_ctx).at[1:1 + dec_batch].set(c)
    mod, w_in_b = _adaln(cond8, ada_w[0], ada_b[0], w_in[0])
    tabs = _decay_tables(ret_decay_fwd[0], ret_decay_bwd[0])

    cw = jnp.broadcast_to(conv_dw_w[0][:, None, :], (CONV_WIDTH, 8, D_MODEL))
    cb = jnp.broadcast_to(conv_dw_b, (8, D_MODEL))
    fw = jnp.zeros((8, 2 * D_FF), F32).at[:FFN_CONV_WIDTH].set(ffn_dw_w[0])
    fb = ffn_dw_b
    conv = (cw, cb, conv_ln_w, conv_ln_b)

    xp = x_prompt.reshape(batch * seq, D_MODEL)
    later = (ret_w_out[0], conv_w_out[0], w_mix_out[0], ffn_w_up[0], ffn_w_down[0])
    ro, cu, gr, gc, sf, sb, wro, wco, wmx, wup, wdn = _proj(
        xp, mod, norm1_pre, w_in_b, *conv, latent=False, seq=seq, tm=seq, gn_w=ret_gn_w, tabs=tabs, later=later)

    def tail(x, ro, cu, gr, gc, *, latent, seq_len):
        x1, h2 = _mixer(x, ro, cu, gr, gc, mod, wro, wco, wmx, norm1_post, norm2_pre,
                        seq=seq_len, tm=512, per_seq_mod=latent)
        return _ffn(x1, h2, mod, wup, fw, fb, wdn, norm2_post, seq=seq_len, tm=256, per_seq_mod=latent)

    yp = tail(xp, ro, cu, gr, gc, latent=False, seq_len=seq)

    xs = x_sample.reshape(dec_batch * dec_seq, D_MODEL)
    q, k, v, sg, cu, gr, gc = _proj(xs, mod, norm1_pre, w_in_b, *conv, latent=True, seq=dec_seq, tm=256)
    ro = _ret_lat(q, k, v, sg, ret_gn_w, state_ret_fwd, state_ret_bwd, tabs, dec_batch, dec_seq)
    ys = tail(xs, ro, cu, gr, gc, latent=True, seq_len=dec_seq)
    return (yp.reshape(batch, seq, D_MODEL), ys.reshape(dec_batch, dec_seq, D_MODEL), sf, sb)
```

```python
import functools

import numpy as np
import jax
import jax.numpy as jnp
from jax import lax
from jax.experimental import pallas as pl
from jax.experimental.pallas import tpu as pltpu

F32 = jnp.float32
BF16 = jnp.bfloat16

D_MODEL = 1024
N_HEADS = 4
D_HEAD = 256
D_FF = 2816
CONV_WIDTH = 31
FFN_CONV_WIDTH = 3
GRID_W = 64
ROPE_BASE = 10000.0
EPS = 1e-6
GN_EPS = 1e-5
RET_CHUNK = 256
LAT_HEADS = 2
LAT_UNROLL = 4
HALO = 16
CONV_COLS = 256
PROJ_CONV_ROWS = 64
MIXER_PARTS = 4
FFN_COLS = 256
FFN_SEQS = 2
FFN_AHEAD = 3
VMEM_LIMIT = 56 * 1024 * 1024


def _sigmoid(x):
    return 1.0 / (1.0 + jnp.exp(-x))


def _rms(x, w):
    ms = jnp.mean(x * x, axis=-1, keepdims=True)
    return (x * lax.rsqrt(ms + EPS)) * w


def _params(sem):
    return pltpu.CompilerParams(dimension_semantics=sem, vmem_limit_bytes=VMEM_LIMIT)


def _resident(shape):
    nd = len(shape)
    return pl.BlockSpec(shape, lambda *_: (0,) * nd, pipeline_mode=pl.Buffered(1))


def _adaln_kernel(cond_ref, w_ref, b_ref, win_ref, o_ref, winb_ref):
    c = cond_ref[...]
    s = (c * _sigmoid(c)).astype(BF16)
    o_ref[0] = jnp.dot(s, w_ref[...].astype(BF16), preferred_element_type=F32) + b_ref[0]
    winb_ref[...] = win_ref[...].astype(BF16)


def _adaln(cond8, ada_w, ada_b, w_in):
    n_mod = ada_w.shape[1] // D_MODEL
    steps = w_in.shape[1] // D_MODEL
    assert steps >= n_mod
    last = n_mod - 1
    return pl.pallas_call(
        _adaln_kernel,
        grid=(steps,),
        in_specs=[pl.BlockSpec((8, D_MODEL), lambda j: (0, 0)),
                  pl.BlockSpec((D_MODEL, D_MODEL), lambda j: (0, jnp.minimum(j, last))),
                  pl.BlockSpec((1, 1, D_MODEL), lambda j: (jnp.minimum(j, last), 0, 0)),
                  pl.BlockSpec((D_MODEL, D_MODEL), lambda j: (0, j))],
        out_specs=[pl.BlockSpec((1, 8, D_MODEL), lambda j: (jnp.minimum(j, last), 0, 0)),
                   pl.BlockSpec((D_MODEL, D_MODEL), lambda j: (0, j))],
        out_shape=[jax.ShapeDtypeStruct((n_mod, 8, D_MODEL), F32), jax.ShapeDtypeStruct(w_in.shape, BF16)],
        compiler_params=_params(("arbitrary",)),
        name="adaln",
    )(cond8, ada_w, ada_b.reshape(n_mod, 1, D_MODEL), w_in)


def _decay_kernel(df_ref, db_ref, d_ref, kf_ref, kb_ref, qf_ref, qb_ref, cf_ref, cb_ref):
    def log_sigmoid(x):
        return -(jnp.maximum(-x, 0.0) + jnp.log1p(jnp.exp(-jnp.abs(x))))

    lgf = log_sigmoid(df_ref[...])
    lgb = log_sigmoid(db_ref[...])
    c = RET_CHUNK
    row = lax.broadcasted_iota(jnp.int32, (c, 128), 0).astype(F32)
    col = lax.broadcasted_iota(jnp.int32, (c, 128), 1).astype(F32)
    for h in range(N_HEADS):
        f = lgf[h:h + 1, :]
        b = lgb[h:h + 1, :]
        for half in range(2):
            sl = slice(half * 128, (half + 1) * 128)
            rel = row - (col + 128.0 * half)
            fwd = jnp.exp(jnp.maximum(rel, 0.0) * f)
            bwd = jnp.exp(jnp.maximum(-rel, 0.0) * b)
            d_ref[h, :, sl] = jnp.where(rel > 0, fwd, jnp.where(rel < 0, bwd, 2.0))
            kf_ref[h, :, sl] = jnp.exp((c - 1.0 - row) * f)
            kb_ref[h, :, sl] = jnp.exp(row * b)
            qf_ref[h, :, sl] = jnp.exp((row + 1.0) * f)
            qb_ref[h, :, sl] = jnp.exp((c - row) * b)
            cf_ref[h, :, sl] = jnp.exp(jnp.broadcast_to(c * f, (8, 128)))
            cb_ref[h, :, sl] = jnp.exp(jnp.broadcast_to(c * b, (8, 128)))


def _decay_tables(dec_f, dec_b):
    def lanes(d):
        return jnp.zeros((8, 128), F32).at[:N_HEADS].set(jnp.broadcast_to(d[:, None], (N_HEADS, 128)))

    big = jax.ShapeDtypeStruct((N_HEADS, RET_CHUNK, D_HEAD), F32)
    small = jax.ShapeDtypeStruct((N_HEADS, 8, D_HEAD), F32)
    return pl.pallas_call(
        _decay_kernel,
        out_shape=(big, big, big, big, big, small, small),
        name="decay_tables",
    )(lanes(dec_f), lanes(dec_b))


def _rope_tables(seq):
    rows = seq // GRID_W
    nf = D_HEAD // 4
    r = np.repeat(np.arange(rows), GRID_W).astype(np.float64)
    c = np.tile(np.arange(GRID_W), rows).astype(np.float64)
    inv = ROPE_BASE ** (-np.arange(nf, dtype=np.float64) / nf)
    ar, ac = r[:, None] * inv[None, :], c[:, None] * inv[None, :]
    cos = np.concatenate([np.cos(ar), np.cos(ar), np.cos(ac), np.cos(ac)], axis=1)
    sin = np.concatenate([-np.sin(ar), np.sin(ar), -np.sin(ac), np.sin(ac)], axis=1)
    return jnp.asarray(cos, F32), jnp.asarray(sin, F32)


def _proj_kernel(*refs, latent, tm, tiles_per_seq, n_later=0):
    it = iter(refs)
    x_ref = next(it)
    xp_ref, xn_ref = (next(it), next(it)) if latent else (None, None)
    mod_ref, n1_ref, w_ref, cw_ref, cb_ref, lnw_ref, lnb_ref = (next(it) for _ in range(7))
    if latent:
        cos_ref, sin_ref = next(it), next(it)
        q_ref, k_ref, v_ref, sg_ref, cu_ref, gr_ref, gc_ref = (next(it) for _ in range(7))
    else:
        gnw_ref, d_ref, kf_ref, kb_ref = (next(it) for _ in range(4))
        later_f32 = [next(it) for _ in range(n_later)]
        ro_ref, cu_ref, gr_ref, gc_ref, sf_ref, sb_ref = (next(it) for _ in range(6))
        for src, dst in zip(later_f32, [next(it) for _ in range(n_later)]):
            dst[...] = src[...].astype(BF16)
    pad_ref, uc_ref = it

    i = pl.program_id(0)
    r = (1 + i // tiles_per_seq) if latent else 0
    sh1 = mod_ref[0, pl.ds(r, 1), :]
    sc1 = mod_ref[1, pl.ds(r, 1), :]

    def norm(x):
        return (_rms(x, n1_ref[...]) * (1.0 + sc1) + sh1).astype(BF16)

    hb = norm(x_ref[...])
    n_pad = tm + 2 * HALO
    if latent:
        he = jnp.concatenate([norm(xp_ref[...]), hb, norm(xn_ref[...])], axis=0)
        row = lax.broadcasted_iota(jnp.int32, (n_pad, 1), 0)
        first = (i % tiles_per_seq) == 0
        last = (i % tiles_per_seq) == tiles_per_seq - 1
        outside = (first & (row < HALO)) | (last & (row >= HALO + tm))
    else:
        he = hb

    def proj(lhs, col, width=D_MODEL):
        return jnp.dot(lhs, w_ref[:, col:col + width], preferred_element_type=F32)

    off = HALO - CONV_WIDTH // 2
    n_sub = PROJ_CONV_ROWS // 8

    def conv_chunk(c):
        cs = slice(c * CONV_COLS, (c + 1) * CONV_COLS)
        u = (proj(he, 4 * D_MODEL + c * CONV_COLS, CONV_COLS)
             * _sigmoid(proj(he, 5 * D_MODEL + c * CONV_COLS, CONV_COLS)))
        if latent:
            pad_ref[c, 0] = jnp.where(outside, 0.0, u)
        else:
            pad_ref[c, 0, 0:HALO, :] = jnp.zeros((HALO, CONV_COLS), F32)
            pad_ref[c, 0, HALO:HALO + tm, :] = u
            pad_ref[c, 0, HALO + tm:, :] = jnp.zeros((HALO, CONV_COLS), F32)
        padded = pad_ref[c, 0]
        for s in range(1, 8):
            pad_ref[c, s] = pltpu.roll(padded, n_pad - s, axis=0)
        for rb in range(tm // PROJ_CONV_ROWS):
            base = rb * PROJ_CONV_ROWS
            acc = [cb_ref[:, cs]] * n_sub
            for j in range(CONV_WIDTH):
                s = j + off
                w = cw_ref[j, :, cs]
                for a in range(n_sub):
                    lo = base + (s // 8 + a) * 8
                    acc[a] = acc[a] + pad_ref[c, s % 8, lo:lo + 8, :] * w
            for a in range(n_sub):
                uc_ref[base + a * 8:base + (a + 1) * 8, cs] = acc[a]

    def conv_norm():
        uc = uc_ref[...]
        mu = jnp.mean(uc, axis=-1, keepdims=True)
        d = uc - mu
        var = jnp.mean(d * d, axis=-1, keepdims=True)
        ln = (d * lax.rsqrt(var + GN_EPS)) * lnw_ref[...] + lnb_ref[...]
        cu_ref[...] = (ln * _sigmoid(ln)).astype(BF16)

    def rotate(a):
        slabs = []
        for j in range(D_MODEL // 128):
            slab = a[:, j * 128:(j + 1) * 128]
            tsl = slice((j % 2) * 128, (j % 2 + 1) * 128)
            slabs.append(slab * cos_ref[:, tsl] + pltpu.roll(slab, 64, axis=1) * sin_ref[:, tsl])
        return jnp.concatenate(slabs, axis=1)

    def head(h):
        sl = slice(h * D_HEAD, (h + 1) * D_HEAD)
        qh, kh, vh = q[:, sl], k[:, sl], v[:, sl]
        p = (_dot_nt(qh, kh) * d_ref[h]).astype(BF16)
        o = jnp.dot(p, vh, preferred_element_type=F32)
        ro_ref[:, sl] = _gn_gate(o, gnw_ref[:, sl], sg[:, sl])
        kf = kh.astype(F32)
        sf_ref[0, 0, h] = _dot_tn((kf * kf_ref[h]).astype(BF16), vh)
        sb_ref[0, 0, h] = _dot_tn((kf * kb_ref[h]).astype(BF16), vh)

    def gate_c():
        a_gc = proj(hb, 7 * D_MODEL)
        gc_ref[...] = _sigmoid(a_gc).astype(BF16)

    conv_chunk(0)
    conv_chunk(1)
    a_q = proj(hb, 0)
    q = ((rotate(a_q) if latent else a_q) * D_HEAD ** -0.5).astype(BF16)
    conv_chunk(2)
    a_k = proj(hb, D_MODEL)
    k = (rotate(a_k) if latent else a_k).astype(BF16)
    a_v = proj(hb, 2 * D_MODEL)
    v = a_v.astype(BF16)
    a_g = proj(hb, 3 * D_MODEL)
    sg = a_g * _sigmoid(a_g)
    if latent:
        q_ref[...] = q
        k_ref[...] = k
        v_ref[...] = v
        sg_ref[...] = sg.astype(BF16)
    else:
        for h in range(N_HEADS):
            head(h)
    a_gr = proj(hb, 6 * D_MODEL)
    gr_ref[...] = _sigmoid(a_gr).astype(BF16)
    if latent:
        gate_c()
        conv_chunk(3)
    else:
        conv_chunk(3)
        gate_c()
    conv_norm()


def _proj(x, mod, n1_pre, w_in, cw, cb, lnw, lnb, *, latent, seq, tm, gn_w=None, tabs=None, later=()):
    t = x.shape[0]
    tiles_per_seq = seq // tm
    hb = tm // HALO
    n_hb = t // HALO
    tile = pl.BlockSpec((tm, D_MODEL), lambda i: (i, 0))
    vec = _resident((1, D_MODEL))
    out = jax.ShapeDtypeStruct((t, D_MODEL), BF16)
    in_specs, args = [tile], [x]
    if latent:
        in_specs += [pl.BlockSpec((HALO, D_MODEL), lambda i: (jnp.maximum(i * hb - 1, 0), 0)),
                     pl.BlockSpec((HALO, D_MODEL), lambda i: (jnp.minimum((i + 1) * hb, n_hb - 1), 0))]
        args += [x, x]
    in_specs += [_resident(mod.shape), vec, _resident(w_in.shape), _resident(cw.shape), _resident(cb.shape), vec, vec]
    args += [mod, n1_pre, w_in, cw, cb, lnw, lnb]
    if latent:
        cos, sin = _rope_tables(seq)
        tab = pl.BlockSpec((tm, D_HEAD), lambda i: (i % tiles_per_seq, 0))
        in_specs += [tab, tab]
        args += [cos, sin]
        out_specs, out_shape = [tile] * 7, [out] * 7
    else:
        assert tiles_per_seq == 1 and tm == RET_CHUNK
        d, kf, kb = tabs[0], tabs[1], tabs[2]
        in_specs += [vec, _resident(d.shape), _resident(kf.shape), _resident(kb.shape)]
        args += [gn_w, d, kf, kb]
        st = pl.BlockSpec((1, 1, N_HEADS, D_HEAD, D_HEAD), lambda i: (i, 0, 0, 0, 0))
        st_shape = jax.ShapeDtypeStruct((t // tm, 1, N_HEADS, D_HEAD, D_HEAD), F32)
        out_specs, out_shape = [tile] * 4 + [st, st], [out] * 4 + [st_shape, st_shape]
        steps = t // tm
        for w in later:
            rows, cols = w.shape
            n = max(n for n in range(1, steps + 1) if rows % n == 0 and (rows // n) % HALO == 0)
            blk = pl.BlockSpec((rows // n, cols), lambda i, n=n: (jnp.minimum(i, n - 1), 0))
            in_specs.append(blk)
            args.append(w)
            out_specs.append(blk)
            out_shape.append(jax.ShapeDtypeStruct(w.shape, BF16))
    return pl.pallas_call(
        functools.partial(_proj_kernel, latent=latent, tm=tm, tiles_per_seq=tiles_per_seq, n_later=len(later)),
        grid=(t // tm,),
        in_specs=in_specs,
        out_specs=out_specs,
        out_shape=out_shape,
        scratch_shapes=[pltpu.VMEM((D_MODEL // CONV_COLS, 8, tm + 2 * HALO, CONV_COLS), F32),
                        pltpu.VMEM((tm, D_MODEL), F32)],
        compiler_params=_params(("arbitrary",)),
        name="proj_lat" if latent else "proj_ctx",
    )(*args)


def _dot_nt(a, b):
    return lax.dot_general(a, b, (((1,), (1,)), ((), ())), preferred_element_type=F32)


def _dot_tn(a, b):
    return lax.dot_general(a, b, (((0,), (0,)), ((), ())), preferred_element_type=F32)


def _gn_gate(o, gnw, sg):
    mu = jnp.mean(o, axis=-1, keepdims=True)
    d = o - mu
    var = jnp.mean(d * d, axis=-1, keepdims=True)
    y = (d * lax.rsqrt(var + GN_EPS)) * gnw
    return (sg.astype(F32) * y).astype(BF16)


def _ret_lat_kernel(q_ref, k_ref, v_ref, sg_ref, gnw_ref, sf0_ref, sb0_ref,
                    d_ref, kf_ref, kb_ref, qf_ref, qb_ref, cf_ref, cb_ref,
                    ro_ref, sbs_ref, *, n_chunks):
    c = RET_CHUNK

    def rows(j):
        return pl.ds(pl.multiple_of(j * c, c), c)

    def cols(h):
        return slice(h * D_HEAD, (h + 1) * D_HEAD)

    def bwd(jj, states):
        j = n_chunks - 1 - jj
        new = []
        for h, s in enumerate(states):
            sbs_ref[h, j] = s
            kb = (k_ref[rows(j), cols(h)].astype(F32) * kb_ref[h]).astype(BF16)
            new.append(s * cb_ref[h, 0:1, :] + _dot_tn(kb, v_ref[rows(j), cols(h)]))
        return tuple(new)

    lax.fori_loop(0, n_chunks, bwd, tuple(sb0_ref[0, 0, h] for h in range(LAT_HEADS)), unroll=LAT_UNROLL)

    def fwd(j, states):
        new = []
        for h, s in enumerate(states):
            q, k, v = q_ref[rows(j), cols(h)], k_ref[rows(j), cols(h)], v_ref[rows(j), cols(h)]
            qf32, kf32 = q.astype(F32), k.astype(F32)
            p = (_dot_nt(q, k) * d_ref[h]).astype(BF16)
            o = jnp.dot(p, v, preferred_element_type=F32)
            o = o + jnp.dot((qf32 * qf_ref[h]).astype(BF16), s.astype(BF16), preferred_element_type=F32)
            o = o + jnp.dot((qf32 * qb_ref[h]).astype(BF16), sbs_ref[h, j].astype(BF16),
                            preferred_element_type=F32)
            ro_ref[rows(j), cols(h)] = _gn_gate(o, gnw_ref[:, cols(h)], sg_ref[rows(j), cols(h)])
            new.append(s * cf_ref[h, 0:1, :] + _dot_tn((kf32 * kf_ref[h]).astype(BF16), v))
        return tuple(new)

    lax.fori_loop(0, n_chunks, fwd, tuple(sf0_ref[0, 0, h] for h in range(LAT_HEADS)), unroll=LAT_UNROLL)


def _ret_lat(q, k, v, sg, gn_w, sf0, sb0, tabs, batch, seq):
    n_chunks = seq // RET_CHUNK
    w = LAT_HEADS * D_HEAD
    tile = pl.BlockSpec((seq, w), lambda s, h: (s, h))
    st = pl.BlockSpec((1, 1, LAT_HEADS, D_HEAD, D_HEAD), lambda s, h: (s, 0, h, 0, 0))
    big = pl.BlockSpec((LAT_HEADS, RET_CHUNK, D_HEAD), lambda s, h: (h, 0, 0))
    small = pl.BlockSpec((LAT_HEADS, 8, D_HEAD), lambda s, h: (h, 0, 0))
    return pl.pallas_call(
        functools.partial(_ret_lat_kernel, n_chunks=n_chunks),
        grid=(batch, N_HEADS // LAT_HEADS),
        in_specs=[tile, tile, tile, tile, pl.BlockSpec((1, w), lambda s, h: (0, h)),
                  st, st, big, big, big, big, big, small, small],
        out_specs=tile,
        out_shape=jax.ShapeDtypeStruct((batch * seq, D_MODEL), BF16),
        scratch_shapes=[pltpu.VMEM((LAT_HEADS, n_chunks, D_HEAD, D_HEAD), F32)],
        compiler_params=_params(("arbitrary", "arbitrary")),
        name="ret_lat",
    )(q, k, v, sg, gn_w, sf0, sb0, *tabs)


def _mixer_kernel(x_ref, ro_ref, cu_ref, gr_ref, gc_ref, mod_ref, wro_ref, wco_ref, wmx_ref, n1p_ref, n2_ref,
                  x1_ref, h2_ref, *, tiles_per_seq, per_seq_mod):
    r = (1 + pl.program_id(0) // tiles_per_seq) if per_seq_mod else 0
    g1 = mod_ref[2, pl.ds(r, 1), :]
    sh2 = mod_ref[3, pl.ds(r, 1), :]
    sc2 = mod_ref[4, pl.ds(r, 1), :]
    part = x_ref.shape[0] // MIXER_PARTS
    rows = [slice(p * part, (p + 1) * part) for p in range(MIXER_PARTS)]
    outs = [(jnp.dot(cu_ref[rs, :], wco_ref[...], preferred_element_type=F32),
             jnp.dot(ro_ref[rs, :], wro_ref[...], preferred_element_type=F32)) for rs in rows]
    ms = []
    for rs, (conv_out, ret_out) in zip(rows, outs):
        mixed = gr_ref[rs, :].astype(F32) * ret_out + gc_ref[rs, :].astype(F32) * conv_out
        ms.append(jnp.dot(mixed.astype(BF16), wmx_ref[...], preferred_element_type=F32))
    for rs, m in zip(rows, ms):
        x1 = x_ref[rs, :] + g1 * _rms(m, n1p_ref[...])
        x1_ref[rs, :] = x1
        h2_ref[rs, :] = (_rms(x1, n2_ref[...]) * (1.0 + sc2) + sh2).astype(BF16)


def _mixer(x, ro, cu, gr, gc, mod, wro, wco, wmx, n1p, n2, *, seq, tm, per_seq_mod):
    t = x.shape[0]
    tile = pl.BlockSpec((tm, D_MODEL), lambda i: (i, 0))
    vec = _resident((1, D_MODEL))
    mat = _resident((D_MODEL, D_MODEL))
    return pl.pallas_call(
        functools.partial(_mixer_kernel, tiles_per_seq=max(seq // tm, 1), per_seq_mod=per_seq_mod),
        grid=(t // tm,),
        in_specs=[tile, tile, tile, tile, tile, _resident(mod.shape), mat, mat, mat, vec, vec],
        out_specs=[tile, tile],
        out_shape=[jax.ShapeDtypeStruct((t, D_MODEL), F32), jax.ShapeDtypeStruct((t, D_MODEL), BF16)],
        compiler_params=_params(("arbitrary",)),
        name="mixer_lat" if per_seq_mod else "mixer_ctx",
    )(x, ro, cu, gr, gc, mod, wro, wco, wmx, n1p, n2)


def _ffn_kernel(x1_ref, h2_ref, hp_ref, hn_ref, mod_ref, wup_ref, fw_ref, fb_ref, wdn_ref, n2p_ref,
                y_ref, *, tm, tiles_per_seq, per_seq_mod):
    i = pl.program_id(0)
    r = (1 + i // tiles_per_seq) if per_seq_mod else 0
    g2 = mod_ref[5, pl.ds(r, 1), :]
    first = (i % tiles_per_seq) == 0
    last = (i % tiles_per_seq) == tiles_per_seq - 1

    whole_seq = tiles_per_seq == 1
    if whole_seq:
        n_ext = tm
        he = h2_ref[...]
        sub = lax.broadcasted_iota(jnp.int32, (8, 1), 0)
    else:
        n_ext = tm + HALO
        row = lax.broadcasted_iota(jnp.int32, (HALO, 1), 0)
        zero = jnp.zeros((), BF16)
        halo = jnp.where(row < HALO // 2, jnp.where(last, zero, hn_ref[...]), jnp.where(first, zero, hp_ref[...]))
        he = jnp.concatenate([h2_ref[...], halo], axis=0)

    inner = slice(0, tm)
    n_chunks = D_FF // FFN_COLS

    def cols(cc, half):
        return slice(half * D_FF + cc * FFN_COLS, half * D_FF + (cc + 1) * FFN_COLS)

    def up_proj(cc):
        return [jnp.dot(he, wup_ref[:, cols(cc, half)], preferred_element_type=F32) for half in range(2)]

    def conv3(up, cs):
        before = pltpu.roll(up, 1, axis=0)[inner]
        after = pltpu.roll(up, n_ext - 1, axis=0)[inner]
        if whole_seq:
            before = jnp.concatenate([jnp.where(sub == 0, 0.0, before[0:8]), before[8:]], axis=0)
            after = jnp.concatenate([after[:tm - 8], jnp.where(sub == 7, 0.0, after[tm - 8:])], axis=0)
        return (fb_ref[:, cs] + before * fw_ref[0:1, cs] + up[inner] * fw_ref[1:2, cs]
                + after * fw_ref[2:3, cs])

    f = None
    ups = [up_proj(cc) for cc in range(FFN_AHEAD)]
    for cc in range(n_chunks):
        up = ups[cc]
        if cc + FFN_AHEAD < n_chunks:
            ups.append(up_proj(cc + FFN_AHEAD))
        a, gl = conv3(up[0], cols(cc, 0)), conv3(up[1], cols(cc, 1))
        act = ((a * _sigmoid(a)) * gl).astype(BF16)
        part = jnp.dot(act, wdn_ref[cc * FFN_COLS:(cc + 1) * FFN_COLS, :], preferred_element_type=F32)
        f = part if f is None else f + part
    y_ref[...] = x1_ref[...] + g2 * _rms(f, n2p_ref[...])


def _ffn_seqs_kernel(x1_ref, h2_ref, hp_ref, hn_ref, *rest, seqs, tm, **kw):
    *consts, y_ref = rest

    def one(j, carry):
        rows = pl.ds(pl.multiple_of(j * tm, tm), tm)
        _ffn_kernel(x1_ref.at[rows], h2_ref.at[rows], hp_ref, hn_ref, *consts, y_ref.at[rows], tm=tm, **kw)
        return carry

    lax.fori_loop(0, seqs, one, 0)


def _ffn(x1, h2, mod, wup, fw, fb, wdn, n2p, *, seq, tm, per_seq_mod):
    t = x1.shape[0]
    tiles_per_seq = seq // tm
    if tiles_per_seq == 1 and (t // tm) % FFN_SEQS == 0:
        body = functools.partial(_ffn_seqs_kernel, seqs=FFN_SEQS, tm=tm, tiles_per_seq=1, per_seq_mod=per_seq_mod)
        tm = tm * FFN_SEQS
    else:
        body = functools.partial(_ffn_kernel, tm=tm, tiles_per_seq=tiles_per_seq, per_seq_mod=per_seq_mod)
    hb = tm // HALO
    n_hb = t // HALO
    tile = pl.BlockSpec((tm, D_MODEL), lambda i: (i, 0))
    prev = pl.BlockSpec((HALO, D_MODEL), lambda i: (jnp.maximum(i * hb - 1, 0), 0))
    nxt = pl.BlockSpec((HALO, D_MODEL), lambda i: (jnp.minimum((i + 1) * hb, n_hb - 1), 0))
    return pl.pallas_call(
        body,
        grid=(t // tm,),
        in_specs=[tile, tile, prev, nxt, _resident(mod.shape), _resident(wup.shape),
                  _resident(fw.shape), _resident(fb.shape), _resident(wdn.shape), _resident((1, D_MODEL))],
        out_specs=tile,
        out_shape=jax.ShapeDtypeStruct((t, D_MODEL), F32),
        compiler_params=_params(("arbitrary",)),
        name="ffn_lat" if per_seq_mod else "ffn_ctx",
    )(x1, h2, h2, h2, mod, wup, fw, fb, wdn, n2p)


def kernel(x_prompt, x_sample, state_ret_fwd, state_ret_bwd, c, c_ctx, norm1_pre, norm1_post, norm2_pre, norm2_post, ada_w, ada_b, w_in, ret_decay_fwd, ret_decay_bwd, ret_gn_w, ret_w_out, conv_dw_w, conv_dw_b, conv_ln_w, conv_ln_b, conv_w_out, w_mix_out, ffn_w_up, ffn_dw_w, ffn_dw_b, ffn_w_down):
    batch, seq, _ = x_prompt.shape
    dec_batch, dec_seq, _ = x_sample.shape
    assert norm1_pre.shape[0] == 1 and dec_batch + 1 <= 8

    cond8 = jnp.zeros((8, D_MODEL), F32).at[0].set(c_ctx).at[1:1 + dec_batch].set(c)
    mod, w_in_b = _adaln(cond8, ada_w[0], ada_b[0], w_in[0])
    tabs = _decay_tables(ret_decay_fwd[0], ret_decay_bwd[0])

    cw = jnp.broadcast_to(conv_dw_w[0][:, None, :], (CONV_WIDTH, 8, D_MODEL))
    cb = jnp.broadcast_to(conv_dw_b, (8, D_MODEL))
    fw = jnp.zeros((8, 2 * D_FF), F32).at[:FFN_CONV_WIDTH].set(ffn_dw_w[0])
    fb = ffn_dw_b
    conv = (cw, cb, conv_ln_w, conv_ln_b)

    xp = x_prompt.reshape(batch * seq, D_MODEL)
    later = (ret_w_out[0], conv_w_out[0], w_mix_out[0], ffn_w_up[0], ffn_w_down[0])
    ro, cu, gr, gc, sf, sb, wro, wco, wmx, wup, wdn = _proj(
        xp, mod, norm1_pre, w_in_b, *conv, latent=False, seq=seq, tm=seq, gn_w=ret_gn_w, tabs=tabs, later=later)

    def tail(x, ro, cu, gr, gc, *, latent, seq_len):
        x1, h2 = _mixer(x, ro, cu, gr, gc, mod, wro, wco, wmx, norm1_post, norm2_pre,
                        seq=seq_len, tm=512, per_seq_mod=latent)
        return _ffn(x1, h2, mod, wup, fw, fb, wdn, norm2_post, seq=seq_len, tm=256, per_seq_mod=latent)

    yp = tail(xp, ro, cu, gr, gc, latent=False, seq_len=seq)

    xs = x_sample.reshape(dec_batch * dec_seq, D_MODEL)
    q, k, v, sg, cu, gr, gc = _proj(xs, mod, norm1_pre, w_in_b, *conv, latent=True, seq=dec_seq, tm=256)
    ro = _ret_lat(q, k, v, sg, ret_gn_w, state_ret_fwd, state_ret_bwd, tabs, dec_batch, dec_seq)
    ys = tail(xs, ro, cu, gr, gc, latent=True, seq_len=dec_seq)
    return (yp.reshape(batch, seq, D_MODEL), ys.reshape(dec_batch, dec_seq, D_MODEL), sf, sb)
```
